```python
import jax, jax.numpy as jnp
from jax import lax
import numpy as np

D_MODEL = 4096
BATCH = 2
SEQ = 4096
DEPTH = 2

CHUNK = 64
Q_BLOCK = 128
EPS = 1e-6
D_FF = 2 * D_MODEL
MIX_W = D_MODEL // 2
N_BRANCH = 3
N_MOD = 9

POOL_WINDOWS = (2, 4, 8, 16)
POOL_GROUPS = len(POOL_WINDOWS)
POOL_GW = MIX_W // POOL_GROUPS

MLA_NOPE = 128
MLA_ROPE = 64
MLA_V = 128
MLA_HEADS = MIX_W // MLA_V
MLA_Q_RANK = D_MODEL // 4
MLA_KV_RANK = 512
MLA_SCALE = (MLA_NOPE + MLA_ROPE) ** -0.5
ROPE_THETA = 10000.0

ML_HEADS = 8
ML_DK = 128
ML_DV = MIX_W // ML_HEADS
CONV_W = 4

IN_SPLITS = (
    MIX_W,
    MLA_Q_RANK,
    MLA_KV_RANK,
    MLA_ROPE,
    ML_HEADS * ML_DK,
    ML_HEADS * ML_DK,
    ML_HEADS * ML_DV,
    ML_HEADS * ML_DV,
    ML_HEADS,
    ML_HEADS,
    N_BRANCH * D_MODEL,
)
IN_OFFSETS = tuple(int(v) for v in np.cumsum(IN_SPLITS)[:-1])
N_IN = int(sum(IN_SPLITS))

kernel_name = "hybrid_pool_mla_mlstm_macaron_adaln"


def rmsnorm(x, w=None):
    x32 = x.astype(jnp.float32)
    y = x32 * lax.rsqrt(jnp.mean(x32 * x32, axis=-1, keepdims=True) + EPS)
    if w is not None:
        y = y * w.astype(jnp.float32)
    return y.astype(x.dtype)


def modulate(x, shift, scale):
    return rmsnorm(x) * (1 + scale[:, None, :]) + shift[:, None, :]


def swiglu(h, w_in, w_out):
    g, u = jnp.split(h @ w_in, 2, axis=-1)
    return (jax.nn.silu(g) * u) @ w_out


def rope_cos_sin(positions, dim):
    inv = ROPE_THETA ** (-jnp.arange(0, dim, 2, dtype=jnp.float32) / dim)
    ang = positions.astype(jnp.float32)[..., None] * inv
    return jnp.cos(ang), jnp.sin(ang)


def apply_rope(x, cos, sin):
    x32 = x.astype(jnp.float32)
    x1, x2 = jnp.split(x32, 2, axis=-1)
    return jnp.concatenate([x1 * cos - x2 * sin, x1 * sin + x2 * cos], axis=-1).astype(x.dtype)


def causal_conv(x, w, b):
    C = x.shape[-1]
    y = lax.conv_general_dilated(
        x, w.astype(x.dtype)[:, None, :], window_strides=(1,), padding=((CONV_W - 1, 0),),
        dimension_numbers=("NWC", "WIO", "NWC"), feature_group_count=C)
    return y + b


def pool_mixer(u, pool_w, pool_scale):
    B, S, _ = u.shape
    u32 = u.astype(jnp.float32)
    cs0 = jnp.pad(jnp.cumsum(u32, axis=1), ((0, 0), (1, 0), (0, 0)))
    t = jnp.arange(S)
    outs = []
    for g, w in enumerate(POOL_WINDOWS):
        sl = cs0[..., g * POOL_GW:(g + 1) * POOL_GW]
        lag = jnp.pad(sl, ((0, 0), (w - 1, 0), (0, 0)))[:, :S]
        cnt = jnp.minimum(t + 1, w).astype(jnp.float32)[None, :, None]
        outs.append((sl[:, 1:] - lag) / cnt - u32[..., g * POOL_GW:(g + 1) * POOL_GW])
    p = jnp.stack(outs, axis=2).astype(u.dtype)
    y = jnp.einsum("bsgc,gcd->bsgd", p, pool_w).reshape(B, S, MIX_W)
    return y * pool_scale


def mla_mixer(cq, ckv, krope, positions, q_norm, w_uq, kv_norm, w_ukv):
    B, S, _ = cq.shape
    q = (rmsnorm(cq, q_norm) @ w_uq).reshape(B, S, MLA_HEADS, MLA_NOPE + MLA_ROPE)
    kv = (rmsnorm(ckv, kv_norm) @ w_ukv).reshape(B, S, MLA_HEADS, MLA_NOPE + MLA_V)
    q_nope, q_rope = q[..., :MLA_NOPE], q[..., MLA_NOPE:]
    k_nope, v = kv[..., :MLA_NOPE], kv[..., MLA_NOPE:]
    cos, sin = rope_cos_sin(positions, MLA_ROPE)
    q_rope = apply_rope(q_rope, cos[:, :, None, :], sin[:, :, None, :])
    k_rope = apply_rope(krope, cos, sin)
    chunk_id = positions // CHUNK

    def attend_block(i):
        start = i * Q_BLOCK
        qn = lax.dynamic_slice_in_dim(q_nope, start, Q_BLOCK, axis=1)
        qr = lax.dynamic_slice_in_dim(q_rope, start, Q_BLOCK, axis=1)
        cid_q = lax.dynamic_slice_in_dim(chunk_id, start, Q_BLOCK, axis=1)
        s = (jnp.einsum("bqhd,bkhd->bhqk", qn, k_nope)
             + jnp.einsum("bqhd,bkd->bhqk", qr, k_rope)).astype(jnp.float32) * MLA_SCALE
        mask = chunk_id[:, None, None, :] <= cid_q[:, None, :, None]
        s = jnp.where(mask, s, -jnp.inf)
        p = jax.nn.softmax(s, axis=-1).astype(v.dtype)
        return jnp.einsum("bhqk,bkhd->bqhd", p, v)

    o = lax.map(attend_block, jnp.arange(S // Q_BLOCK))
    return o.transpose(1, 0, 2, 3, 4).reshape(B, S, MLA_HEADS * MLA_V)


def mlstm_chunkwise(q, k, v, logi, logf):
    B, S, NH, DK = q.shape
    DV = v.shape[-1]
    nc = S // CHUNK

    def to_chunks(a):
        return a.reshape(B, nc, CHUNK, NH, a.shape[-1]).transpose(1, 0, 3, 2, 4)

    def gate_chunks(a):
        return a.reshape(B, nc, CHUNK, NH).transpose(1, 0, 3, 2)

    qc, kc, vc = to_chunks(q * (DK ** -0.5)), to_chunks(k), to_chunks(v)
    lic = gate_chunks(logi)
    bc = jnp.cumsum(gate_chunks(logf), axis=-1)
    tril = jnp.tril(jnp.ones((CHUNK, CHUNK), dtype=bool))

    def step(carry, inp):
        C, n, m = carry
        qi, ki, vi, li, bi = inp
        dlog = bi[..., :, None] - bi[..., None, :] + li[..., None, :]
        dlog = jnp.where(tril, dlog, -jnp.inf)
        inter = bi + m[..., None]
        m_t = jnp.maximum(inter, jnp.max(dlog, axis=-1))
        w_inter = jnp.exp(inter - m_t)
        s = jnp.einsum("bhtd,bhsd->bhts", qi, ki) * jnp.exp(dlog - m_t[..., None])
        num = (jnp.einsum("bhts,bhsv->bhtv", s, vi)
               + w_inter[..., None] * jnp.einsum("bhvd,bhtd->bhtv", C, qi))
        den = jnp.sum(s, axis=-1) + w_inter * jnp.einsum("bhd,bhtd->bht", n, qi)
        h = num / jnp.maximum(jnp.abs(den), jnp.exp(-m_t))[..., None]
        b_last = bi[..., -1]
        g = b_last[..., None] - bi + li
        m_new = jnp.maximum(b_last + m, jnp.max(g, axis=-1))
        decay = jnp.exp(b_last + m - m_new)
        ws = jnp.exp(g - m_new[..., None])
        C = decay[..., None, None] * C + jnp.einsum("bhs,bhsv,bhsd->bhvd", ws, vi, ki)
        n = decay[..., None] * n + jnp.einsum("bhs,bhsd->bhd", ws, ki)
        return (C, n, m_new), h

    init = (jnp.zeros((B, NH, DV, DK), jnp.float32), jnp.zeros((B, NH, DK), jnp.float32),
            jnp.zeros((B, NH), jnp.float32))
    _, h = lax.scan(step, init, (qc, kc, vc, lic, bc))
    return h.transpose(1, 0, 3, 2, 4).reshape(B, S, NH, DV)


def mlstm_mixer(mq, mk, mv, mo, mi, mf, conv_w, conv_b, gate_b, head_norm):
    B, S, _ = mq.shape
    qk = jax.nn.silu(causal_conv(jnp.concatenate([mq, mk], axis=-1), conv_w, conv_b))
    q, k = jnp.split(qk, 2, axis=-1)
    f32 = jnp.float32
    q = q.astype(f32).reshape(B, S, ML_HEADS, ML_DK)
    k = k.astype(f32).reshape(B, S, ML_HEADS, ML_DK)
    v = mv.astype(f32).reshape(B, S, ML_HEADS, ML_DV)
    logi = mi.astype(f32) + gate_b[:ML_HEADS].astype(f32)
    logf = jax.nn.log_sigmoid(mf.astype(f32) + gate_b[ML_HEADS:].astype(f32))
    h = mlstm_chunkwise(q, k, v, logi, logf).astype(mq.dtype)
    h = rmsnorm(h, head_norm.reshape(ML_HEADS, ML_DV)).reshape(B, S, MIX_W)
    return jax.nn.sigmoid(mo) * h


def hybrid_mixer(h, positions, w_in, pool_w, pool_scale, mla_q_norm, mla_w_uq, mla_kv_norm,
                 mla_w_ukv, ml_conv_w, ml_conv_b, ml_gate_b, ml_head_norm, w_branch, w_out):
    B, S, _ = h.shape
    z = h @ w_in
    (u_pool, cq, ckv, krope, mq, mk, mv, mo, mi, mf, gpre) = jnp.split(z, IN_OFFSETS, axis=-1)
    y_a = pool_mixer(u_pool, pool_w, pool_scale)
    y_b = mla_mixer(cq, ckv, krope, positions, mla_q_norm, mla_w_uq, mla_kv_norm, mla_w_ukv)
    y_c = mlstm_mixer(mq, mk, mv, mo, mi, mf, ml_conv_w, ml_conv_b, ml_gate_b, ml_head_norm)
    g = jax.nn.sigmoid(gpre.astype(jnp.float32)).astype(h.dtype).reshape(B, S, N_BRANCH, D_MODEL)
    merged = (g[:, :, 0] * (y_a @ w_branch[0])
              + g[:, :, 1] * (y_b @ w_branch[1])
              + g[:, :, 2] * (y_c @ w_branch[2]))
    return merged @ w_out


def setup_inputs(seed: int = 0) -> dict:
    key = jax.random.key(seed)
    ks = jax.random.split(key, 26)

    def nrm(k, shape, scale):
        return scale * jax.random.normal(k, shape, jnp.float32)

    x = nrm(ks[0], (BATCH, SEQ, D_MODEL), 1.0)
    c = nrm(ks[1], (BATCH, D_MODEL), 1.0)
    offset = CHUNK * jax.random.randint(ks[2], (BATCH, 1), 0, 64)
    positions = (offset + jnp.arange(SEQ, dtype=jnp.int32)[None, :]).astype(jnp.int32)
    w_ada = nrm(ks[3], (D_MODEL, N_MOD * D_MODEL), D_MODEL ** -0.5)
    b_ada = nrm(ks[4], (N_MOD * D_MODEL,), 0.02)
    ada_table = nrm(ks[5], (DEPTH, N_MOD, D_MODEL), 0.1)
    ffn_a_w_in = nrm(ks[6], (DEPTH, D_MODEL, 2 * D_FF), D_MODEL ** -0.5)
    ffn_a_w_out = nrm(ks[7], (DEPTH, D_FF, D_MODEL), D_FF ** -0.5)
    w_mix_in = nrm(ks[8], (DEPTH, D_MODEL, N_IN), D_MODEL ** -0.5)
    pool_w = nrm(ks[9], (DEPTH, POOL_GROUPS, POOL_GW, POOL_GW), POOL_GW ** -0.5)
    pool_scale = 1.0 + nrm(ks[10], (DEPTH, MIX_W), 0.1)
    mla_q_norm = 1.0 + nrm(ks[11], (DEPTH, MLA_Q_RANK), 0.05)
    mla_w_uq = nrm(ks[12], (DEPTH, MLA_Q_RANK, MLA_HEADS * (MLA_NOPE + MLA_ROPE)), MLA_Q_RANK ** -0.5)
    mla_kv_norm = 1.0 + nrm(ks[13], (DEPTH, MLA_KV_RANK), 0.05)
    mla_w_ukv = nrm(ks[14], (DEPTH, MLA_KV_RANK, MLA_HEADS * (MLA_NOPE + MLA_V)), MLA_KV_RANK ** -0.5)
    ml_conv_w = nrm(ks[15], (DEPTH, CONV_W, 2 * ML_HEADS * ML_DK), CONV_W ** -0.5)
    ml_conv_b = nrm(ks[16], (DEPTH, 2 * ML_HEADS * ML_DK), 0.02)
    i_bias = nrm(ks[17], (DEPTH, ML_HEADS), 0.1)
    f_bias = jnp.linspace(3.0, 6.0, ML_HEADS, dtype=jnp.float32)[None, :] + nrm(ks[18], (DEPTH, ML_HEADS), 0.1)
    ml_gate_b = jnp.concatenate([i_bias, f_bias], axis=-1)
    ml_head_norm = 1.0 + nrm(ks[19], (DEPTH, MIX_W), 0.05)
    w_branch = nrm(ks[20], (DEPTH, N_BRANCH, MIX_W, D_MODEL), MIX_W ** -0.5)
    w_out = nrm(ks[21], (DEPTH, D_MODEL, D_MODEL), D_MODEL ** -0.5)
    ffn_b_w_in = nrm(ks[22], (DEPTH, D_MODEL, 2 * D_FF), D_MODEL ** -0.5)
    ffn_b_w_out = nrm(ks[23], (DEPTH, D_FF, D_MODEL), D_FF ** -0.5)
    final_norm = 1.0 + nrm(ks[24], (D_MODEL,), 0.05)
    return {
        "x": x, "c": c, "positions": positions,
        "w_ada": w_ada, "b_ada": b_ada, "ada_table": ada_table,
        "ffn_a_w_in": ffn_a_w_in, "ffn_a_w_out": ffn_a_w_out,
        "w_mix_in": w_mix_in, "pool_w": pool_w, "pool_scale": pool_scale,
        "mla_q_norm": mla_q_norm, "mla_w_uq": mla_w_uq, "mla_kv_norm": mla_kv_norm, "mla_w_ukv": mla_w_ukv,
        "ml_conv_w": ml_conv_w, "ml_conv_b": ml_conv_b, "ml_gate_b": ml_gate_b, "ml_head_norm": ml_head_norm,
        "w_branch": w_branch, "w_out": w_out,
        "ffn_b_w_in": ffn_b_w_in, "ffn_b_w_out": ffn_b_w_out,
        "final_norm": final_norm,
    }


def reference(x, c, positions, w_ada, b_ada, ada_table, ffn_a_w_in, ffn_a_w_out, w_mix_in, pool_w,
              pool_scale, mla_q_norm, mla_w_uq, mla_kv_norm, mla_w_ukv, ml_conv_w, ml_conv_b, ml_gate_b,
              ml_head_norm, w_branch, w_out, ffn_b_w_in, ffn_b_w_out, final_norm):
    B = x.shape[0]
    mod_shared = (jax.nn.silu(c) @ w_ada + b_ada).reshape(B, N_MOD, D_MODEL)
    for l in range(DEPTH):
        mod = mod_shared + ada_table[l][None]
        h = modulate(x, mod[:, 0], mod[:, 1])
        x = x + 0.5 * mod[:, 2][:, None, :] * swiglu(h, ffn_a_w_in[l], ffn_a_w_out[l])
        h = modulate(x, mod[:, 3], mod[:, 4])
        y = hybrid_mixer(h, positions, w_mix_in[l], pool_w[l], pool_scale[l], mla_q_norm[l], mla_w_uq[l],
                         mla_kv_norm[l], mla_w_ukv[l], ml_conv_w[l], ml_conv_b[l], ml_gate_b[l],
                         ml_head_norm[l], w_branch[l], w_out[l])
        x = x + mod[:, 5][:, None, :] * y
        h = modulate(x, mod[:, 6], mod[:, 7])
        x = x + 0.5 * mod[:, 8][:, None, :] * swiglu(h, ffn_b_w_in[l], ffn_b_w_out[l])
    return rmsnorm(x, final_norm)
```

```python
import functools

import numpy as np
import jax
import jax.numpy as jnp
from jax import lax
from jax.experimental import pallas as pl
from jax.experimental.pallas import tpu as pltpu

F32 = jnp.float32
BF16 = jnp.bfloat16

D_MODEL = 4096
DEPTH = 2
CHUNK = 64
EPS = 1e-6
D_FF = 2 * D_MODEL
MIX_W = D_MODEL // 2
N_BRANCH = 3
N_MOD = 9
POOL_WINDOWS = (2, 4, 8, 16)
POOL_GW = MIX_W // len(POOL_WINDOWS)
MLA_NOPE = 128
MLA_ROPE = 64
MLA_V = 128
MLA_HEADS = MIX_W // MLA_V
MLA_Q_RANK = D_MODEL // 4
MLA_KV_RANK = 512
MLA_SCALE = (MLA_NOPE + MLA_ROPE) ** -0.5
ROPE_THETA = 10000.0
ML_HEADS = 8
ML_DK = 128
ML_DV = MIX_W // ML_HEADS
CONV_W = 4

_SPLITS = (MIX_W, MLA_Q_RANK, MLA_KV_RANK, MLA_ROPE, ML_HEADS * ML_DK, ML_HEADS * ML_DK,
           ML_HEADS * ML_DV, ML_HEADS * ML_DV, ML_HEADS, ML_HEADS, N_BRANCH * D_MODEL)
_OFF = tuple(int(v) for v in np.cumsum((0,) + _SPLITS))

LANES = 128
HALO = 16
VMEM_LIMIT = 60 * 1024 * 1024


def _params(sem):
    return pltpu.CompilerParams(dimension_semantics=sem, vmem_limit_bytes=VMEM_LIMIT)


def _mm_body(*refs, nw, ne, no, nk, epilogue):
    x_ref = refs[0]
    w_refs = refs[1:1 + nw]
    e_refs = refs[1 + nw:1 + nw + ne]
    o_refs = refs[1 + nw + ne:1 + nw + ne + no]
    acc_refs = refs[1 + nw + ne + no:]
    if nk == 1:
        accs = [jnp.dot(x_ref[...], w[...], preferred_element_type=F32) for w in w_refs]
        epilogue(accs, e_refs, o_refs)
        return
    k = pl.program_id(2)

    @pl.when(k == 0)
    def _():
        for a, w in zip(acc_refs, w_refs):
            a[...] = jnp.dot(x_ref[...], w[...], preferred_element_type=F32)

    @pl.when(k > 0)
    def _():
        for a, w in zip(acc_refs, w_refs):
            a[...] += jnp.dot(x_ref[...], w[...], preferred_element_type=F32)

    @pl.when(k == nk - 1)
    def _():
        epilogue([a[...] for a in acc_refs], e_refs, o_refs)


def _mm(x, ws, extras, outs, epilogue, *, tm, tn, nj, nk=1):
    m, kdim = x.shape
    tk = kdim // nk
    in_specs = [pl.BlockSpec((tm, tk), lambda i, j, k: (i, k))]
    for _, off in ws:
        in_specs.append(pl.BlockSpec((tk, tn), lambda i, j, k, off=off: (k, off + j)))
    for _, blk, f in extras:
        in_specs.append(pl.BlockSpec(blk, lambda i, j, k, f=f: f(i, j)))
    out_specs = [pl.BlockSpec(blk, lambda i, j, k, f=f: f(i, j)) for _, _, blk, f in outs]
    out_shape = [jax.ShapeDtypeStruct(s, d) for s, d, _, _ in outs]
    scratch = [pltpu.VMEM((tm, tn), F32) for _ in ws] if nk > 1 else []
    body = functools.partial(_mm_body, nw=len(ws), ne=len(extras), no=len(outs), nk=nk,
                             epilogue=epilogue)
    res = pl.pallas_call(
        body, grid=(m // tm, nj, nk), in_specs=in_specs, out_specs=out_specs,
        out_shape=out_shape, scratch_shapes=scratch,
        compiler_params=_params(("parallel", "parallel", "arbitrary")),
    )(x, *[w for w, _ in ws], *[a for a, _, _ in extras])
    return res


def _rope128(v, cos, sina, sinb):
    return (v * cos + pltpu.roll(v, LANES - MLA_ROPE // 2, 1) * sina
            + pltpu.roll(v, MLA_ROPE // 2, 1) * sinb)


def _ada_kernel(c_ref, w_ref, b_ref, t_ref, o_ref):
    c = c_ref[...]
    s = c * jax.nn.sigmoid(c)
    acc = jnp.dot(s.astype(BF16), w_ref[...].astype(BF16), preferred_element_type=F32) + b_ref[...]
    for l in range(DEPTH):
        o_ref[l] = acc + t_ref[l]


def _ada(c, w_ada, b_ada, ada_table):
    b = c.shape[0]
    rows = 8
    c8 = jnp.zeros((rows, D_MODEL), F32).at[:b].set(c)
    n = N_MOD * D_MODEL
    tn = 1024
    out = pl.pallas_call(
        _ada_kernel, grid=(n // tn,),
        in_specs=[pl.BlockSpec((rows, D_MODEL), lambda j: (0, 0)),
                  pl.BlockSpec((D_MODEL, tn), lambda j: (0, j)),
                  pl.BlockSpec((1, tn), lambda j: (0, j)),
                  pl.BlockSpec((DEPTH, 1, tn), lambda j: (0, 0, j))],
        out_specs=pl.BlockSpec((DEPTH, rows, tn), lambda j: (0, 0, j)),
        out_shape=jax.ShapeDtypeStruct((DEPTH, rows, n), F32),
        compiler_params=_params(("parallel",)),
    )(c8, w_ada, b_ada.reshape(1, n), ada_table.reshape(DEPTH, 1, n))
    return out[:, :b].reshape(DEPTH, b, N_MOD, D_MODEL)


def _normmod_kernel(x_ref, shift_ref, scale_ref, o_ref):
    x = x_ref[0]
    y = x * lax.rsqrt(jnp.mean(x * x, axis=-1, keepdims=True) + EPS)
    o_ref[0] = (y * (1.0 + scale_ref[0]) + shift_ref[0]).astype(o_ref.dtype)


def _finalnorm_kernel(x_ref, w_ref, o_ref):
    x = x_ref[0]
    y = x * lax.rsqrt(jnp.mean(x * x, axis=-1, keepdims=True) + EPS)
    o_ref[0] = y * w_ref[...]


def _normmod(x, shift, scale, ts=512):
    b, s, d = x.shape
    vec = pl.BlockSpec((1, 1, d), lambda bi, i: (bi, 0, 0))
    return pl.pallas_call(
        _normmod_kernel, grid=(b, s // ts),
        in_specs=[pl.BlockSpec((1, ts, d), lambda bi, i: (bi, i, 0)), vec, vec],
        out_specs=pl.BlockSpec((1, ts, d), lambda bi, i: (bi, i, 0)),
        out_shape=jax.ShapeDtypeStruct((b, s, d), BF16),
        compiler_params=_params(("parallel", "parallel")),
    )(x, shift.reshape(b, 1, d), scale.reshape(b, 1, d))


def _finalnorm(x, w, ts=512):
    b, s, d = x.shape
    return pl.pallas_call(
        _finalnorm_kernel, grid=(b, s // ts),
        in_specs=[pl.BlockSpec((1, ts, d), lambda bi, i: (bi, i, 0)),
                  pl.BlockSpec((1, d), lambda bi, i: (0, 0))],
        out_specs=pl.BlockSpec((1, ts, d), lambda bi, i: (bi, i, 0)),
        out_shape=jax.ShapeDtypeStruct((b, s, d), F32),
        compiler_params=_params(("parallel", "parallel")),
    )(x, w.reshape(1, d))


def _swiglu_epilogue(accs, e_refs, o_refs):
    g, u = accs
    o_refs[0][...] = (g * jax.nn.sigmoid(g) * u).astype(BF16)


def _resid_epilogue(accs, e_refs, o_refs, *, coef):
    x_ref, g_ref = e_refs
    o_refs[0][...] = x_ref[...] + (coef * g_ref[0]) * accs[0]


def _resid_mm(a, w, x2d, gate, coef, seq, *, tm=1024, tn=512, nk=1):
    m, n = x2d.shape
    per_b = seq // tm
    return _mm(a, [(w, 0)],
               [(x2d, (tm, tn), lambda i, j: (i, j)),
                (gate.reshape(-1, 1, n), (1, 1, tn), lambda i, j: (i // per_b, 0, j))],
               [((m, n), F32, (tm, tn), lambda i, j: (i, j))],
               functools.partial(_resid_epilogue, coef=coef), tm=tm, tn=tn, nj=n // tn, nk=nk)[0]


def _ffn(x, h, w_in, w_out, gate, *, tm=1024, tn=512):
    b, s, d = x.shape
    m = b * s
    a = _mm(h.reshape(m, d), [(w_in, 0), (w_in, D_FF // tn)], [],
            [((m, D_FF), BF16, (tm, tn), lambda i, j: (i, j))],
            _swiglu_epilogue, tm=tm, tn=tn, nj=D_FF // tn)[0]
    y = _resid_mm(a, w_out, x.reshape(m, d), gate, 0.5, s, tm=tm, tn=tn, nk=2)
    return y.reshape(b, s, d)


def _rope_tab_kernel(p_ref, inv_ref, cos_ref, sina_ref, sinb_ref):
    ang = p_ref[...].astype(F32) * inv_ref[...]
    lane = lax.broadcasted_iota(jnp.int32, ang.shape, 1)
    half = MLA_ROPE // 2
    c = jnp.cos(ang)
    s = jnp.sin(ang)
    cos_ref[...] = jnp.where(lane < MLA_ROPE, c, 0.0)
    sina_ref[...] = jnp.where(lane < half, -s, 0.0)
    sinb_ref[...] = jnp.where(lane >= half, jnp.where(lane < MLA_ROPE, s, 0.0), 0.0)


def _rope_tables(positions, ts=512):
    m = positions.size
    half = MLA_ROPE // 2
    inv = ROPE_THETA ** (-jnp.arange(0, MLA_ROPE, 2, dtype=F32) / MLA_ROPE)
    inv128 = jnp.concatenate([inv, inv, jnp.zeros((LANES - 2 * half,), F32)]).reshape(1, LANES)
    spec = pl.BlockSpec((ts, LANES), lambda i: (i, 0))
    shp = jax.ShapeDtypeStruct((m, LANES), F32)
    return pl.pallas_call(
        _rope_tab_kernel, grid=(m // ts,),
        in_specs=[pl.BlockSpec((ts, 1), lambda i: (i, 0)), pl.BlockSpec((1, LANES), lambda i: (0, 0))],
        out_specs=[spec, spec, spec], out_shape=[shp, shp, shp],
        compiler_params=_params(("parallel",)),
    )(positions.reshape(m, 1), inv128)


def _plain_epilogue(accs, e_refs, o_refs):
    o_refs[0][...] = accs[0].astype(o_refs[0].dtype)


def _sigmoid_epilogue(accs, e_refs, o_refs):
    o_refs[0][...] = jax.nn.sigmoid(accs[0]).astype(o_refs[0].dtype)


def _rmsw_epilogue(accs, e_refs, o_refs):
    a = accs[0]
    y = a * lax.rsqrt(jnp.mean(a * a, axis=-1, keepdims=True) + EPS) * e_refs[0][...]
    o_refs[0][...] = y.astype(o_refs[0].dtype)


def _kvlatent_epilogue(accs, e_refs, o_refs):
    a = accs[0]
    w_ref, cos_ref, sina_ref, sinb_ref = e_refs
    ckv = a[:, :MLA_KV_RANK]
    y = ckv * lax.rsqrt(jnp.mean(ckv * ckv, axis=-1, keepdims=True) + EPS) * w_ref[...]
    o_refs[0][...] = y.astype(BF16)
    kr = a[:, MLA_KV_RANK:MLA_KV_RANK + LANES]
    o_refs[1][...] = _rope128(kr, cos_ref[...], sina_ref[...], sinb_ref[...]).astype(BF16)
    o_refs[2][...] = a[:, MLA_KV_RANK + LANES:]


def _qrope_epilogue(accs, e_refs, o_refs, *, heads_per_tile):
    a = accs[0] * MLA_SCALE
    cos_ref, sina_ref, sinb_ref = e_refs
    cos, sina, sinb = cos_ref[...], sina_ref[...], sinb_ref[...]
    for c in range(heads_per_tile):
        sl = slice(c * LANES, (c + 1) * LANES)
        o_refs[0][:, sl] = _rope128(a[:, sl], cos, sina, sinb).astype(BF16)


def _scaled_epilogue(accs, e_refs, o_refs, *, scale):
    o_refs[0][...] = (accs[0] * scale).astype(o_refs[0].dtype)


def _band(ts, lo, hi, first_tile):
    t = lax.broadcasted_iota(jnp.int32, (ts, HALO + ts), 0)
    s = lax.broadcasted_iota(jnp.int32, (ts, HALO + ts), 1)
    d = t + HALO - s
    ok = jnp.where(d >= lo, jnp.where(d < hi, 1.0, 0.0), 0.0)
    ok = jnp.where(s < HALO, jnp.where(first_tile, 0.0, ok), ok)
    return ok.astype(BF16)


def _pool_kernel(u_ref, halo_ref, pw_ref, ps_ref, o_ref, *, ts):
    i = pl.program_id(1)
    u = u_ref[0]
    ucat = jnp.concatenate([halo_ref[0], u], axis=0)
    tg = i * ts + lax.broadcasted_iota(jnp.int32, (ts, 1), 0)
    for g, w in enumerate(POOL_WINDOWS):
        sl = slice(g * POOL_GW, (g + 1) * POOL_GW)
        win = jnp.dot(_band(ts, 0, w, i == 0), ucat[:, sl], preferred_element_type=F32)
        cnt = jnp.minimum(tg + 1, w).astype(F32)
        p = win / cnt - u[:, sl].astype(F32)
        y = jnp.dot(p.astype(BF16), pw_ref[g], preferred_element_type=F32)
        o_ref[0, :, sl] = (y * ps_ref[:, sl]).astype(BF16)


def _pool(zp, pool_w, pool_scale, b, s, ts=256):
    hb = ts // HALO
    return pl.pallas_call(
        functools.partial(_pool_kernel, ts=ts), grid=(b, s // ts),
        in_specs=[pl.BlockSpec((1, ts, MIX_W), lambda bi, i: (bi, i, 0)),
                  pl.BlockSpec((1, HALO, MIX_W), lambda bi, i: (bi, jnp.maximum(i * hb - 1, 0), 0)),
                  pl.BlockSpec((len(POOL_WINDOWS), POOL_GW, POOL_GW), lambda bi, i: (0, 0, 0)),
                  pl.BlockSpec((1, MIX_W), lambda bi, i: (0, 0))],
        out_specs=pl.BlockSpec((1, ts, MIX_W), lambda bi, i: (bi, i, 0)),
        out_shape=jax.ShapeDtypeStruct((b, s, MIX_W), BF16),
        compiler_params=_params(("parallel", "parallel")),
    )(zp, zp, pool_w, pool_scale.reshape(1, MIX_W))


def _conv_kernel(x_ref, halo_ref, w_ref, b_ref, sc_ref, o_ref, *, ts):
    i = pl.program_id(1)
    x = x_ref[0]
    xcat = jnp.concatenate([halo_ref[0], x], axis=0)
    acc = x.astype(F32) * w_ref[CONV_W - 1:CONV_W, :] + b_ref[...]
    for d in range(1, CONV_W):
        xs = jnp.dot(_band(ts, d, d + 1, i == 0), xcat, preferred_element_type=F32)
        acc = acc + xs * w_ref[CONV_W - 1 - d:CONV_W - d, :]
    o_ref[0] = (acc * jax.nn.sigmoid(acc) * sc_ref[...]).astype(BF16)


def _conv_silu(zp, conv_w, conv_b, b, s, ts=256):
    c = 2 * ML_HEADS * ML_DK
    hb = ts // HALO
    post = jnp.concatenate([jnp.full((c // 2,), ML_DK ** -0.5, F32), jnp.ones((c // 2,), F32)])
    return pl.pallas_call(
        functools.partial(_conv_kernel, ts=ts), grid=(b, s // ts),
        in_specs=[pl.BlockSpec((1, ts, c), lambda bi, i: (bi, i, 1)),
                  pl.BlockSpec((1, HALO, c), lambda bi, i: (bi, jnp.maximum(i * hb - 1, 0), 1)),
                  pl.BlockSpec((CONV_W, c), lambda bi, i: (0, 0)),
                  pl.BlockSpec((1, c), lambda bi, i: (0, 0)),
                  pl.BlockSpec((1, c), lambda bi, i: (0, 0))],
        out_specs=pl.BlockSpec((1, ts, c), lambda bi, i: (bi, i, 0)),
        out_shape=jax.ShapeDtypeStruct((b, s, c), BF16),
        compiler_params=_params(("parallel", "parallel")),
    )(zp, zp, conv_w, conv_b.reshape(1, c), post.reshape(1, c))


def _attn_kernel(qn_ref, qr_ref, kn_ref, kr_ref, v_ref, pq_ref, pk_ref, o_ref, *, tq):
    i = pl.program_id(2)
    q = jnp.concatenate([qn_ref[0], qr_ref[0]], axis=-1)
    nt = (((1,), (1,)), ((), ()))

    def scores(start):
        k = jnp.concatenate([kn_ref[0, pl.ds(start, tq), :], kr_ref[0, pl.ds(start, tq), :]], axis=-1)
        return lax.dot_general(q, k, nt, preferred_element_type=F32)

    d0 = pl.multiple_of(i * tq, tq)
    s = scores(d0)
    sh = CHUNK.bit_length() - 1
    s = jnp.where(lax.shift_right_arithmetic(pk_ref[0], sh) <= lax.shift_right_arithmetic(pq_ref[0], sh),
                  s, -jnp.inf)
    m = jnp.max(s, axis=-1, keepdims=True)
    p = jnp.exp(s - m)
    l = jnp.sum(p, axis=-1, keepdims=True)
    acc = jnp.dot(p.astype(BF16), v_ref[0, pl.ds(d0, tq), :], preferred_element_type=F32)

    def body(j, carry):
        m, l, acc = carry
        start = pl.multiple_of(j * tq, tq)
        s = scores(start)
        m_new = jnp.maximum(m, jnp.max(s, axis=-1, keepdims=True))
        alpha = jnp.exp(m - m_new)
        p = jnp.exp(s - m_new)
        l = alpha * l + jnp.sum(p, axis=-1, keepdims=True)
        acc = alpha * acc + jnp.dot(p.astype(BF16), v_ref[0, pl.ds(start, tq), :],
                                    preferred_element_type=F32)
        return m_new, l, acc

    m, l, acc = lax.fori_loop(0, i, body, (m, l, acc))
    o_ref[0] = (acc / l).astype(o_ref.dtype)


def _attention(qn, qr, kv, kr, positions, tq=256):
    b, s, _ = qn.shape
    h = MLA_HEADS
    return pl.pallas_call(
        functools.partial(_attn_kernel, tq=tq), grid=(b, h, s // tq),
        in_specs=[pl.BlockSpec((1, tq, LANES), lambda bi, hi, i: (bi, i, hi)),
                  pl.BlockSpec((1, tq, LANES), lambda bi, hi, i: (bi, i, hi)),
                  pl.BlockSpec((1, s, LANES), lambda bi, hi, i: (bi, 0, hi)),
                  pl.BlockSpec((1, s, LANES), lambda bi, hi, i: (bi, 0, 0)),
                  pl.BlockSpec((1, s, LANES), lambda bi, hi, i: (bi, 0, h + hi)),
                  pl.BlockSpec((1, tq, 1), lambda bi, hi, i: (bi, i, 0)),
                  pl.BlockSpec((1, 1, tq), lambda bi, hi, i: (bi, 0, i))],
        out_specs=pl.BlockSpec((1, tq, LANES), lambda bi, hi, i: (bi, i, hi)),
        out_shape=jax.ShapeDtypeStruct((b, s, h * MLA_V), BF16),
        compiler_params=_params(("parallel", "parallel", "arbitrary")),
    )(qn, qr, kv, kr, kv, positions.reshape(b, s, 1), positions.reshape(b, 1, s))


def _split3_dot(a_bf16, x):
    hi = x.astype(BF16)
    r1 = x - hi.astype(F32)
    mid = r1.astype(BF16)
    lo = (r1 - mid.astype(F32)).astype(BF16)
    return (jnp.dot(a_bf16, hi, preferred_element_type=F32)
            + jnp.dot(a_bf16, mid, preferred_element_type=F32)
            + jnp.dot(a_bf16, lo, preferred_element_type=F32))


def _mlstm_kernel(qk_ref, v_ref, g_ref, og_ref, gb_ref, hn_ref, o_ref, ct_ref, m_ref):
    cidx = pl.program_id(1)

    @pl.when(cidx == 0)
    def _():
        ct_ref[...] = jnp.zeros_like(ct_ref)
        m_ref[...] = jnp.zeros_like(m_ref)

    L = CHUNK
    g = g_ref[0]
    gb = gb_ref[...]
    logi = g[:, :LANES] + gb[:, :LANES]
    xf = g[:, LANES:] + gb[:, LANES:]
    logf = jnp.minimum(xf, 0.0) - jnp.log1p(jnp.exp(-jnp.abs(xf)))
    row = lax.broadcasted_iota(jnp.int32, (L, L), 0)
    col = lax.broadcasted_iota(jnp.int32, (L, L), 1)
    tril = row >= col
    bcum = _split3_dot(jnp.where(tril, 1.0, 0.0).astype(BF16), logf)
    x = logi - bcum
    xt = x.T
    m_row = m_ref[...]
    b_last = bcum[L - 1:L, :]
    m_new = jnp.maximum(b_last + m_row, b_last + jnp.max(x, axis=0, keepdims=True))
    decay = jnp.exp(b_last + m_row - m_new)
    ws_all = jnp.exp(b_last + x - m_new)
    inter_all = bcum + m_row
    ones_col = jnp.where(lax.broadcasted_iota(jnp.int32, (L, LANES), 1) == 0, 1.0, 0.0).astype(BF16)
    nt = (((1,), (1,)), ((), ()))
    tn = (((0,), (0,)), ((), ()))
    for h in range(ML_HEADS):
        q = qk_ref[0, :, h * ML_DK:(h + 1) * ML_DK]
        k = qk_ref[0, :, (ML_HEADS + h) * ML_DK:(ML_HEADS + h + 1) * ML_DK]
        vaug = jnp.concatenate([v_ref[0, :, h * ML_DV:(h + 1) * ML_DV], ones_col], axis=-1)
        dlog = jnp.where(tril, bcum[:, h:h + 1] + xt[h:h + 1, :], -jnp.inf)
        inter = inter_all[:, h:h + 1]
        mt = jnp.maximum(inter, jnp.max(dlog, axis=-1, keepdims=True))
        w_inter = jnp.exp(inter - mt)
        sc = lax.dot_general(q, k, nt, preferred_element_type=F32) * jnp.exp(dlog - mt)
        ct = ct_ref[h]
        num = (jnp.dot(sc.astype(BF16), vaug, preferred_element_type=F32)
               + w_inter * jnp.dot(q, ct.astype(BF16), preferred_element_type=F32))
        den = num[:, ML_DV:ML_DV + 1]
        hh = num[:, :ML_DV] / jnp.maximum(jnp.abs(den), jnp.exp(-mt))
        wv = (ws_all[:, h:h + 1] * vaug.astype(F32)).astype(BF16)
        ct_ref[h] = decay[:, h:h + 1] * ct + lax.dot_general(k, wv, tn, preferred_element_type=F32)
        hs = slice(h * ML_DV, (h + 1) * ML_DV)
        hn = hh * lax.rsqrt(jnp.mean(hh * hh, axis=-1, keepdims=True) + EPS) * hn_ref[:, hs]
        o_ref[0, :, hs] = (og_ref[0, :, hs].astype(F32) * hn).astype(BF16)
    m_ref[...] = m_new


def _mlstm(qk, zp, gates, zs, gate_b, head_norm):
    b, s, _ = qk.shape
    gb = jnp.zeros((1, 2 * LANES), F32)
    gb = gb.at[0, :ML_HEADS].set(gate_b[:ML_HEADS]).at[0, LANES:LANES + ML_HEADS].set(gate_b[ML_HEADS:])
    blk = lambda col: pl.BlockSpec((1, CHUNK, MIX_W), lambda bi, c, col=col: (bi, c, col))
    return pl.pallas_call(
        _mlstm_kernel, grid=(b, s // CHUNK),
        in_specs=[blk(0), blk(2),
                  pl.BlockSpec((1, CHUNK, 2 * LANES), lambda bi, c: (bi, c, 0)),
                  blk(0),
                  pl.BlockSpec((1, 2 * LANES), lambda bi, c: (0, 0)),
                  pl.BlockSpec((1, MIX_W), lambda bi, c: (0, 0))],
        out_specs=blk(0),
        out_shape=jax.ShapeDtypeStruct((b, s, MIX_W), BF16),
        scratch_shapes=[pltpu.VMEM((ML_HEADS, ML_DK, ML_DV + LANES), F32), pltpu.VMEM((1, LANES), F32)],
        compiler_params=_params(("parallel", "arbitrary")),
    )(qk, zp, gates, zs, gb, head_norm.reshape(1, MIX_W))


def _merge_kernel(ya_ref, yb_ref, yc_ref, wa_ref, wb_ref, wc_ref, ga_ref, gb_ref, gc_ref, o_ref):
    acc = ga_ref[...].astype(F32) * jnp.dot(ya_ref[...], wa_ref[0], preferred_element_type=F32)
    acc += gb_ref[...].astype(F32) * jnp.dot(yb_ref[...], wb_ref[0], preferred_element_type=F32)
    acc += gc_ref[...].astype(F32) * jnp.dot(yc_ref[...], wc_ref[0], preferred_element_type=F32)
    o_ref[...] = acc.astype(BF16)


def _merge(ya, yb, yc, w_branch, zs, tm=1024, tn=512):
    m = ya.shape[0]
    yspec = pl.BlockSpec((tm, MIX_W), lambda i, j: (i, 0))
    wspec = lambda k: pl.BlockSpec((1, MIX_W, tn), lambda i, j, k=k: (k, 0, j))
    goff = MIX_W // tn
    gspec = lambda k: pl.BlockSpec((tm, tn), lambda i, j, k=k: (i, goff + k * (D_MODEL // tn) + j))
    return pl.pallas_call(
        _merge_kernel, grid=(m // tm, D_MODEL // tn),
        in_specs=[yspec, yspec, yspec, wspec(0), wspec(1), wspec(2), gspec(0), gspec(1), gspec(2)],
        out_specs=pl.BlockSpec((tm, tn), lambda i, j: (i, j)),
        out_shape=jax.ShapeDtypeStruct((m, D_MODEL), BF16),
        compiler_params=_params(("parallel", "parallel")),
    )(ya, yb, yc, w_branch, w_branch, w_branch, zs, zs, zs)


def _prep_mixer_weights(w_mix_in, mla_w_uq, mla_w_ukv):
    w = w_mix_in
    o = _OFF
    w_plain = jnp.concatenate([w[:, o[0]:o[1]], w[:, o[4]:o[7]]], axis=1).astype(BF16)
    w_sig = jnp.concatenate([w[:, o[7]:o[8]], w[:, o[10]:o[11]]], axis=1).astype(BF16)
    w_cq = w[:, o[1]:o[2]].astype(BF16)
    pad = lambda a, n: jnp.pad(a, ((0, 0), (0, n - a.shape[1])))
    w_lat = jnp.concatenate([w[:, o[2]:o[3]], pad(w[:, o[3]:o[4]], LANES),
                             pad(w[:, o[8]:o[9]], LANES), pad(w[:, o[9]:o[10]], LANES)], axis=1).astype(BF16)
    uq = mla_w_uq.reshape(MLA_Q_RANK, MLA_HEADS, MLA_NOPE + MLA_ROPE)
    w_qn = uq[:, :, :MLA_NOPE].reshape(MLA_Q_RANK, MLA_HEADS * MLA_NOPE).astype(BF16)
    w_qr = jnp.pad(uq[:, :, MLA_NOPE:], ((0, 0), (0, 0), (0, LANES - MLA_ROPE)))
    w_qr = w_qr.reshape(MLA_Q_RANK, MLA_HEADS * LANES).astype(BF16)
    ukv = mla_w_ukv.reshape(MLA_KV_RANK, MLA_HEADS, 2, MLA_NOPE)
    w_kv = ukv.transpose(0, 2, 1, 3).reshape(MLA_KV_RANK, 2 * MLA_HEADS * MLA_NOPE).astype(BF16)
    return w_plain, w_sig, w_cq, w_lat, w_qn, w_qr, w_kv


def _mixer(h, positions, rope_tabs, w_mix_in, pool_w, pool_scale, mla_q_norm, mla_w_uq, mla_kv_norm,
           mla_w_ukv, ml_conv_w, ml_conv_b, ml_gate_b, ml_head_norm, w_branch, tm=1024):
    b, s, d = h.shape
    m = b * s
    h2 = h.reshape(m, d)
    w_plain, w_sig, w_cq, w_lat, w_qn, w_qr, w_kv = _prep_mixer_weights(w_mix_in, mla_w_uq, mla_w_ukv)
    cos, sina, sinb = rope_tabs
    tab = lambda a: (a, (tm, LANES), lambda i, j: (i, 0))
    tile = lambda tn: (lambda i, j: (i, j))

    n_plain = w_plain.shape[1]
    zp = _mm(h2, [(w_plain, 0)], [], [((m, n_plain), BF16, (tm, 512), tile(512))],
             _plain_epilogue, tm=tm, tn=512, nj=n_plain // 512)[0]
    n_sig = w_sig.shape[1]
    zs = _mm(h2, [(w_sig, 0)], [], [((m, n_sig), BF16, (tm, 512), tile(512))],
             _sigmoid_epilogue, tm=tm, tn=512, nj=n_sig // 512)[0]
    cqn = _mm(h2, [(w_cq, 0)], [(mla_q_norm.reshape(1, -1), (1, MLA_Q_RANK), lambda i, j: (0, 0))],
              [((m, MLA_Q_RANK), BF16, (tm, MLA_Q_RANK), tile(MLA_Q_RANK))],
              _rmsw_epilogue, tm=tm, tn=MLA_Q_RANK, nj=1)[0]
    n_lat = w_lat.shape[1]
    ckvn, kr, gates = _mm(
        h2, [(w_lat, 0)],
        [(mla_kv_norm.reshape(1, -1), (1, MLA_KV_RANK), lambda i, j: (0, 0)), tab(cos), tab(sina), tab(sinb)],
        [((m, MLA_KV_RANK), BF16, (tm, MLA_KV_RANK), tile(0)),
         ((m, LANES), BF16, (tm, LANES), tile(0)),
         ((m, 2 * LANES), F32, (tm, 2 * LANES), tile(0))],
        _kvlatent_epilogue, tm=tm, tn=n_lat, nj=1)

    zp3 = zp.reshape(b, s, n_plain)
    ya = _pool(zp3, pool_w.astype(BF16), pool_scale, b, s)

    nq = MLA_HEADS * LANES
    qn = _mm(cqn, [(w_qn, 0)], [], [((m, nq), BF16, (tm, 512), tile(512))],
             functools.partial(_scaled_epilogue, scale=MLA_SCALE), tm=tm, tn=512, nj=nq // 512)[0]
    qr = _mm(cqn, [(w_qr, 0)], [tab(cos), tab(sina), tab(sinb)], [((m, nq), BF16, (tm, 512), tile(512))],
             functools.partial(_qrope_epilogue, heads_per_tile=512 // LANES), tm=tm, tn=512, nj=nq // 512)[0]
    kv = _mm(ckvn, [(w_kv, 0)], [], [((m, 2 * nq), BF16, (tm, 512), tile(512))],
             _plain_epilogue, tm=tm, tn=512, nj=2 * nq // 512)[0]
    yb = _attention(qn.reshape(b, s, nq), qr.reshape(b, s, nq), kv.reshape(b, s, 2 * nq),
                    kr.reshape(b, s, LANES), positions)

    qk = _conv_silu(zp3, ml_conv_w, ml_conv_b, b, s)
    yc = _mlstm(qk, zp3, gates.reshape(b, s, 2 * LANES), zs.reshape(b, s, n_sig), ml_gate_b, ml_head_norm)

    return _merge(ya.reshape(m, MIX_W), yb.reshape(m, MIX_W), yc.reshape(m, MIX_W),
                  w_branch.astype(BF16), zs)


def kernel(x, c, positions, w_ada, b_ada, ada_table, ffn_a_w_in, ffn_a_w_out, w_mix_in, pool_w, pool_scale, mla_q_norm, mla_w_uq, mla_kv_norm, mla_w_ukv, ml_conv_w, ml_conv_b, ml_gate_b, ml_head_norm, w_branch, w_out, ffn_b_w_in, ffn_b_w_out, final_norm):
    b, s, d = x.shape
    m = b * s
    mod = _ada(c, w_ada, b_ada, ada_table)
    rope_tabs = _rope_tables(positions)
    for l in range(DEPTH):
        md = mod[l]
        h = _normmod(x, md[:, 0], md[:, 1])
        x = _ffn(x, h, ffn_a_w_in[l].astype(BF16), ffn_a_w_out[l].astype(BF16), md[:, 2])
        h = _normmod(x, md[:, 3], md[:, 4])
        merged = _mixer(h, positions, rope_tabs, w_mix_in[l], pool_w[l], pool_scale[l], mla_q_norm[l],
                        mla_w_uq[l], mla_kv_norm[l], mla_w_ukv[l], ml_conv_w[l], ml_conv_b[l],
                        ml_gate_b[l], ml_head_norm[l], w_branch[l])
        x = _resid_mm(merged, w_out[l].astype(BF16), x.reshape(m, d), md[:, 5], 1.0, s).reshape(b, s, d)
        h = _normmod(x, md[:, 6], md[:, 7])
        x = _ffn(x, h, ffn_b_w_in[l].astype(BF16), ffn_b_w_out[l].astype(BF16), md[:, 8])
    return _finalnorm(x, final_norm)
```

```python
import functools

import numpy as np
import jax
import jax.numpy as jnp
from jax import lax
from jax.experimental import pallas as pl
from jax.experimental.pallas import tpu as pltpu

F32 = jnp.float32
BF16 = jnp.bfloat16

D_MODEL = 4096
DEPTH = 2
CHUNK = 64
EPS = 1e-6
D_FF = 2 * D_MODEL
MIX_W = D_MODEL // 2
N_BRANCH = 3
N_MOD = 9
POOL_WINDOWS = (2, 4, 8, 16)
POOL_GW = MIX_W // len(POOL_WINDOWS)
MLA_NOPE = 128
MLA_ROPE = 64
MLA_V = 128
MLA_HEADS = MIX_W // MLA_V
MLA_Q_RANK = D_MODEL // 4
MLA_KV_RANK = 512
MLA_SCALE = (MLA_NOPE + MLA_ROPE) ** -0.5
ROPE_THETA = 10000.0
ML_HEADS = 8
ML_DK = 128
ML_DV = MIX_W // ML_HEADS
CONV_W = 4

_SPLITS = (MIX_W, MLA_Q_RANK, MLA_KV_RANK, MLA_ROPE, ML_HEADS * ML_DK, ML_HEADS * ML_DK,
           ML_HEADS * ML_DV, ML_HEADS * ML_DV, ML_HEADS, ML_HEADS, N_BRANCH * D_MODEL)
_OFF = tuple(int(v) for v in np.cumsum((0,) + _SPLITS))

LANES = 128
HALO = 16
VMEM_LIMIT = 60 * 1024 * 1024

N_PLAIN = 3 * MIX_W
N_SIG = MIX_W + N_BRANCH * D_MODEL
N_LAT = MLA_KV_RANK + 3 * LANES
COL_PLAIN = 0
COL_SIG = COL_PLAIN + N_PLAIN
COL_CQ = COL_SIG + N_SIG
COL_LAT = COL_CQ + MLA_Q_RANK


def _params(sem):
    return pltpu.CompilerParams(dimension_semantics=sem, vmem_limit_bytes=VMEM_LIMIT)


def _mm_body(*refs, nw, ne, no, nk, epilogue):
    x_ref = refs[0]
    w_refs = refs[1:1 + nw]
    e_refs = refs[1 + nw:1 + nw + ne]
    o_refs = refs[1 + nw + ne:1 + nw + ne + no]
    acc_refs = refs[1 + nw + ne + no:]
    if nk == 1:
        accs = [jnp.dot(x_ref[...], w[0], preferred_element_type=F32) for w in w_refs]
        epilogue(accs, e_refs, o_refs)
        return
    k = pl.program_id(2)

    @pl.when(k == 0)
    def _():
        for a, w in zip(acc_refs, w_refs):
            a[...] = jnp.dot(x_ref[...], w[0], preferred_element_type=F32)

    @pl.when(k > 0)
    def _():
        for a, w in zip(acc_refs, w_refs):
            a[...] += jnp.dot(x_ref[...], w[0], preferred_element_type=F32)

    @pl.when(k == nk - 1)
    def _():
        epilogue([a[...] for a in acc_refs], e_refs, o_refs)


def _mm(x, ws, extras, outs, epilogue, *, tm, tn, nj, nk=1):
    m, kdim = x.shape
    tk = kdim // nk
    in_specs = [pl.BlockSpec((tm, tk), lambda i, j, k: (i, k))]
    for _, lay, col in ws:
        assert col % tn == 0
        in_specs.append(pl.BlockSpec((1, tk, tn), lambda i, j, k, lay=lay, off=col // tn: (lay, k, off + j)))
    for _, blk, f in extras:
        in_specs.append(pl.BlockSpec(blk, lambda i, j, k, f=f: f(i, j)))
    out_specs = [pl.BlockSpec(blk, lambda i, j, k, f=f: f(i, j)) for _, _, blk, f in outs]
    out_shape = [jax.ShapeDtypeStruct(s, d) for s, d, _, _ in outs]
    scratch = [pltpu.VMEM((tm, tn), F32) for _ in ws] if nk > 1 else []
    body = functools.partial(_mm_body, nw=len(ws), ne=len(extras), no=len(outs), nk=nk,
                             epilogue=epilogue)
    return pl.pallas_call(
        body, grid=(m // tm, nj, nk), in_specs=in_specs, out_specs=out_specs,
        out_shape=out_shape, scratch_shapes=scratch,
        compiler_params=_params(("parallel", "parallel", "arbitrary")),
    )(x, *[w for w, _, _ in ws], *[a for a, _, _ in extras])


def _rope128(v, cos, sina, sinb):
    return (v * cos + pltpu.roll(v, LANES - MLA_ROPE // 2, 1) * sina
            + pltpu.roll(v, MLA_ROPE // 2, 1) * sinb)


def _ada_kernel(c_ref, w_ref, b_ref, t_ref, o_ref):
    c = c_ref[...]
    s = c * jax.nn.sigmoid(c)
    acc = jnp.dot(s.astype(BF16), w_ref[...].astype(BF16), preferred_element_type=F32) + b_ref[...]
    for l in range(DEPTH):
        o_ref[l] = acc + t_ref[l]


def _ada(c, w_ada, b_ada, ada_table):
    b = c.shape[0]
    rows = 8
    c8 = jnp.zeros((rows, D_MODEL), F32).at[:b].set(c)
    n = N_MOD * D_MODEL
    tn = 1024
    out = pl.pallas_call(
        _ada_kernel, grid=(n // tn,),
        in_specs=[pl.BlockSpec((rows, D_MODEL), lambda j: (0, 0)),
                  pl.BlockSpec((D_MODEL, tn), lambda j: (0, j)),
                  pl.BlockSpec((1, tn), lambda j: (0, j)),
                  pl.BlockSpec((DEPTH, 1, tn), lambda j: (0, 0, j))],
        out_specs=pl.BlockSpec((DEPTH, rows, tn), lambda j: (0, 0, j)),
        out_shape=jax.ShapeDtypeStruct((DEPTH, rows, n), F32),
        compiler_params=_params(("parallel",)),
    )(c8, w_ada, b_ada.reshape(1, n), ada_table.reshape(DEPTH, 1, n))
    return out[:, :b].reshape(DEPTH, b, N_MOD, D_MODEL)


def _normmod_kernel(x_ref, shift_ref, scale_ref, o_ref):
    x = x_ref[0]
    y = x * lax.rsqrt(jnp.mean(x * x, axis=-1, keepdims=True) + EPS)
    o_ref[0] = (y * (1.0 + scale_ref[0]) + shift_ref[0]).astype(o_ref.dtype)


def _finalnorm_kernel(x_ref, w_ref, o_ref):
    x = x_ref[0]
    y = x * lax.rsqrt(jnp.mean(x * x, axis=-1, keepdims=True) + EPS)
    o_ref[0] = y * w_ref[...]


def _normmod(x, shift, scale, ts=512):
    b, s, d = x.shape
    vec = pl.BlockSpec((1, 1, d), lambda bi, i: (bi, 0, 0))
    return pl.pallas_call(
        _normmod_kernel, grid=(b, s // ts),
        in_specs=[pl.BlockSpec((1, ts, d), lambda bi, i: (bi, i, 0)), vec, vec],
        out_specs=pl.BlockSpec((1, ts, d), lambda bi, i: (bi, i, 0)),
        out_shape=jax.ShapeDtypeStruct((b, s, d), BF16),
        compiler_params=_params(("parallel", "parallel")),
    )(x, shift.reshape(b, 1, d), scale.reshape(b, 1, d))


def _finalnorm(x, w, ts=512):
    b, s, d = x.shape
    return pl.pallas_call(
        _finalnorm_kernel, grid=(b, s // ts),
        in_specs=[pl.BlockSpec((1, ts, d), lambda bi, i: (bi, i, 0)),
                  pl.BlockSpec((1, d), lambda bi, i: (0, 0))],
        out_specs=pl.BlockSpec((1, ts, d), lambda bi, i: (bi, i, 0)),
        out_shape=jax.ShapeDtypeStruct((b, s, d), F32),
        compiler_params=_params(("parallel", "parallel")),
    )(x, w.reshape(1, d))


def _swiglu_epilogue(accs, e_refs, o_refs):
    g, u = accs
    o_refs[0][...] = (g * jax.nn.sigmoid(g) * u).astype(BF16)


def _resid_epilogue(accs, e_refs, o_refs, *, coef):
    x_ref, g_ref = e_refs
    o_refs[0][...] = x_ref[...] + (coef * g_ref[0]) * accs[0]


def _resid_mm(a, w, lay, x2d, gate, coef, seq, *, tm=1024, tn=512, nk=1):
    m, n = x2d.shape
    per_b = seq // tm
    return _mm(a, [(w, lay, 0)],
               [(x2d, (tm, tn), lambda i, j: (i, j)),
                (gate.reshape(-1, 1, n), (1, 1, tn), lambda i, j: (i // per_b, 0, j))],
               [((m, n), F32, (tm, tn), lambda i, j: (i, j))],
               functools.partial(_resid_epilogue, coef=coef), tm=tm, tn=tn, nj=n // tn, nk=nk)[0]


def _ffn(x, h, w_in, w_out, lay, gate, *, tm=1024, tn=512):
    b, s, d = x.shape
    m = b * s
    a = _mm(h.reshape(m, d), [(w_in, lay, 0), (w_in, lay, D_FF)], [],
            [((m, D_FF), BF16, (tm, tn), lambda i, j: (i, j))],
            _swiglu_epilogue, tm=tm, tn=tn, nj=D_FF // tn)[0]
    y = _resid_mm(a, w_out, lay, x.reshape(m, d), gate, 0.5, s, tm=tm, tn=tn, nk=2)
    return y.reshape(b, s, d)


def _rope_tab_kernel(p_ref, inv_ref, cos_ref, sina_ref, sinb_ref):
    ang = p_ref[...].astype(F32) * inv_ref[...]
    lane = lax.broadcasted_iota(jnp.int32, ang.shape, 1)
    half = MLA_ROPE // 2
    c = jnp.cos(ang)
    s = jnp.sin(ang)
    cos_ref[...] = jnp.where(lane < MLA_ROPE, c, 0.0)
    sina_ref[...] = jnp.where(lane < half, -s, 0.0)
    sinb_ref[...] = jnp.where(lane >= half, jnp.where(lane < MLA_ROPE, s, 0.0), 0.0)


def _rope_tables(positions, ts=512):
    m = positions.size
    half = MLA_ROPE // 2
    inv = ROPE_THETA ** (-jnp.arange(0, MLA_ROPE, 2, dtype=F32) / MLA_ROPE)
    inv128 = jnp.concatenate([inv, inv, jnp.zeros((LANES - 2 * half,), F32)]).reshape(1, LANES)
    spec = pl.BlockSpec((ts, LANES), lambda i: (i, 0))
    shp = jax.ShapeDtypeStruct((m, LANES), F32)
    return pl.pallas_call(
        _rope_tab_kernel, grid=(m // ts,),
        in_specs=[pl.BlockSpec((ts, 1), lambda i: (i, 0)), pl.BlockSpec((1, LANES), lambda i: (0, 0))],
        out_specs=[spec, spec, spec], out_shape=[shp, shp, shp],
        compiler_params=_params(("parallel",)),
    )(positions.reshape(m, 1), inv128)


def _plain_epilogue(accs, e_refs, o_refs):
    o_refs[0][...] = accs[0].astype(o_refs[0].dtype)


def _sigmoid_epilogue(accs, e_refs, o_refs):
    o_refs[0][...] = jax.nn.sigmoid(accs[0]).astype(o_refs[0].dtype)


def _rmsw_epilogue(accs, e_refs, o_refs):
    a = accs[0]
    y = a * lax.rsqrt(jnp.mean(a * a, axis=-1, keepdims=True) + EPS) * e_refs[0][0]
    o_refs[0][...] = y.astype(o_refs[0].dtype)


def _kvlatent_epilogue(accs, e_refs, o_refs):
    a = accs[0]
    w_ref, cos_ref, sina_ref, sinb_ref = e_refs
    ckv = a[:, :MLA_KV_RANK]
    y = ckv * lax.rsqrt(jnp.mean(ckv * ckv, axis=-1, keepdims=True) + EPS) * w_ref[0]
    o_refs[0][...] = y.astype(BF16)
    kr = a[:, MLA_KV_RANK:MLA_KV_RANK + LANES]
    o_refs[1][...] = _rope128(kr, cos_ref[...], sina_ref[...], sinb_ref[...]).astype(BF16)
    o_refs[2][...] = a[:, MLA_KV_RANK + LANES:]


def _qrope_epilogue(accs, e_refs, o_refs, *, heads_per_tile):
    a = accs[0] * MLA_SCALE
    cos_ref, sina_ref, sinb_ref = e_refs
    cos, sina, sinb = cos_ref[...], sina_ref[...], sinb_ref[...]
    for c in range(heads_per_tile):
        sl = slice(c * LANES, (c + 1) * LANES)
        o_refs[0][:, sl] = _rope128(a[:, sl], cos, sina, sinb).astype(BF16)


def _scaled_epilogue(accs, e_refs, o_refs, *, scale):
    o_refs[0][...] = (accs[0] * scale).astype(o_refs[0].dtype)


def _band(ts, lo, hi, first_tile):
    t = lax.broadcasted_iota(jnp.int32, (ts, HALO + ts), 0)
    s = lax.broadcasted_iota(jnp.int32, (ts, HALO + ts), 1)
    d = t + HALO - s
    ok = jnp.where(d >= lo, jnp.where(d < hi, 1.0, 0.0), 0.0)
    ok = jnp.where(s < HALO, jnp.where(first_tile, 0.0, ok), ok)
    return ok.astype(BF16)


def _pool_kernel(u_ref, halo_ref, pw_ref, ps_ref, o_ref, *, ts):
    i = pl.program_id(1)
    u = u_ref[0]
    ucat = jnp.concatenate([halo_ref[0], u], axis=0)
    tg = i * ts + lax.broadcasted_iota(jnp.int32, (ts, 1), 0)
    for g, w in enumerate(POOL_WINDOWS):
        sl = slice(g * POOL_GW, (g + 1) * POOL_GW)
        win = jnp.dot(_band(ts, 0, w, i == 0), ucat[:, sl], preferred_element_type=F32)
        cnt = jnp.minimum(tg + 1, w).astype(F32)
        p = win / cnt - u[:, sl].astype(F32)
        y = jnp.dot(p.astype(BF16), pw_ref[0, g], preferred_element_type=F32)
        o_ref[0, :, sl] = (y * ps_ref[0, :, sl]).astype(BF16)


def _pool(zp, pool_w, pool_scale, lay, b, s, ts=256):
    hb = ts // HALO
    ng = len(POOL_WINDOWS)
    return pl.pallas_call(
        functools.partial(_pool_kernel, ts=ts), grid=(b, s // ts),
        in_specs=[pl.BlockSpec((1, ts, MIX_W), lambda bi, i: (bi, i, 0)),
                  pl.BlockSpec((1, HALO, MIX_W), lambda bi, i: (bi, jnp.maximum(i * hb - 1, 0), 0)),
                  pl.BlockSpec((1, ng, POOL_GW, POOL_GW), lambda bi, i: (lay, 0, 0, 0)),
                  pl.BlockSpec((1, 1, MIX_W), lambda bi, i: (lay, 0, 0))],
        out_specs=pl.BlockSpec((1, ts, MIX_W), lambda bi, i: (bi, i, 0)),
        out_shape=jax.ShapeDtypeStruct((b, s, MIX_W), BF16),
        compiler_params=_params(("parallel", "parallel")),
    )(zp, zp, pool_w, pool_scale.reshape(DEPTH, 1, MIX_W))


def _conv_kernel(x_ref, halo_ref, w_ref, b_ref, sc_ref, o_ref, *, ts):
    i = pl.program_id(1)
    x = x_ref[0]
    xcat = jnp.concatenate([halo_ref[0], x], axis=0)
    w = w_ref[0]
    acc = x.astype(F32) * w[CONV_W - 1:CONV_W, :] + b_ref[0]
    for d in range(1, CONV_W):
        xs = jnp.dot(_band(ts, d, d + 1, i == 0), xcat, preferred_element_type=F32)
        acc = acc + xs * w[CONV_W - 1 - d:CONV_W - d, :]
    o_ref[0] = (acc * jax.nn.sigmoid(acc) * sc_ref[...]).astype(BF16)


def _conv_silu(zp, conv_w, conv_b, lay, b, s, ts=256):
    c = 2 * ML_HEADS * ML_DK
    hb = ts // HALO
    post = jnp.concatenate([jnp.full((c // 2,), ML_DK ** -0.5, F32), jnp.ones((c // 2,), F32)])
    return pl.pallas_call(
        functools.partial(_conv_kernel, ts=ts), grid=(b, s // ts),
        in_specs=[pl.BlockSpec((1, ts, c), lambda bi, i: (bi, i, 1)),
                  pl.BlockSpec((1, HALO, c), lambda bi, i: (bi, jnp.maximum(i * hb - 1, 0), 1)),
                  pl.BlockSpec((1, CONV_W, c), lambda bi, i: (lay, 0, 0)),
                  pl.BlockSpec((1, 1, c), lambda bi, i: (lay, 0, 0)),
                  pl.BlockSpec((1, c), lambda bi, i: (0, 0))],
        out_specs=pl.BlockSpec((1, ts, c), lambda bi, i: (bi, i, 0)),
        out_shape=jax.ShapeDtypeStruct((b, s, c), BF16),
        compiler_params=_params(("parallel", "parallel")),
    )(zp, zp, conv_w, conv_b.reshape(DEPTH, 1, c), post.reshape(1, c))


ATTN_HEADS_PER_STEP = 4


def _attn_kernel(qn_ref, qr_ref, kn_ref, kr_ref, v_ref, pq_ref, pk_ref, o_ref, m_sc, l_sc, acc_sc, *, tq):
    i = pl.program_id(2)
    nt = (((1,), (1,)), ((), ()))
    tn = (((0,), (0,)), ((), ()))
    hs = [slice(g * LANES, (g + 1) * LANES) for g in range(ATTN_HEADS_PER_STEP)]

    def block(start, mask, first):
        kr = kr_ref[0, pl.ds(start, tq), :]
        for g, sl in enumerate(hs):
            q = jnp.concatenate([qn_ref[0, :, sl], qr_ref[0, :, sl]], axis=-1)
            k = jnp.concatenate([kn_ref[0, pl.ds(start, tq), sl], kr], axis=-1)
            st = lax.dot_general(k, q, nt, preferred_element_type=F32)
            if mask is not None:
                st = jnp.where(mask, st, -jnp.inf)
            v = v_ref[0, pl.ds(start, tq), sl]
            smax = jnp.max(st, axis=0, keepdims=True)
            if first:
                m_new = smax
                p = jnp.exp(st - m_new)
                l_sc[g] = jnp.sum(p, axis=0, keepdims=True)
                acc_sc[g] = lax.dot_general(v, p.astype(BF16), tn, preferred_element_type=F32)
            else:
                m_old = m_sc[g]
                m_new = jnp.maximum(m_old, smax)
                alpha = jnp.exp(m_old - m_new)
                p = jnp.exp(st - m_new)
                l_sc[g] = alpha * l_sc[g] + jnp.sum(p, axis=0, keepdims=True)
                acc_sc[g] = alpha * acc_sc[g] + lax.dot_general(v, p.astype(BF16), tn,
                                                                preferred_element_type=F32)
            m_sc[g] = m_new

    sh = CHUNK.bit_length() - 1
    mask = lax.shift_right_arithmetic(pk_ref[0], sh) <= lax.shift_right_arithmetic(pq_ref[0], sh)
    block(pl.multiple_of(i * tq, tq), mask, True)

    def body(j, carry):
        block(pl.multiple_of(j * tq, tq), None, False)
        return carry

    lax.fori_loop(0, i, body, 0)
    for g, sl in enumerate(hs):
        o_ref[0, :, sl] = (acc_sc[g] / l_sc[g]).T.astype(o_ref.dtype)


def _attention(q, kv, kr, positions, tq=512):
    b, s, _ = q.shape
    G = ATTN_HEADS_PER_STEP
    w = G * LANES
    ng = MLA_HEADS // G
    return pl.pallas_call(
        functools.partial(_attn_kernel, tq=tq), grid=(b, ng, s // tq),
        in_specs=[pl.BlockSpec((1, tq, w), lambda bi, hi, i: (bi, i, hi)),
                  pl.BlockSpec((1, tq, w), lambda bi, hi, i: (bi, i, ng + hi)),
                  pl.BlockSpec((1, s, w), lambda bi, hi, i: (bi, 0, hi)),
                  pl.BlockSpec((1, s, LANES), lambda bi, hi, i: (bi, 0, 0)),
                  pl.BlockSpec((1, s, w), lambda bi, hi, i: (bi, 0, ng + hi)),
                  pl.BlockSpec((1, 1, tq), lambda bi, hi, i: (bi, 0, i)),
                  pl.BlockSpec((1, tq, 1), lambda bi, hi, i: (bi, i, 0))],
        out_specs=pl.BlockSpec((1, tq, w), lambda bi, hi, i: (bi, i, hi)),
        out_shape=jax.ShapeDtypeStruct((b, s, MLA_HEADS * MLA_V), BF16),
        scratch_shapes=[pltpu.VMEM((G, 1, tq), F32), pltpu.VMEM((G, 1, tq), F32),
                        pltpu.VMEM((G, MLA_V, tq), F32)],
        compiler_params=_params(("parallel", "parallel", "arbitrary")),
    )(q, q, kv, kr, kv, positions.reshape(b, 1, s), positions.reshape(b, s, 1))


def _split3_dot(a_bf16, x):
    hi = x.astype(BF16)
    r1 = x - hi.astype(F32)
    mid = r1.astype(BF16)
    lo = (r1 - mid.astype(F32)).astype(BF16)
    return (jnp.dot(a_bf16, hi, preferred_element_type=F32)
            + jnp.dot(a_bf16, mid, preferred_element_type=F32)
            + jnp.dot(a_bf16, lo, preferred_element_type=F32))


def _mlstm_kernel(qk_ref, v_ref, g_ref, og_ref, gb_ref, hn_ref, o_ref, ct_ref, m_ref):
    cidx = pl.program_id(1)

    @pl.when(cidx == 0)
    def _():
        ct_ref[...] = jnp.zeros_like(ct_ref)
        m_ref[...] = jnp.zeros_like(m_ref)

    L = CHUNK
    g = g_ref[0]
    gb = gb_ref[...]
    logi = g[:, :LANES] + gb[:, :LANES]
    xf = g[:, LANES:] + gb[:, LANES:]
    logf = jnp.minimum(xf, 0.0) - jnp.log1p(jnp.exp(-jnp.abs(xf)))
    row = lax.broadcasted_iota(jnp.int32, (L, L), 0)
    col = lax.broadcasted_iota(jnp.int32, (L, L), 1)
    tril = row >= col
    bcum = _split3_dot(jnp.where(tril, 1.0, 0.0).astype(BF16), logf)
    x = logi - bcum
    xt = x.T
    m_row = m_ref[...]
    b_last = bcum[L - 1:L, :]
    m_new = jnp.maximum(b_last + m_row, b_last + jnp.max(x, axis=0, keepdims=True))
    decay = jnp.exp(b_last + m_row - m_new)
    ws_all = jnp.exp(b_last + x - m_new)
    inter_all = bcum + m_row
    ones_col = jnp.where(lax.broadcasted_iota(jnp.int32, (L, LANES), 1) == 0, 1.0, 0.0).astype(BF16)
    nt = (((1,), (1,)), ((), ()))
    tn = (((0,), (0,)), ((), ()))
    for h in range(ML_HEADS):
        q = qk_ref[0, :, h * ML_DK:(h + 1) * ML_DK]
        k = qk_ref[0, :, (ML_HEADS + h) * ML_DK:(ML_HEADS + h + 1) * ML_DK]
        vaug = jnp.concatenate([v_ref[0, :, h * ML_DV:(h + 1) * ML_DV], ones_col], axis=-1)
        dlog = jnp.where(tril, bcum[:, h:h + 1] + xt[h:h + 1, :], -jnp.inf)
        inter = inter_all[:, h:h + 1]
        mt = jnp.maximum(inter, jnp.max(dlog, axis=-1, keepdims=True))
        w_inter = jnp.exp(inter - mt)
        sc = lax.dot_general(q, k, nt, preferred_element_type=F32) * jnp.exp(dlog - mt)
        ct = ct_ref[h]
        num = (jnp.dot(sc.astype(BF16), vaug, preferred_element_type=F32)
               + w_inter * jnp.dot(q, ct.astype(BF16), preferred_element_type=F32))
        den = num[:, ML_DV:ML_DV + 1]
        hh = num[:, :ML_DV] / jnp.maximum(jnp.abs(den), jnp.exp(-mt))
        wv = (ws_all[:, h:h + 1] * vaug.astype(F32)).astype(BF16)
        ct_ref[h] = decay[:, h:h + 1] * ct + lax.dot_general(k, wv, tn, preferred_element_type=F32)
        hs = slice(h * ML_DV, (h + 1) * ML_DV)
        hn = hh * lax.rsqrt(jnp.mean(hh * hh, axis=-1, keepdims=True) + EPS) * hn_ref[0, :, hs]
        o_ref[0, :, hs] = (og_ref[0, :, hs].astype(F32) * hn).astype(BF16)
    m_ref[...] = m_new


def _mlstm(qk, zp, gates, zs, gate_b, head_norm, lay):
    b, s, _ = qk.shape
    gb = jnp.zeros((1, 2 * LANES), F32)
    gb = gb.at[0, :ML_HEADS].set(gate_b[:ML_HEADS]).at[0, LANES:LANES + ML_HEADS].set(gate_b[ML_HEADS:])
    blk = lambda col: pl.BlockSpec((1, CHUNK, MIX_W), lambda bi, c, col=col: (bi, c, col))
    return pl.pallas_call(
        _mlstm_kernel, grid=(b, s // CHUNK),
        in_specs=[blk(0), blk(2),
                  pl.BlockSpec((1, CHUNK, 2 * LANES), lambda bi, c: (bi, c, 0)),
                  blk(0),
                  pl.BlockSpec((1, 2 * LANES), lambda bi, c: (0, 0)),
                  pl.BlockSpec((1, 1, MIX_W), lambda bi, c: (lay, 0, 0))],
        out_specs=blk(0),
        out_shape=jax.ShapeDtypeStruct((b, s, MIX_W), BF16),
        scratch_shapes=[pltpu.VMEM((ML_HEADS, ML_DK, ML_DV + LANES), F32), pltpu.VMEM((1, LANES), F32)],
        compiler_params=_params(("parallel", "arbitrary")),
    )(qk, zp, gates, zs, gb, head_norm.reshape(DEPTH, 1, MIX_W))


def _merge_kernel(ya_ref, yb_ref, yc_ref, wa_ref, wb_ref, wc_ref, ga_ref, gb_ref, gc_ref, o_ref):
    acc = ga_ref[...].astype(F32) * jnp.dot(ya_ref[...], wa_ref[0, 0], preferred_element_type=F32)
    acc += gb_ref[...].astype(F32) * jnp.dot(yb_ref[...], wb_ref[0, 0], preferred_element_type=F32)
    acc += gc_ref[...].astype(F32) * jnp.dot(yc_ref[...], wc_ref[0, 0], preferred_element_type=F32)
    o_ref[...] = acc.astype(BF16)


def _merge(ya, yb, yc, w_branch, lay, zs, tm=1024, tn=512):
    m = ya.shape[0]
    yspec = pl.BlockSpec((tm, MIX_W), lambda i, j: (i, 0))
    wspec = lambda k: pl.BlockSpec((1, 1, MIX_W, tn), lambda i, j, k=k: (lay, k, 0, j))
    goff = MIX_W // tn
    gspec = lambda k: pl.BlockSpec((tm, tn), lambda i, j, k=k: (i, goff + k * (D_MODEL // tn) + j))
    return pl.pallas_call(
        _merge_kernel, grid=(m // tm, D_MODEL // tn),
        in_specs=[yspec, yspec, yspec, wspec(0), wspec(1), wspec(2), gspec(0), gspec(1), gspec(2)],
        out_specs=pl.BlockSpec((tm, tn), lambda i, j: (i, j)),
        out_shape=jax.ShapeDtypeStruct((m, D_MODEL), BF16),
        compiler_params=_params(("parallel", "parallel")),
    )(ya, yb, yc, w_branch, w_branch, w_branch, zs, zs, zs)


def _prep_mixer_weights(w_mix_in, mla_w_uq, mla_w_ukv):
    w = w_mix_in
    o = _OFF
    pad = lambda a: jnp.pad(a, ((0, 0), (0, 0), (0, LANES - a.shape[-1])))
    w_all = jnp.concatenate(
        [w[..., o[0]:o[1]], w[..., o[4]:o[7]],
         w[..., o[7]:o[8]], w[..., o[10]:o[11]],
         w[..., o[1]:o[2]],
         w[..., o[2]:o[3]], pad(w[..., o[3]:o[4]]), pad(w[..., o[8]:o[9]]), pad(w[..., o[9]:o[10]])],
        axis=-1).astype(BF16)
    uq = mla_w_uq.reshape(DEPTH, MLA_Q_RANK, MLA_HEADS, MLA_NOPE + MLA_ROPE)
    w_qn = uq[..., :MLA_NOPE].reshape(DEPTH, MLA_Q_RANK, MLA_HEADS * MLA_NOPE)
    w_qr = jnp.pad(uq[..., MLA_NOPE:], ((0, 0), (0, 0), (0, 0), (0, LANES - MLA_ROPE)))
    w_q = jnp.concatenate([w_qn, w_qr.reshape(DEPTH, MLA_Q_RANK, MLA_HEADS * LANES)], axis=-1).astype(BF16)
    ukv = mla_w_ukv.reshape(DEPTH, MLA_KV_RANK, MLA_HEADS, 2, MLA_NOPE)
    w_kv = ukv.transpose(0, 1, 3, 2, 4).reshape(DEPTH, MLA_KV_RANK, 2 * MLA_HEADS * MLA_NOPE).astype(BF16)
    return w_all, w_q, w_kv


def _mixer(h, lay, positions, rope_tabs, w_all, w_q, w_kv, pool_w, pool_scale, mla_q_norm, mla_kv_norm,
           ml_conv_w, ml_conv_b, ml_gate_b, ml_head_norm, w_branch, tm=1024):
    b, s, d = h.shape
    m = b * s
    h2 = h.reshape(m, d)
    cos, sina, sinb = rope_tabs
    tab = lambda a: (a, (tm, LANES), lambda i, j: (i, 0))
    tile = lambda i, j: (i, j)
    layvec = lambda a: (a.reshape(DEPTH, 1, -1), (1, 1, a.shape[-1]), lambda i, j: (lay, 0, 0))

    zp = _mm(h2, [(w_all, lay, COL_PLAIN)], [], [((m, N_PLAIN), BF16, (tm, 512), tile)],
             _plain_epilogue, tm=tm, tn=512, nj=N_PLAIN // 512)[0]
    zs = _mm(h2, [(w_all, lay, COL_SIG)], [], [((m, N_SIG), BF16, (tm, 512), tile)],
             _sigmoid_epilogue, tm=tm, tn=512, nj=N_SIG // 512)[0]
    cqn = _mm(h2, [(w_all, lay, COL_CQ)], [layvec(mla_q_norm)],
              [((m, MLA_Q_RANK), BF16, (tm, MLA_Q_RANK), tile)],
              _rmsw_epilogue, tm=tm, tn=MLA_Q_RANK, nj=1)[0]
    ckvn, kr, gates = _mm(
        h2, [(w_all, lay, COL_LAT)], [layvec(mla_kv_norm), tab(cos), tab(sina), tab(sinb)],
        [((m, MLA_KV_RANK), BF16, (tm, MLA_KV_RANK), tile),
         ((m, LANES), BF16, (tm, LANES), tile),
         ((m, 2 * LANES), F32, (tm, 2 * LANES), tile)],
        _kvlatent_epilogue, tm=tm, tn=N_LAT, nj=1)

    zp3 = zp.reshape(b, s, N_PLAIN)
    ya = _pool(zp3, pool_w, pool_scale, lay, b, s)

    nq = MLA_HEADS * LANES
    q = _mm(cqn, [(w_q, lay, 0)], [tab(cos), tab(sina), tab(sinb)], [((m, 2 * nq), BF16, (tm, 512), tile)],
            functools.partial(_q_epilogue, heads_per_tile=512 // LANES, n_nope_tiles=nq // 512),
            tm=tm, tn=512, nj=2 * nq // 512)[0]
    kv = _mm(ckvn, [(w_kv, lay, 0)], [], [((m, 2 * nq), BF16, (tm, 512), tile)],
             _plain_epilogue, tm=tm, tn=512, nj=2 * nq // 512)[0]
    yb = _attention(q.reshape(b, s, 2 * nq), kv.reshape(b, s, 2 * nq), kr.reshape(b, s, LANES), positions)

    qk = _conv_silu(zp3, ml_conv_w, ml_conv_b, lay, b, s)
    yc = _mlstm(qk, zp3, gates.reshape(b, s, 2 * LANES), zs.reshape(b, s, N_SIG), ml_gate_b[lay],
                ml_head_norm, lay)

    return _merge(ya.reshape(m, MIX_W), yb.reshape(m, MIX_W), yc.reshape(m, MIX_W), w_branch, lay, zs)


def _q_epilogue(accs, e_refs, o_refs, *, heads_per_tile, n_nope_tiles):
    j = pl.program_id(1)

    @pl.when(j < n_nope_tiles)
    def _():
        _scaled_epilogue(accs, (), o_refs, scale=MLA_SCALE)

    @pl.when(j >= n_nope_tiles)
    def _():
        _qrope_epilogue(accs, e_refs, o_refs, heads_per_tile=heads_per_tile)


def kernel(x, c, positions, w_ada, b_ada, ada_table, ffn_a_w_in, ffn_a_w_out, w_mix_in, pool_w, pool_scale, mla_q_norm, mla_w_uq, mla_kv_norm, mla_w_ukv, ml_conv_w, ml_conv_b, ml_gate_b, ml_head_norm, w_branch, w_out, ffn_b_w_in, ffn_b_w_out, final_norm):
    b, s, d = x.shape
    m = b * s
    mod = _ada(c, w_ada, b_ada, ada_table)
    rope_tabs = _rope_tables(positions)
    fa_in, fa_out = ffn_a_w_in.astype(BF16), ffn_a_w_out.astype(BF16)
    fb_in, fb_out = ffn_b_w_in.astype(BF16), ffn_b_w_out.astype(BF16)
    w_all, w_q, w_kv = _prep_mixer_weights(w_mix_in, mla_w_uq, mla_w_ukv)
    pool_wb, w_branchb, w_outb = pool_w.astype(BF16), w_branch.astype(BF16), w_out.astype(BF16)
    for l in range(DEPTH):
        md = mod[l]
        h = _normmod(x, md[:, 0], md[:, 1])
        x = _ffn(x, h, fa_in, fa_out, l, md[:, 2])
        h = _normmod(x, md[:, 3], md[:, 4])
        merged = _mixer(h, l, positions, rope_tabs, w_all, w_q, w_kv, pool_wb, pool_scale, mla_q_norm,
                        mla_kv_norm, ml_conv_w, ml_conv_b, ml_gate_b, ml_head_norm, w_branchb)
        x = _resid_mm(merged, w_outb, l, x.reshape(m, d), md[:, 5], 1.0, s).reshape(b, s, d)
        h = _normmod(x, md[:, 6], md[:, 7])
        x = _ffn(x, h, fb_in, fb_out, l, md[:, 8])
    return _finalnorm(x, final_norm)
```

```python
import functools

import numpy as np
import jax
import jax.numpy as jnp
from jax import lax
from jax.experimental import pallas as pl
from jax.experimental.pallas import tpu as pltpu

F32 = jnp.float32
BF16 = jnp.bfloat16

D_MODEL = 4096
DEPTH = 2
CHUNK = 64
EPS = 1e-6
D_FF = 2 * D_MODEL
MIX_W = D_MODEL // 2
N_BRANCH = 3
N_MOD = 9
POOL_WINDOWS = (2, 4, 8, 16)
POOL_GW = MIX_W // len(POOL_WINDOWS)
MLA_NOPE = 128
MLA_ROPE = 64
MLA_V = 128
MLA_HEADS = MIX_W // MLA_V
MLA_Q_RANK = D_MODEL // 4
MLA_KV_RANK = 512
MLA_SCALE = (MLA_NOPE + MLA_ROPE) ** -0.5
Q_SCALE_LOG2 = MLA_SCALE * float(np.log2(np.e))
ROPE_THETA = 10000.0
ML_HEADS = 8
ML_DK = 128
ML_DV = MIX_W // ML_HEADS
CONV_W = 4

_SPLITS = (MIX_W, MLA_Q_RANK, MLA_KV_RANK, MLA_ROPE, ML_HEADS * ML_DK, ML_HEADS * ML_DK,
           ML_HEADS * ML_DV, ML_HEADS * ML_DV, ML_HEADS, ML_HEADS, N_BRANCH * D_MODEL)
_OFF = tuple(int(v) for v in np.cumsum((0,) + _SPLITS))

LANES = 128
HALO = 16
VMEM_LIMIT = 60 * 1024 * 1024

N_PLAIN = 3 * MIX_W
N_SIG = MIX_W + N_BRANCH * D_MODEL
N_LAT = MLA_KV_RANK + 3 * LANES
COL_PLAIN = 0
COL_SIG = COL_PLAIN + N_PLAIN
COL_CQ = COL_SIG + N_SIG
COL_LAT = COL_CQ + MLA_Q_RANK


def _params(sem):
    return pltpu.CompilerParams(dimension_semantics=sem, vmem_limit_bytes=VMEM_LIMIT)


def _mm_body(*refs, nw, ne, no, nk, epilogue):
    x_ref = refs[0]
    w_refs = refs[1:1 + nw]
    e_refs = refs[1 + nw:1 + nw + ne]
    o_refs = refs[1 + nw + ne:1 + nw + ne + no]
    acc_refs = refs[1 + nw + ne + no:]
    if nk == 1:
        accs = [jnp.dot(x_ref[...], w[0], preferred_element_type=F32) for w in w_refs]
        epilogue(accs, e_refs, o_refs)
        return
    k = pl.program_id(2)

    @pl.when(k == 0)
    def _():
        for a, w in zip(acc_refs, w_refs):
            a[...] = jnp.dot(x_ref[...], w[0], preferred_element_type=F32)

    @pl.when(k > 0)
    def _():
        for a, w in zip(acc_refs, w_refs):
            a[...] += jnp.dot(x_ref[...], w[0], preferred_element_type=F32)

    @pl.when(k == nk - 1)
    def _():
        epilogue([a[...] for a in acc_refs], e_refs, o_refs)


def _mm(x, ws, extras, outs, epilogue, *, tm, tn, nj, nk=1):
    m, kdim = x.shape
    tk = kdim // nk
    in_specs = [pl.BlockSpec((tm, tk), lambda i, j, k: (i, k))]
    for _, lay, col in ws:
        assert col % tn == 0
        in_specs.append(pl.BlockSpec((1, tk, tn), lambda i, j, k, lay=lay, off=col // tn: (lay, k, off + j)))
    for _, blk, f in extras:
        in_specs.append(pl.BlockSpec(blk, lambda i, j, k, f=f: f(i, j)))
    out_specs = [pl.BlockSpec(blk, lambda i, j, k, f=f: f(i, j)) for _, _, blk, f in outs]
    out_shape = [jax.ShapeDtypeStruct(s, d) for s, d, _, _ in outs]
    scratch = [pltpu.VMEM((tm, tn), F32) for _ in ws] if nk > 1 else []
    body = functools.partial(_mm_body, nw=len(ws), ne=len(extras), no=len(outs), nk=nk,
                             epilogue=epilogue)
    return pl.pallas_call(
        body, grid=(m // tm, nj, nk), in_specs=in_specs, out_specs=out_specs,
        out_shape=out_shape, scratch_shapes=scratch,
        compiler_params=_params(("parallel", "parallel", "arbitrary")),
    )(x, *[w for w, _, _ in ws], *[a for a, _, _ in extras])


def _rope128(v, cos, sina, sinb):
    return (v * cos + pltpu.roll(v, LANES - MLA_ROPE // 2, 1) * sina
            + pltpu.roll(v, MLA_ROPE // 2, 1) * sinb)


def _ada_kernel(c_ref, w_ref, b_ref, t_ref, o_ref):
    c = c_ref[...]
    s = c * jax.nn.sigmoid(c)
    acc = jnp.dot(s.astype(BF16), w_ref[...].astype(BF16), preferred_element_type=F32) + b_ref[...]
    for l in range(DEPTH):
        o_ref[l] = acc + t_ref[l]


def _ada(c, w_ada, b_ada, ada_table):
    b = c.shape[0]
    rows = 8
    c8 = jnp.zeros((rows, D_MODEL), F32).at[:b].set(c)
    n = N_MOD * D_MODEL
    tn = 1024
    out = pl.pallas_call(
        _ada_kernel, grid=(n // tn,),
        in_specs=[pl.BlockSpec((rows, D_MODEL), lambda j: (0, 0)),
                  pl.BlockSpec((D_MODEL, tn), lambda j: (0, j)),
                  pl.BlockSpec((1, tn), lambda j: (0, j)),
                  pl.BlockSpec((DEPTH, 1, tn), lambda j: (0, 0, j))],
        out_specs=pl.BlockSpec((DEPTH, rows, tn), lambda j: (0, 0, j)),
        out_shape=jax.ShapeDtypeStruct((DEPTH, rows, n), F32),
        compiler_params=_params(("parallel",)),
    )(c8, w_ada, b_ada.reshape(1, n), ada_table.reshape(DEPTH, 1, n))
    return out[:, :b].reshape(DEPTH, b, N_MOD, D_MODEL)


def _normmod_kernel(x_ref, shift_ref, scale_ref, o_ref):
    x = x_ref[0]
    y = x * lax.rsqrt(jnp.mean(x * x, axis=-1, keepdims=True) + EPS)
    o_ref[0] = (y * (1.0 + scale_ref[0]) + shift_ref[0]).astype(o_ref.dtype)


def _finalnorm_kernel(x_ref, w_ref, o_ref):
    x = x_ref[0]
    y = x * lax.rsqrt(jnp.mean(x * x, axis=-1, keepdims=True) + EPS)
    o_ref[0] = y * w_ref[...]


def _normmod(x, shift, scale, ts=512):
    b, s, d = x.shape
    vec = pl.BlockSpec((1, 1, d), lambda bi, i: (bi, 0, 0))
    return pl.pallas_call(
        _normmod_kernel, grid=(b, s // ts),
        in_specs=[pl.BlockSpec((1, ts, d), lambda bi, i: (bi, i, 0)), vec, vec],
        out_specs=pl.BlockSpec((1, ts, d), lambda bi, i: (bi, i, 0)),
        out_shape=jax.ShapeDtypeStruct((b, s, d), BF16),
        compiler_params=_params(("parallel", "parallel")),
    )(x, shift.reshape(b, 1, d), scale.reshape(b, 1, d))


def _finalnorm(x, w, ts=512):
    b, s, d = x.shape
    return pl.pallas_call(
        _finalnorm_kernel, grid=(b, s // ts),
        in_specs=[pl.BlockSpec((1, ts, d), lambda bi, i: (bi, i, 0)),
                  pl.BlockSpec((1, d), lambda bi, i: (0, 0))],
        out_specs=pl.BlockSpec((1, ts, d), lambda bi, i: (bi, i, 0)),
        out_shape=jax.ShapeDtypeStruct((b, s, d), F32),
        compiler_params=_params(("parallel", "parallel")),
    )(x, w.reshape(1, d))


def _resid_epilogue(accs, e_refs, o_refs, *, coef):
    x_ref, g_ref = e_refs
    o_refs[0][...] = x_ref[...] + (coef * g_ref[0]) * accs[0]


def _resid_mm(a, w, lay, x2d, gate, coef, seq, *, tm=1024, tn=512, nk=1):
    m, n = x2d.shape
    per_b = seq // tm
    return _mm(a, [(w, lay, 0)],
               [(x2d, (tm, tn), lambda i, j: (i, j)),
                (gate.reshape(-1, 1, n), (1, 1, tn), lambda i, j: (i // per_b, 0, j))],
               [((m, n), F32, (tm, tn), lambda i, j: (i, j))],
               functools.partial(_resid_epilogue, coef=coef), tm=tm, tn=tn, nj=n // tn, nk=nk)[0]


def _swiglu_kernel(x_ref, wg_ref, wu_ref, o_ref, wg_sc, wu_sc):
    @pl.when(pl.program_id(1) == 0)
    def _():
        wg_sc[...] = wg_ref[0].astype(BF16)
        wu_sc[...] = wu_ref[0].astype(BF16)

    x = x_ref[...]
    g = jnp.dot(x, wg_sc[...], preferred_element_type=F32)
    u = jnp.dot(x, wu_sc[...], preferred_element_type=F32)
    o_ref[...] = (g * jax.nn.sigmoid(g) * u).astype(BF16)


def _swiglu_mm(h2, w_in, lay, *, tm=1024, tn=256):
    m, d = h2.shape
    nj = D_FF // tn
    return pl.pallas_call(
        _swiglu_kernel, grid=(nj, m // tm),
        in_specs=[pl.BlockSpec((tm, d), lambda j, i: (i, 0)),
                  pl.BlockSpec((1, d, tn), lambda j, i: (lay, 0, j)),
                  pl.BlockSpec((1, d, tn), lambda j, i: (lay, 0, nj + j))],
        out_specs=pl.BlockSpec((tm, tn), lambda j, i: (i, j)),
        out_shape=jax.ShapeDtypeStruct((m, D_FF), BF16),
        scratch_shapes=[pltpu.VMEM((d, tn), BF16), pltpu.VMEM((d, tn), BF16)],
        compiler_params=_params(("parallel", "arbitrary")),
    )(h2, w_in, w_in)


def _ffn(x, h, w_in, w_out, lay, gate):
    b, s, d = x.shape
    m = b * s
    a = _swiglu_mm(h.reshape(m, d), w_in, lay)
    y = _resid_mm(a, w_out, lay, x.reshape(m, d), gate, 0.5, s, tm=512, tn=512)
    return y.reshape(b, s, d)


def _rope_tab_kernel(p_ref, inv_ref, cos_ref, sina_ref, sinb_ref):
    ang = p_ref[...].astype(F32) * inv_ref[...]
    lane = lax.broadcasted_iota(jnp.int32, ang.shape, 1)
    half = MLA_ROPE // 2
    c = jnp.cos(ang)
    s = jnp.sin(ang)
    cos_ref[...] = jnp.where(lane < MLA_ROPE, c, 0.0)
    sina_ref[...] = jnp.where(lane < half, -s, 0.0)
    sinb_ref[...] = jnp.where(lane >= half, jnp.where(lane < MLA_ROPE, s, 0.0), 0.0)


def _rope_tables(positions, ts=512):
    m = positions.size
    half = MLA_ROPE // 2
    inv = ROPE_THETA ** (-jnp.arange(0, MLA_ROPE, 2, dtype=F32) / MLA_ROPE)
    inv128 = jnp.concatenate([inv, inv, jnp.zeros((LANES - 2 * half,), F32)]).reshape(1, LANES)
    spec = pl.BlockSpec((ts, LANES), lambda i: (i, 0))
    shp = jax.ShapeDtypeStruct((m, LANES), F32)
    return pl.pallas_call(
        _rope_tab_kernel, grid=(m // ts,),
        in_specs=[pl.BlockSpec((ts, 1), lambda i: (i, 0)), pl.BlockSpec((1, LANES), lambda i: (0, 0))],
        out_specs=[spec, spec, spec], out_shape=[shp, shp, shp],
        compiler_params=_params(("parallel",)),
    )(positions.reshape(m, 1), inv128)


def _plain_epilogue(accs, e_refs, o_refs):
    o_refs[0][...] = accs[0].astype(o_refs[0].dtype)


def _sigmoid_epilogue(accs, e_refs, o_refs):
    o_refs[0][...] = jax.nn.sigmoid(accs[0]).astype(o_refs[0].dtype)


def _rmsw_epilogue(accs, e_refs, o_refs):
    a = accs[0]
    y = a * lax.rsqrt(jnp.mean(a * a, axis=-1, keepdims=True) + EPS) * e_refs[0][0]
    o_refs[0][...] = y.astype(o_refs[0].dtype)


def _kvlatent_epilogue(accs, e_refs, o_refs):
    a = accs[0]
    w_ref, cos_ref, sina_ref, sinb_ref = e_refs
    ckv = a[:, :MLA_KV_RANK]
    y = ckv * lax.rsqrt(jnp.mean(ckv * ckv, axis=-1, keepdims=True) + EPS) * w_ref[0]
    o_refs[0][...] = y.astype(BF16)
    kr = a[:, MLA_KV_RANK:MLA_KV_RANK + LANES]
    o_refs[1][...] = _rope128(kr, cos_ref[...], sina_ref[...], sinb_ref[...]).astype(BF16)
    o_refs[2][...] = a[:, MLA_KV_RANK + LANES:]


def _qrope_epilogue(accs, e_refs, o_refs, *, heads_per_tile):
    a = accs[0] * Q_SCALE_LOG2
    cos_ref, sina_ref, sinb_ref = e_refs
    cos, sina, sinb = cos_ref[...], sina_ref[...], sinb_ref[...]
    for c in range(heads_per_tile):
        sl = slice(c * LANES, (c + 1) * LANES)
        o_refs[0][:, sl] = _rope128(a[:, sl], cos, sina, sinb).astype(BF16)


def _scaled_epilogue(accs, e_refs, o_refs, *, scale):
    o_refs[0][...] = (accs[0] * scale).astype(o_refs[0].dtype)


def _band(ts, lo, hi, first_tile):
    t = lax.broadcasted_iota(jnp.int32, (ts, HALO + ts), 0)
    s = lax.broadcasted_iota(jnp.int32, (ts, HALO + ts), 1)
    d = t + HALO - s
    ok = jnp.where(d >= lo, jnp.where(d < hi, 1.0, 0.0), 0.0)
    ok = jnp.where(s < HALO, jnp.where(first_tile, 0.0, ok), ok)
    return ok.astype(BF16)


def _pool_kernel(u_ref, halo_ref, pw_ref, ps_ref, o_ref, *, ts):
    i = pl.program_id(1)
    u = u_ref[0]
    ucat = jnp.concatenate([halo_ref[0], u], axis=0)
    tg = i * ts + lax.broadcasted_iota(jnp.int32, (ts, 1), 0)
    for g, w in enumerate(POOL_WINDOWS):
        sl = slice(g * POOL_GW, (g + 1) * POOL_GW)
        win = jnp.dot(_band(ts, 0, w, i == 0), ucat[:, sl], preferred_element_type=F32)
        cnt = jnp.minimum(tg + 1, w).astype(F32)
        p = win / cnt - u[:, sl].astype(F32)
        y = jnp.dot(p.astype(BF16), pw_ref[0, g], preferred_element_type=F32)
        o_ref[0, :, sl] = (y * ps_ref[0, :, sl]).astype(BF16)


def _pool(zp, pool_w, pool_scale, lay, b, s, ts=256):
    hb = ts // HALO
    ng = len(POOL_WINDOWS)
    return pl.pallas_call(
        functools.partial(_pool_kernel, ts=ts), grid=(b, s // ts),
        in_specs=[pl.BlockSpec((1, ts, MIX_W), lambda bi, i: (bi, i, 0)),
                  pl.BlockSpec((1, HALO, MIX_W), lambda bi, i: (bi, jnp.maximum(i * hb - 1, 0), 0)),
                  pl.BlockSpec((1, ng, POOL_GW, POOL_GW), lambda bi, i: (lay, 0, 0, 0)),
                  pl.BlockSpec((1, 1, MIX_W), lambda bi, i: (lay, 0, 0))],
        out_specs=pl.BlockSpec((1, ts, MIX_W), lambda bi, i: (bi, i, 0)),
        out_shape=jax.ShapeDtypeStruct((b, s, MIX_W), BF16),
        compiler_params=_params(("parallel", "parallel")),
    )(zp, zp, pool_w, pool_scale.reshape(DEPTH, 1, MIX_W))


def _conv_kernel(x_ref, halo_ref, w_ref, b_ref, sc_ref, o_ref, *, ts):
    i = pl.program_id(1)
    x = x_ref[0]
    xcat = jnp.concatenate([halo_ref[0], x], axis=0)
    w = w_ref[0]
    acc = x.astype(F32) * w[CONV_W - 1:CONV_W, :] + b_ref[0]
    for d in range(1, CONV_W):
        xs = jnp.dot(_band(ts, d, d + 1, i == 0), xcat, preferred_element_type=F32)
        acc = acc + xs * w[CONV_W - 1 - d:CONV_W - d, :]
    o_ref[0] = (acc * jax.nn.sigmoid(acc) * sc_ref[...]).astype(BF16)


def _conv_silu(zp, conv_w, conv_b, lay, b, s, ts=256):
    c = 2 * ML_HEADS * ML_DK
    hb = ts // HALO
    post = jnp.concatenate([jnp.full((c // 2,), ML_DK ** -0.5, F32), jnp.ones((c // 2,), F32)])
    return pl.pallas_call(
        functools.partial(_conv_kernel, ts=ts), grid=(b, s // ts),
        in_specs=[pl.BlockSpec((1, ts, c), lambda bi, i: (bi, i, 1)),
                  pl.BlockSpec((1, HALO, c), lambda bi, i: (bi, jnp.maximum(i * hb - 1, 0), 1)),
                  pl.BlockSpec((1, CONV_W, c), lambda bi, i: (lay, 0, 0)),
                  pl.BlockSpec((1, 1, c), lambda bi, i: (lay, 0, 0)),
                  pl.BlockSpec((1, c), lambda bi, i: (0, 0))],
        out_specs=pl.BlockSpec((1, ts, c), lambda bi, i: (bi, i, 0)),
        out_shape=jax.ShapeDtypeStruct((b, s, c), BF16),
        compiler_params=_params(("parallel", "parallel")),
    )(zp, zp, conv_w, conv_b.reshape(DEPTH, 1, c), post.reshape(1, c))


ATTN_HEADS_PER_STEP = 4


def _attn_kernel(qn_ref, qr_ref, kn_ref, kr_ref, v_ref, pq_ref, pk_ref, o_ref, m_sc, l_sc, acc_sc, *, tq):
    i = pl.program_id(2)
    nt = (((1,), (1,)), ((), ()))
    tn = (((0,), (0,)), ((), ()))
    hs = [slice(g * LANES, (g + 1) * LANES) for g in range(ATTN_HEADS_PER_STEP)]

    def block(start, mask, first):
        kr = kr_ref[0, pl.ds(start, tq), :]
        for g, sl in enumerate(hs):
            q = jnp.concatenate([qn_ref[0, :, sl], qr_ref[0, :, sl]], axis=-1)
            k = jnp.concatenate([kn_ref[0, pl.ds(start, tq), sl], kr], axis=-1)
            st = lax.dot_general(k, q, nt, preferred_element_type=F32)
            if mask is not None:
                st = jnp.where(mask, st, -jnp.inf)
            v = v_ref[0, pl.ds(start, tq), sl]
            smax = jnp.max(st, axis=0, keepdims=True)
            if first:
                m_new = smax
                p = jnp.exp2(st - m_new)
                l_sc[g] = jnp.sum(p, axis=0, keepdims=True)
                acc_sc[g] = lax.dot_general(v, p.astype(BF16), tn, preferred_element_type=F32)
            else:
                m_old = m_sc[g]
                m_new = jnp.maximum(m_old, smax)
                alpha = jnp.exp2(m_old - m_new)
                p = jnp.exp2(st - m_new)
                l_sc[g] = alpha * l_sc[g] + jnp.sum(p, axis=0, keepdims=True)
                acc_sc[g] = alpha * acc_sc[g] + lax.dot_general(v, p.astype(BF16), tn,
                                                                preferred_element_type=F32)
            m_sc[g] = m_new

    sh = CHUNK.bit_length() - 1
    mask = lax.shift_right_arithmetic(pk_ref[0], sh) <= lax.shift_right_arithmetic(pq_ref[0], sh)
    block(pl.multiple_of(i * tq, tq), mask, True)

    def body(j, carry):
        block(pl.multiple_of(j * tq, tq), None, False)
        return carry

    lax.fori_loop(0, i, body, 0)
    for g, sl in enumerate(hs):
        o_ref[0, :, sl] = (acc_sc[g] / l_sc[g]).T.astype(o_ref.dtype)


def _attention(q, kv, kr, positions, tq=512):
    b, s, _ = q.shape
    G = ATTN_HEADS_PER_STEP
    w = G * LANES
    ng = MLA_HEADS // G
    return pl.pallas_call(
        functools.partial(_attn_kernel, tq=tq), grid=(b, ng, s // tq),
        in_specs=[pl.BlockSpec((1, tq, w), lambda bi, hi, i: (bi, i, hi)),
                  pl.BlockSpec((1, tq, w), lambda bi, hi, i: (bi, i, ng + hi)),
                  pl.BlockSpec((1, s, w), lambda bi, hi, i: (bi, 0, hi)),
                  pl.BlockSpec((1, s, LANES), lambda bi, hi, i: (bi, 0, 0)),
                  pl.BlockSpec((1, s, w), lambda bi, hi, i: (bi, 0, ng + hi)),
                  pl.BlockSpec((1, 1, tq), lambda bi, hi, i: (bi, 0, i)),
                  pl.BlockSpec((1, tq, 1), lambda bi, hi, i: (bi, i, 0))],
        out_specs=pl.BlockSpec((1, tq, w), lambda bi, hi, i: (bi, i, hi)),
        out_shape=jax.ShapeDtypeStruct((b, s, MLA_HEADS * MLA_V), BF16),
        scratch_shapes=[pltpu.VMEM((G, 1, tq), F32), pltpu.VMEM((G, 1, tq), F32),
                        pltpu.VMEM((G, MLA_V, tq), F32)],
        compiler_params=_params(("parallel", "parallel", "arbitrary")),
    )(q, q, kv, kr, kv, positions.reshape(b, 1, s), positions.reshape(b, s, 1))


def _split3_dot(a_bf16, x):
    hi = x.astype(BF16)
    r1 = x - hi.astype(F32)
    mid = r1.astype(BF16)
    lo = (r1 - mid.astype(F32)).astype(BF16)
    return (jnp.dot(a_bf16, hi, preferred_element_type=F32)
            + jnp.dot(a_bf16, mid, preferred_element_type=F32)
            + jnp.dot(a_bf16, lo, preferred_element_type=F32))


def _mlstm_kernel(qk_ref, v_ref, g_ref, og_ref, gb_ref, hn_ref, o_ref, ct_ref, m_ref):
    cidx = pl.program_id(1)

    @pl.when(cidx == 0)
    def _():
        ct_ref[...] = jnp.zeros_like(ct_ref)
        m_ref[...] = jnp.zeros_like(m_ref)

    L = CHUNK
    g = g_ref[0]
    gb = gb_ref[...]
    logi = g[:, :LANES] + gb[:, :LANES]
    xf = g[:, LANES:] + gb[:, LANES:]
    logf = jnp.minimum(xf, 0.0) - jnp.log1p(jnp.exp(-jnp.abs(xf)))
    row = lax.broadcasted_iota(jnp.int32, (L, L), 0)
    col = lax.broadcasted_iota(jnp.int32, (L, L), 1)
    tril = row >= col
    bcum = _split3_dot(jnp.where(tril, 1.0, 0.0).astype(BF16), logf)
    x = logi - bcum
    xt = x.T
    m_row = m_ref[...]
    b_last = bcum[L - 1:L, :]
    m_new = jnp.maximum(b_last + m_row, b_last + jnp.max(x, axis=0, keepdims=True))
    decay = jnp.exp(b_last + m_row - m_new)
    ws_all = jnp.exp(b_last + x - m_new)
    inter_all = bcum + m_row
    ones_col = jnp.where(lax.broadcasted_iota(jnp.int32, (L, LANES), 1) == 0, 1.0, 0.0).astype(BF16)
    nt = (((1,), (1,)), ((), ()))
    tn = (((0,), (0,)), ((), ()))
    for h in range(ML_HEADS):
        q = qk_ref[0, :, h * ML_DK:(h + 1) * ML_DK]
        k = qk_ref[0, :, (ML_HEADS + h) * ML_DK:(ML_HEADS + h + 1) * ML_DK]
        vaug = jnp.concatenate([v_ref[0, :, h * ML_DV:(h + 1) * ML_DV], ones_col], axis=-1)
        dlog = jnp.where(tril, bcum[:, h:h + 1] + xt[h:h + 1, :], -jnp.inf)
        inter = inter_all[:, h:h + 1]
        mt = jnp.maximum(inter, jnp.max(dlog, axis=-1, keepdims=True))
        w_inter = jnp.exp(inter - mt)
        sc = lax.dot_general(q, k, nt, preferred_element_type=F32) * jnp.exp(dlog - mt)
        ct = ct_ref[h]
        num = (jnp.dot(sc.astype(BF16), vaug, preferred_element_type=F32)
               + w_inter * jnp.dot(q, ct.astype(BF16), preferred_element_type=F32))
        den = num[:, ML_DV:ML_DV + 1]
        hh = num[:, :ML_DV] / jnp.maximum(jnp.abs(den), jnp.exp(-mt))
        wv = (ws_all[:, h:h + 1] * vaug.astype(F32)).astype(BF16)
        ct_ref[h] = decay[:, h:h + 1] * ct + lax.dot_general(k, wv, tn, preferred_element_type=F32)
        hs = slice(h * ML_DV, (h + 1) * ML_DV)
        hn = hh * lax.rsqrt(jnp.mean(hh * hh, axis=-1, keepdims=True) + EPS) * hn_ref[0, :, hs]
        o_ref[0, :, hs] = (og_ref[0, :, hs].astype(F32) * hn).astype(BF16)
    m_ref[...] = m_new


def _mlstm(qk, zp, gates, zs, gate_b, head_norm, lay):
    b, s, _ = qk.shape
    gb = jnp.zeros((1, 2 * LANES), F32)
    gb = gb.at[0, :ML_HEADS].set(gate_b[:ML_HEADS]).at[0, LANES:LANES + ML_HEADS].set(gate_b[ML_HEADS:])
    blk = lambda col: pl.BlockSpec((1, CHUNK, MIX_W), lambda bi, c, col=col: (bi, c, col))
    return pl.pallas_call(
        _mlstm_kernel, grid=(b, s // CHUNK),
        in_specs=[blk(0), blk(2),
                  pl.BlockSpec((1, CHUNK, 2 * LANES), lambda bi, c: (bi, c, 0)),
                  blk(0),
                  pl.BlockSpec((1, 2 * LANES), lambda bi, c: (0, 0)),
                  pl.BlockSpec((1, 1, MIX_W), lambda bi, c: (lay, 0, 0))],
        out_specs=blk(0),
        out_shape=jax.ShapeDtypeStruct((b, s, MIX_W), BF16),
        scratch_shapes=[pltpu.VMEM((ML_HEADS, ML_DK, ML_DV + LANES), F32), pltpu.VMEM((1, LANES), F32)],
        compiler_params=_params(("parallel", "arbitrary")),
    )(qk, zp, gates, zs, gb, head_norm.reshape(DEPTH, 1, MIX_W))


def _merge_kernel(ya_ref, yb_ref, yc_ref, wa_ref, wb_ref, wc_ref, ga_ref, gb_ref, gc_ref, o_ref):
    acc = ga_ref[...].astype(F32) * jnp.dot(ya_ref[...], wa_ref[0, 0], preferred_element_type=F32)
    acc += gb_ref[...].astype(F32) * jnp.dot(yb_ref[...], wb_ref[0, 0], preferred_element_type=F32)
    acc += gc_ref[...].astype(F32) * jnp.dot(yc_ref[...], wc_ref[0, 0], preferred_element_type=F32)
    o_ref[...] = acc.astype(BF16)


def _merge(ya, yb, yc, w_branch, lay, zs, tm=1024, tn=512):
    m = ya.shape[0]
    yspec = pl.BlockSpec((tm, MIX_W), lambda i, j: (i, 0))
    wspec = lambda k: pl.BlockSpec((1, 1, MIX_W, tn), lambda i, j, k=k: (lay, k, 0, j))
    goff = MIX_W // tn
    gspec = lambda k: pl.BlockSpec((tm, tn), lambda i, j, k=k: (i, goff + k * (D_MODEL // tn) + j))
    return pl.pallas_call(
        _merge_kernel, grid=(m // tm, D_MODEL // tn),
        in_specs=[yspec, yspec, yspec, wspec(0), wspec(1), wspec(2), gspec(0), gspec(1), gspec(2)],
        out_specs=pl.BlockSpec((tm, tn), lambda i, j: (i, j)),
        out_shape=jax.ShapeDtypeStruct((m, D_MODEL), BF16),
        compiler_params=_params(("parallel", "parallel")),
    )(ya, yb, yc, w_branch, w_branch, w_branch, zs, zs, zs)


_REGROUP = ((_OFF[0], _OFF[1], None), (_OFF[4], _OFF[7], None),
            (_OFF[7], _OFF[8], None), (_OFF[10], _OFF[11], None),
            (_OFF[1], _OFF[2], None),
            (_OFF[2], _OFF[3], None), (_OFF[3], _OFF[4], LANES),
            (_OFF[8], _OFF[9], LANES), (_OFF[9], _OFF[10], LANES))
N_REGROUP = sum(p if p else hi - lo for lo, hi, p in _REGROUP)
assert N_REGROUP == COL_LAT + N_LAT


def _regroup_kernel(w_ref, o_ref):
    col = 0
    for lo, hi, padded in _REGROUP:
        n = hi - lo
        o_ref[0, :, col:col + n] = w_ref[0, :, lo:hi].astype(BF16)
        if padded:
            o_ref[0, :, col + n:col + padded] = jnp.zeros((o_ref.shape[1], padded - n), BF16)
        col += padded if padded else n


def _regroup_mix_in(w, tr=128):
    dep, k, n = w.shape
    return pl.pallas_call(
        _regroup_kernel, grid=(dep, k // tr),
        in_specs=[pl.BlockSpec((1, tr, n), lambda l, i: (l, i, 0))],
        out_specs=pl.BlockSpec((1, tr, N_REGROUP), lambda l, i: (l, i, 0)),
        out_shape=jax.ShapeDtypeStruct((dep, k, N_REGROUP), BF16),
        compiler_params=_params(("parallel", "parallel")),
    )(w)


def _prep_mixer_weights(w_mix_in, mla_w_uq, mla_w_ukv):
    w_all = _regroup_mix_in(w_mix_in)
    uq = mla_w_uq.reshape(DEPTH, MLA_Q_RANK, MLA_HEADS, MLA_NOPE + MLA_ROPE)
    w_qn = uq[..., :MLA_NOPE].reshape(DEPTH, MLA_Q_RANK, MLA_HEADS * MLA_NOPE)
    w_qr = jnp.pad(uq[..., MLA_NOPE:], ((0, 0), (0, 0), (0, 0), (0, LANES - MLA_ROPE)))
    w_q = jnp.concatenate([w_qn, w_qr.reshape(DEPTH, MLA_Q_RANK, MLA_HEADS * LANES)], axis=-1).astype(BF16)
    ukv = mla_w_ukv.reshape(DEPTH, MLA_KV_RANK, MLA_HEADS, 2, MLA_NOPE)
    w_kv = ukv.transpose(0, 1, 3, 2, 4).reshape(DEPTH, MLA_KV_RANK, 2 * MLA_HEADS * MLA_NOPE).astype(BF16)
    return w_all, w_q, w_kv


def _mixer(h, lay, positions, rope_tabs, w_all, w_q, w_kv, pool_w, pool_scale, mla_q_norm, mla_kv_norm,
           ml_conv_w, ml_conv_b, ml_gate_b, ml_head_norm, w_branch, tm=1024):
    b, s, d = h.shape
    m = b * s
    h2 = h.reshape(m, d)
    cos, sina, sinb = rope_tabs
    tab = lambda a: (a, (tm, LANES), lambda i, j: (i, 0))
    tile = lambda i, j: (i, j)
    layvec = lambda a: (a.reshape(DEPTH, 1, -1), (1, 1, a.shape[-1]), lambda i, j: (lay, 0, 0))

    zp = _mm(h2, [(w_all, lay, COL_PLAIN)], [], [((m, N_PLAIN), BF16, (tm, 512), tile)],
             _plain_epilogue, tm=tm, tn=512, nj=N_PLAIN // 512)[0]
    zs = _mm(h2, [(w_all, lay, COL_SIG)], [], [((m, N_SIG), BF16, (tm, 512), tile)],
             _sigmoid_epilogue, tm=tm, tn=512, nj=N_SIG // 512)[0]
    cqn = _mm(h2, [(w_all, lay, COL_CQ)], [layvec(mla_q_norm)],
              [((m, MLA_Q_RANK), BF16, (tm, MLA_Q_RANK), tile)],
              _rmsw_epilogue, tm=tm, tn=MLA_Q_RANK, nj=1)[0]
    ckvn, kr, gates = _mm(
        h2, [(w_all, lay, COL_LAT)], [layvec(mla_kv_norm), tab(cos), tab(sina), tab(sinb)],
        [((m, MLA_KV_RANK), BF16, (tm, MLA_KV_RANK), tile),
         ((m, LANES), BF16, (tm, LANES), tile),
         ((m, 2 * LANES), F32, (tm, 2 * LANES), tile)],
        _kvlatent_epilogue, tm=tm, tn=N_LAT, nj=1)

    zp3 = zp.reshape(b, s, N_PLAIN)
    ya = _pool(zp3, pool_w, pool_scale, lay, b, s)

    nq = MLA_HEADS * LANES
    q = _mm(cqn, [(w_q, lay, 0)], [tab(cos), tab(sina), tab(sinb)], [((m, 2 * nq), BF16, (tm, 512), tile)],
            functools.partial(_q_epilogue, heads_per_tile=512 // LANES, n_nope_tiles=nq // 512),
            tm=tm, tn=512, nj=2 * nq // 512)[0]
    kv = _mm(ckvn, [(w_kv, lay, 0)], [], [((m, 2 * nq), BF16, (tm, 512), tile)],
             _plain_epilogue, tm=tm, tn=512, nj=2 * nq // 512)[0]
    yb = _attention(q.reshape(b, s, 2 * nq), kv.reshape(b, s, 2 * nq), kr.reshape(b, s, LANES), positions)

    qk = _conv_silu(zp3, ml_conv_w, ml_conv_b, lay, b, s)
    yc = _mlstm(qk, zp3, gates.reshape(b, s, 2 * LANES), zs.reshape(b, s, N_SIG), ml_gate_b[lay],
                ml_head_norm, lay)

    return _merge(ya.reshape(m, MIX_W), yb.reshape(m, MIX_W), yc.reshape(m, MIX_W), w_branch, lay, zs)


def _q_epilogue(accs, e_refs, o_refs, *, heads_per_tile, n_nope_tiles):
    j = pl.program_id(1)

    @pl.when(j < n_nope_tiles)
    def _():
        _scaled_epilogue(accs, (), o_refs, scale=Q_SCALE_LOG2)

    @pl.when(j >= n_nope_tiles)
    def _():
        _qrope_epilogue(accs, e_refs, o_refs, heads_per_tile=heads_per_tile)


def kernel(x, c, positions, w_ada, b_ada, ada_table, ffn_a_w_in, ffn_a_w_out, w_mix_in, pool_w, pool_scale, mla_q_norm, mla_w_uq, mla_kv_norm, mla_w_ukv, ml_conv_w, ml_conv_b, ml_gate_b, ml_head_norm, w_branch, w_out, ffn_b_w_in, ffn_b_w_out, final_norm):
    b, s, d = x.shape
    m = b * s
    mod = _ada(c, w_ada, b_ada, ada_table)
    rope_tabs = _rope_tables(positions)
    fa_in, fa_out = ffn_a_w_in, ffn_a_w_out.astype(BF16)
    fb_in, fb_out = ffn_b_w_in, ffn_b_w_out.astype(BF16)
    w_all, w_q, w_kv = _prep_mixer_weights(w_mix_in, mla_w_uq, mla_w_ukv)
    pool_wb, w_branchb, w_outb = pool_w.astype(BF16), w_branch.astype(BF16), w_out.astype(BF16)
    for l in range(DEPTH):
        md = mod[l]
        h = _normmod(x, md[:, 0], md[:, 1])
        x = _ffn(x, h, fa_in, fa_out, l, md[:, 2])
        h = _normmod(x, md[:, 3], md[:, 4])
        merged = _mixer(h, l, positions, rope_tabs, w_all, w_q, w_kv, pool_wb, pool_scale, mla_q_norm,
                        mla_kv_norm, ml_conv_w, ml_conv_b, ml_gate_b, ml_head_norm, w_branchb)
        x = _resid_mm(merged, w_outb, l, x.reshape(m, d), md[:, 5], 1.0, s).reshape(b, s, d)
        h = _normmod(x, md[:, 6], md[:, 7])
        x = _ffn(x, h, fb_in, fb_out, l, md[:, 8])
    return _finalnorm(x, final_norm)
```

```python
import functools

import numpy as np
import jax
import jax.numpy as jnp
from jax import lax
from jax.experimental import pallas as pl
from jax.experimental.pallas import tpu as pltpu

F32 = jnp.float32
BF16 = jnp.bfloat16

D_MODEL = 4096
DEPTH = 2
CHUNK = 64
EPS = 1e-6
D_FF = 2 * D_MODEL
MIX_W = D_MODEL // 2
N_BRANCH = 3
N_MOD = 9
POOL_WINDOWS = (2, 4, 8, 16)
POOL_GW = MIX_W // len(POOL_WINDOWS)
MLA_NOPE = 128
MLA_ROPE = 64
MLA_V = 128
MLA_HEADS = MIX_W // MLA_V
MLA_Q_RANK = D_MODEL // 4
MLA_KV_RANK = 512
MLA_SCALE = (MLA_NOPE + MLA_ROPE) ** -0.5
Q_SCALE_LOG2 = MLA_SCALE * float(np.log2(np.e))
ROPE_THETA = 10000.0
ML_HEADS = 8
ML_DK = 128
ML_DV = MIX_W // ML_HEADS
CONV_W = 4

_SPLITS = (MIX_W, MLA_Q_RANK, MLA_KV_RANK, MLA_ROPE, ML_HEADS * ML_DK, ML_HEADS * ML_DK,
           ML_HEADS * ML_DV, ML_HEADS * ML_DV, ML_HEADS, ML_HEADS, N_BRANCH * D_MODEL)
_OFF = tuple(int(v) for v in np.cumsum((0,) + _SPLITS))

LANES = 128
HALO = 16
VMEM_LIMIT = 60 * 1024 * 1024

N_PLAIN = 3 * MIX_W
N_SIG = MIX_W + N_BRANCH * D_MODEL
N_LAT = MLA_KV_RANK + 3 * LANES
COL_PLAIN = 0
COL_SIG = COL_PLAIN + N_PLAIN
COL_CQ = COL_SIG + N_SIG
COL_LAT = COL_CQ + MLA_Q_RANK


def _params(sem):
    return pltpu.CompilerParams(dimension_semantics=sem, vmem_limit_bytes=VMEM_LIMIT)


def _mm_body(*refs, nw, ne, no, nk, epilogue):
    x_ref = refs[0]
    w_refs = refs[1:1 + nw]
    e_refs = refs[1 + nw:1 + nw + ne]
    o_refs = refs[1 + nw + ne:1 + nw + ne + no]
    acc_refs = refs[1 + nw + ne + no:]
    if nk == 1:
        accs = [jnp.dot(x_ref[...], w[0], preferred_element_type=F32) for w in w_refs]
        epilogue(accs, e_refs, o_refs)
        return
    k = pl.program_id(2)

    @pl.when(k == 0)
    def _():
        for a, w in zip(acc_refs, w_refs):
            a[...] = jnp.dot(x_ref[...], w[0], preferred_element_type=F32)

    @pl.when(k > 0)
    def _():
        for a, w in zip(acc_refs, w_refs):
            a[...] += jnp.dot(x_ref[...], w[0], preferred_element_type=F32)

    @pl.when(k == nk - 1)
    def _():
        epilogue([a[...] for a in acc_refs], e_refs, o_refs)


def _mm(x, ws, extras, outs, epilogue, *, tm, tn, nj, nk=1):
    m, kdim = x.shape
    tk = kdim // nk
    in_specs = [pl.BlockSpec((tm, tk), lambda i, j, k: (i, k))]
    for _, lay, col in ws:
        assert col % tn == 0
        in_specs.append(pl.BlockSpec((1, tk, tn), lambda i, j, k, lay=lay, off=col // tn: (lay, k, off + j)))
    for _, blk, f in extras:
        in_specs.append(pl.BlockSpec(blk, lambda i, j, k, f=f: f(i, j)))
    out_specs = [pl.BlockSpec(blk, lambda i, j, k, f=f: f(i, j)) for _, _, blk, f in outs]
    out_shape = [jax.ShapeDtypeStruct(s, d) for s, d, _, _ in outs]
    scratch = [pltpu.VMEM((tm, tn), F32) for _ in ws] if nk > 1 else []
    body = functools.partial(_mm_body, nw=len(ws), ne=len(extras), no=len(outs), nk=nk,
                             epilogue=epilogue)
    return pl.pallas_call(
        body, grid=(m // tm, nj, nk), in_specs=in_specs, out_specs=out_specs,
        out_shape=out_shape, scratch_shapes=scratch,
        compiler_params=_params(("parallel", "parallel", "arbitrary")),
    )(x, *[w for w, _, _ in ws], *[a for a, _, _ in extras])


def _rope128(v, cos, sina, sinb):
    return (v * cos + pltpu.roll(v, LANES - MLA_ROPE // 2, 1) * sina
            + pltpu.roll(v, MLA_ROPE // 2, 1) * sinb)


def _ada_kernel(c_ref, w_ref, b_ref, t_ref, o_ref):
    c = c_ref[...]
    s = c * jax.nn.sigmoid(c)
    acc = jnp.dot(s.astype(BF16), w_ref[...].astype(BF16), preferred_element_type=F32) + b_ref[...]
    for l in range(DEPTH):
        o_ref[l] = acc + t_ref[l]


def _ada(c, w_ada, b_ada, ada_table):
    b = c.shape[0]
    rows = 8
    c8 = jnp.zeros((rows, D_MODEL), F32).at[:b].set(c)
    n = N_MOD * D_MODEL
    tn = 1024
    out = pl.pallas_call(
        _ada_kernel, grid=(n // tn,),
        in_specs=[pl.BlockSpec((rows, D_MODEL), lambda j: (0, 0)),
                  pl.BlockSpec((D_MODEL, tn), lambda j: (0, j)),
                  pl.BlockSpec((1, tn), lambda j: (0, j)),
                  pl.BlockSpec((DEPTH, 1, tn), lambda j: (0, 0, j))],
        out_specs=pl.BlockSpec((DEPTH, rows, tn), lambda j: (0, 0, j)),
        out_shape=jax.ShapeDtypeStruct((DEPTH, rows, n), F32),
        compiler_params=_params(("parallel",)),
    )(c8, w_ada, b_ada.reshape(1, n), ada_table.reshape(DEPTH, 1, n))
    return out[:, :b].reshape(DEPTH, b, N_MOD, D_MODEL)


def _normmod_kernel(x_ref, shift_ref, scale_ref, o_ref):
    x = x_ref[0]
    y = x * lax.rsqrt(jnp.mean(x * x, axis=-1, keepdims=True) + EPS)
    o_ref[0] = (y * (1.0 + scale_ref[0]) + shift_ref[0]).astype(o_ref.dtype)


def _finalnorm_kernel(x_ref, w_ref, o_ref):
    x = x_ref[0]
    y = x * lax.rsqrt(jnp.mean(x * x, axis=-1, keepdims=True) + EPS)
    o_ref[0] = y * w_ref[...]


def _normmod(x, shift, scale, ts=512):
    b, s, d = x.shape
    vec = pl.BlockSpec((1, 1, d), lambda bi, i: (bi, 0, 0))
    return pl.pallas_call(
        _normmod_kernel, grid=(b, s // ts),
        in_specs=[pl.BlockSpec((1, ts, d), lambda bi, i: (bi, i, 0)), vec, vec],
        out_specs=pl.BlockSpec((1, ts, d), lambda bi, i: (bi, i, 0)),
        out_shape=jax.ShapeDtypeStruct((b, s, d), BF16),
        compiler_params=_params(("parallel", "parallel")),
    )(x, shift.reshape(b, 1, d), scale.reshape(b, 1, d))


def _finalnorm(x, w, ts=512):
    b, s, d = x.shape
    return pl.pallas_call(
        _finalnorm_kernel, grid=(b, s // ts),
        in_specs=[pl.BlockSpec((1, ts, d), lambda bi, i: (bi, i, 0)),
                  pl.BlockSpec((1, d), lambda bi, i: (0, 0))],
        out_specs=pl.BlockSpec((1, ts, d), lambda bi, i: (bi, i, 0)),
        out_shape=jax.ShapeDtypeStruct((b, s, d), F32),
        compiler_params=_params(("parallel", "parallel")),
    )(x, w.reshape(1, d))


def _resid_epilogue(accs, e_refs, o_refs, *, coef):
    x_ref, g_ref = e_refs
    o_refs[0][...] = x_ref[...] + (coef * g_ref[0]) * accs[0]


def _resid_mm(a, w, lay, x2d, gate, coef, seq, *, tm=1024, tn=512, nk=1):
    m, n = x2d.shape
    per_b = seq // tm
    return _mm(a, [(w, lay, 0)],
               [(x2d, (tm, tn), lambda i, j: (i, j)),
                (gate.reshape(-1, 1, n), (1, 1, tn), lambda i, j: (i // per_b, 0, j))],
               [((m, n), F32, (tm, tn), lambda i, j: (i, j))],
               functools.partial(_resid_epilogue, coef=coef), tm=tm, tn=tn, nj=n // tn, nk=nk)[0]


def _swiglu_kernel(x_ref, wg_ref, wu_ref, o_ref, wg_sc, wu_sc):
    @pl.when(pl.program_id(1) == 0)
    def _():
        wg_sc[...] = wg_ref[0].astype(BF16)
        wu_sc[...] = wu_ref[0].astype(BF16)

    x = x_ref[...]
    g = jnp.dot(x, wg_sc[...], preferred_element_type=F32)
    u = jnp.dot(x, wu_sc[...], preferred_element_type=F32)
    o_ref[...] = (g * jax.nn.sigmoid(g) * u).astype(BF16)


def _swiglu_mm(h2, w_in, lay, *, tm=1024, tn=256):
    m, d = h2.shape
    nj = D_FF // tn
    return pl.pallas_call(
        _swiglu_kernel, grid=(nj, m // tm),
        in_specs=[pl.BlockSpec((tm, d), lambda j, i: (i, 0)),
                  pl.BlockSpec((1, d, tn), lambda j, i: (lay, 0, j)),
                  pl.BlockSpec((1, d, tn), lambda j, i: (lay, 0, nj + j))],
        out_specs=pl.BlockSpec((tm, tn), lambda j, i: (i, j)),
        out_shape=jax.ShapeDtypeStruct((m, D_FF), BF16),
        scratch_shapes=[pltpu.VMEM((d, tn), BF16), pltpu.VMEM((d, tn), BF16)],
        compiler_params=_params(("parallel", "arbitrary")),
    )(h2, w_in, w_in)


def _ffn(x, h, w_in, w_out, lay, gate):
    b, s, d = x.shape
    m = b * s
    a = _swiglu_mm(h.reshape(m, d), w_in, lay)
    y = _resid_mm(a, w_out, lay, x.reshape(m, d), gate, 0.5, s, tm=512, tn=512)
    return y.reshape(b, s, d)


def _rope_tab_kernel(p_ref, inv_ref, cos_ref, sina_ref, sinb_ref):
    ang = p_ref[...].astype(F32) * inv_ref[...]
    lane = lax.broadcasted_iota(jnp.int32, ang.shape, 1)
    half = MLA_ROPE // 2
    c = jnp.cos(ang)
    s = jnp.sin(ang)
    cos_ref[...] = jnp.where(lane < MLA_ROPE, c, 0.0)
    sina_ref[...] = jnp.where(lane < half, -s, 0.0)
    sinb_ref[...] = jnp.where(lane >= half, jnp.where(lane < MLA_ROPE, s, 0.0), 0.0)


def _rope_tables(positions, ts=512):
    m = positions.size
    half = MLA_ROPE // 2
    inv = ROPE_THETA ** (-jnp.arange(0, MLA_ROPE, 2, dtype=F32) / MLA_ROPE)
    inv128 = jnp.concatenate([inv, inv, jnp.zeros((LANES - 2 * half,), F32)]).reshape(1, LANES)
    spec = pl.BlockSpec((ts, LANES), lambda i: (i, 0))
    shp = jax.ShapeDtypeStruct((m, LANES), F32)
    return pl.pallas_call(
        _rope_tab_kernel, grid=(m // ts,),
        in_specs=[pl.BlockSpec((ts, 1), lambda i: (i, 0)), pl.BlockSpec((1, LANES), lambda i: (0, 0))],
        out_specs=[spec, spec, spec], out_shape=[shp, shp, shp],
        compiler_params=_params(("parallel",)),
    )(positions.reshape(m, 1), inv128)


def _plain_epilogue(accs, e_refs, o_refs):
    o_refs[0][...] = accs[0].astype(o_refs[0].dtype)


def _sigmoid_epilogue(accs, e_refs, o_refs):
    o_refs[0][...] = jax.nn.sigmoid(accs[0]).astype(o_refs[0].dtype)


def _rmsw_epilogue(accs, e_refs, o_refs):
    a = accs[0]
    y = a * lax.rsqrt(jnp.mean(a * a, axis=-1, keepdims=True) + EPS) * e_refs[0][0]
    o_refs[0][...] = y.astype(o_refs[0].dtype)


def _kvlatent_epilogue(accs, e_refs, o_refs):
    a = accs[0]
    w_ref, cos_ref, sina_ref, sinb_ref = e_refs
    ckv = a[:, :MLA_KV_RANK]
    y = ckv * lax.rsqrt(jnp.mean(ckv * ckv, axis=-1, keepdims=True) + EPS) * w_ref[0]
    o_refs[0][...] = y.astype(BF16)
    kr = a[:, MLA_KV_RANK:MLA_KV_RANK + LANES]
    o_refs[1][...] = _rope128(kr, cos_ref[...], sina_ref[...], sinb_ref[...]).astype(BF16)
    o_refs[2][...] = a[:, MLA_KV_RANK + LANES:]


def _qrope_epilogue(accs, e_refs, o_refs, *, heads_per_tile):
    a = accs[0] * Q_SCALE_LOG2
    cos_ref, sina_ref, sinb_ref = e_refs
    cos, sina, sinb = cos_ref[...], sina_ref[...], sinb_ref[...]
    for c in range(heads_per_tile):
        sl = slice(c * LANES, (c + 1) * LANES)
        o_refs[0][:, sl] = _rope128(a[:, sl], cos, sina, sinb).astype(BF16)


def _scaled_epilogue(accs, e_refs, o_refs, *, scale):
    o_refs[0][...] = (accs[0] * scale).astype(o_refs[0].dtype)


def _band(ts, lo, hi, first_tile):
    t = lax.broadcasted_iota(jnp.int32, (ts, HALO + ts), 0)
    s = lax.broadcasted_iota(jnp.int32, (ts, HALO + ts), 1)
    d = t + HALO - s
    ok = jnp.where(d >= lo, jnp.where(d < hi, 1.0, 0.0), 0.0)
    ok = jnp.where(s < HALO, jnp.where(first_tile, 0.0, ok), ok)
    return ok.astype(BF16)


def _pool_kernel(u_ref, halo_ref, pw_ref, ps_ref, o_ref, *, ts):
    i = pl.program_id(1)
    u = u_ref[0]
    ucat = jnp.concatenate([halo_ref[0], u], axis=0)
    tg = i * ts + lax.broadcasted_iota(jnp.int32, (ts, 1), 0)
    for g, w in enumerate(POOL_WINDOWS):
        sl = slice(g * POOL_GW, (g + 1) * POOL_GW)
        win = jnp.dot(_band(ts, 0, w, i == 0), ucat[:, sl], preferred_element_type=F32)
        cnt = jnp.minimum(tg + 1, w).astype(F32)
        p = win / cnt - u[:, sl].astype(F32)
        y = jnp.dot(p.astype(BF16), pw_ref[0, g], preferred_element_type=F32)
        o_ref[0, :, sl] = (y * ps_ref[0, :, sl]).astype(BF16)


def _pool(zp, pool_w, pool_scale, lay, b, s, ts=256):
    hb = ts // HALO
    ng = len(POOL_WINDOWS)
    return pl.pallas_call(
        functools.partial(_pool_kernel, ts=ts), grid=(b, s // ts),
        in_specs=[pl.BlockSpec((1, ts, MIX_W), lambda bi, i: (bi, i, 0)),
                  pl.BlockSpec((1, HALO, MIX_W), lambda bi, i: (bi, jnp.maximum(i * hb - 1, 0), 0)),
                  pl.BlockSpec((1, ng, POOL_GW, POOL_GW), lambda bi, i: (lay, 0, 0, 0)),
                  pl.BlockSpec((1, 1, MIX_W), lambda bi, i: (lay, 0, 0))],
        out_specs=pl.BlockSpec((1, ts, MIX_W), lambda bi, i: (bi, i, 0)),
        out_shape=jax.ShapeDtypeStruct((b, s, MIX_W), BF16),
        compiler_params=_params(("parallel", "parallel")),
    )(zp, zp, pool_w, pool_scale.reshape(DEPTH, 1, MIX_W))


def _conv_kernel(x_ref, halo_ref, w_ref, b_ref, sc_ref, o_ref, *, ts):
    i = pl.program_id(1)
    x = x_ref[0]
    xcat = jnp.concatenate([halo_ref[0], x], axis=0)
    w = w_ref[0]
    acc = x.astype(F32) * w[CONV_W - 1:CONV_W, :] + b_ref[0]
    for d in range(1, CONV_W):
        xs = jnp.dot(_band(ts, d, d + 1, i == 0), xcat, preferred_element_type=F32)
        acc = acc + xs * w[CONV_W - 1 - d:CONV_W - d, :]
    o_ref[0] = (acc * jax.nn.sigmoid(acc) * sc_ref[...]).astype(BF16)


def _conv_silu(zp, conv_w, conv_b, lay, b, s, ts=256):
    c = 2 * ML_HEADS * ML_DK
    hb = ts // HALO
    post = jnp.concatenate([jnp.full((c // 2,), ML_DK ** -0.5, F32), jnp.ones((c // 2,), F32)])
    return pl.pallas_call(
        functools.partial(_conv_kernel, ts=ts), grid=(b, s // ts),
        in_specs=[pl.BlockSpec((1, ts, c), lambda bi, i: (bi, i, 1)),
                  pl.BlockSpec((1, HALO, c), lambda bi, i: (bi, jnp.maximum(i * hb - 1, 0), 1)),
                  pl.BlockSpec((1, CONV_W, c), lambda bi, i: (lay, 0, 0)),
                  pl.BlockSpec((1, 1, c), lambda bi, i: (lay, 0, 0)),
                  pl.BlockSpec((1, c), lambda bi, i: (0, 0))],
        out_specs=pl.BlockSpec((1, ts, c), lambda bi, i: (bi, i, 0)),
        out_shape=jax.ShapeDtypeStruct((b, s, c), BF16),
        compiler_params=_params(("parallel", "parallel")),
    )(zp, zp, conv_w, conv_b.reshape(DEPTH, 1, c), post.reshape(1, c))


ATTN_HEADS_PER_STEP = 8


def _attn_kernel(qn_ref, qr_ref, kn_ref, kr_ref, v_ref, pq_ref, pk_ref, o_ref, m_sc, l_sc, acc_sc, *, tq):
    i = pl.program_id(2)
    nt = (((1,), (1,)), ((), ()))
    tn = (((0,), (0,)), ((), ()))
    hs = [slice(g * LANES, (g + 1) * LANES) for g in range(ATTN_HEADS_PER_STEP)]

    def block(start, mask, first):
        kr = kr_ref[0, pl.ds(start, tq), :]

        def scores(g):
            sl = hs[g]
            q = jnp.concatenate([qn_ref[0, :, sl], qr_ref[0, :, sl]], axis=-1)
            k = jnp.concatenate([kn_ref[0, pl.ds(start, tq), sl], kr], axis=-1)
            st = lax.dot_general(k, q, nt, preferred_element_type=F32)
            if mask is not None:
                st = jnp.where(mask, st, -jnp.inf)
            return st

        st_next = scores(0)
        for g, sl in enumerate(hs):
            st = st_next
            if g + 1 < len(hs):
                st_next = scores(g + 1)
            v = v_ref[0, pl.ds(start, tq), sl]
            smax = jnp.max(st, axis=0, keepdims=True)
            if first:
                m_new = smax
                p = jnp.exp2(st - m_new)
                l_sc[g] = jnp.sum(p, axis=0, keepdims=True)
                acc_sc[g] = lax.dot_general(v, p.astype(BF16), tn, preferred_element_type=F32)
            else:
                m_old = m_sc[g]
                m_new = jnp.maximum(m_old, smax)
                alpha = jnp.exp2(m_old - m_new)
                p = jnp.exp2(st - m_new)
                l_sc[g] = alpha * l_sc[g] + jnp.sum(p, axis=0, keepdims=True)
                acc_sc[g] = alpha * acc_sc[g] + lax.dot_general(v, p.astype(BF16), tn,
                                                                preferred_element_type=F32)
            m_sc[g] = m_new

    sh = CHUNK.bit_length() - 1
    mask = lax.shift_right_arithmetic(pk_ref[0], sh) <= lax.shift_right_arithmetic(pq_ref[0], sh)
    block(pl.multiple_of(i * tq, tq), mask, True)

    def body(j, carry):
        block(pl.multiple_of(j * tq, tq), None, False)
        return carry

    lax.fori_loop(0, i, body, 0)
    for g, sl in enumerate(hs):
        o_ref[0, :, sl] = (acc_sc[g] / l_sc[g]).T.astype(o_ref.dtype)


def _attention(q, kv, kr, positions, tq=512):
    b, s, _ = q.shape
    G = ATTN_HEADS_PER_STEP
    w = G * LANES
    ng = MLA_HEADS // G
    return pl.pallas_call(
        functools.partial(_attn_kernel, tq=tq), grid=(b, ng, s // tq),
        in_specs=[pl.BlockSpec((1, tq, w), lambda bi, hi, i: (bi, i, hi)),
                  pl.BlockSpec((1, tq, w), lambda bi, hi, i: (bi, i, ng + hi)),
                  pl.BlockSpec((1, s, w), lambda bi, hi, i: (bi, 0, hi)),
                  pl.BlockSpec((1, s, LANES), lambda bi, hi, i: (bi, 0, 0)),
                  pl.BlockSpec((1, s, w), lambda bi, hi, i: (bi, 0, ng + hi)),
                  pl.BlockSpec((1, 1, tq), lambda bi, hi, i: (bi, 0, i)),
                  pl.BlockSpec((1, tq, 1), lambda bi, hi, i: (bi, i, 0))],
        out_specs=pl.BlockSpec((1, tq, w), lambda bi, hi, i: (bi, i, hi)),
        out_shape=jax.ShapeDtypeStruct((b, s, MLA_HEADS * MLA_V), BF16),
        scratch_shapes=[pltpu.VMEM((G, 1, tq), F32), pltpu.VMEM((G, 1, tq), F32),
                        pltpu.VMEM((G, MLA_V, tq), F32)],
        compiler_params=_params(("parallel", "parallel", "arbitrary")),
    )(q, q, kv, kr, kv, positions.reshape(b, 1, s), positions.reshape(b, s, 1))


def _split3_dot(a_bf16, x):
    hi = x.astype(BF16)
    r1 = x - hi.astype(F32)
    mid = r1.astype(BF16)
    lo = (r1 - mid.astype(F32)).astype(BF16)
    return (jnp.dot(a_bf16, hi, preferred_element_type=F32)
            + jnp.dot(a_bf16, mid, preferred_element_type=F32)
            + jnp.dot(a_bf16, lo, preferred_element_type=F32))


def _mlstm_kernel(qk_ref, v_ref, g_ref, og_ref, gb_ref, hn_ref, o_ref, ct_ref, m_ref):
    cidx = pl.program_id(1)

    @pl.when(cidx == 0)
    def _():
        ct_ref[...] = jnp.zeros_like(ct_ref)
        m_ref[...] = jnp.zeros_like(m_ref)

    L = CHUNK
    g = g_ref[0]
    gb = gb_ref[...]
    logi = g[:, :LANES] + gb[:, :LANES]
    xf = g[:, LANES:] + gb[:, LANES:]
    logf = jnp.minimum(xf, 0.0) - jnp.log1p(jnp.exp(-jnp.abs(xf)))
    row = lax.broadcasted_iota(jnp.int32, (L, L), 0)
    col = lax.broadcasted_iota(jnp.int32, (L, L), 1)
    tril = row >= col
    bcum = _split3_dot(jnp.where(tril, 1.0, 0.0).astype(BF16), logf)
    x = logi - bcum
    xt = x.T
    m_row = m_ref[...]
    b_last = bcum[L - 1:L, :]
    m_new = jnp.maximum(b_last + m_row, b_last + jnp.max(x, axis=0, keepdims=True))
    decay = jnp.exp(b_last + m_row - m_new)
    ws_all = jnp.exp(b_last + x - m_new)
    inter_all = bcum + m_row
    ones_col = jnp.where(lax.broadcasted_iota(jnp.int32, (L, LANES), 1) == 0, 1.0, 0.0).astype(BF16)
    nt = (((1,), (1,)), ((), ()))
    tn = (((0,), (0,)), ((), ()))
    def early(h):
        q = qk_ref[0, :, h * ML_DK:(h + 1) * ML_DK]
        k = qk_ref[0, :, (ML_HEADS + h) * ML_DK:(ML_HEADS + h + 1) * ML_DK]
        ct = ct_ref[h]
        qk = lax.dot_general(q, k, nt, preferred_element_type=F32)
        qc = jnp.dot(q, ct.astype(BF16), preferred_element_type=F32)
        dlog = jnp.where(tril, bcum[:, h:h + 1] + xt[h:h + 1, :], -jnp.inf)
        inter = inter_all[:, h:h + 1]
        mt = jnp.maximum(inter, jnp.max(dlog, axis=-1, keepdims=True))
        return k, ct, qk, qc, mt, jnp.exp(inter - mt), jnp.exp(dlog - mt)

    ahead = early(0)
    for h in range(ML_HEADS):
        k, ct, qk, qc, mt, w_inter, e = ahead
        if h + 1 < ML_HEADS:
            ahead = early(h + 1)
        vaug = jnp.concatenate([v_ref[0, :, h * ML_DV:(h + 1) * ML_DV], ones_col], axis=-1)
        num = jnp.dot((qk * e).astype(BF16), vaug, preferred_element_type=F32) + w_inter * qc
        den = num[:, ML_DV:ML_DV + 1]
        hh = num[:, :ML_DV] / jnp.maximum(jnp.abs(den), jnp.exp(-mt))
        wv = (ws_all[:, h:h + 1] * vaug.astype(F32)).astype(BF16)
        ct_ref[h] = decay[:, h:h + 1] * ct + lax.dot_general(k, wv, tn, preferred_element_type=F32)
        hs = slice(h * ML_DV, (h + 1) * ML_DV)
        hn = hh * lax.rsqrt(jnp.mean(hh * hh, axis=-1, keepdims=True) + EPS) * hn_ref[0, :, hs]
        o_ref[0, :, hs] = (og_ref[0, :, hs].astype(F32) * hn).astype(BF16)
    m_ref[...] = m_new


def _mlstm(qk, zp, gates, zs, gate_b, head_norm, lay):
    b, s, _ = qk.shape
    gb = jnp.zeros((1, 2 * LANES), F32)
    gb = gb.at[0, :ML_HEADS].set(gate_b[:ML_HEADS]).at[0, LANES:LANES + ML_HEADS].set(gate_b[ML_HEADS:])
    blk = lambda col: pl.BlockSpec((1, CHUNK, MIX_W), lambda bi, c, col=col: (bi, c, col))
    return pl.pallas_call(
        _mlstm_kernel, grid=(b, s // CHUNK),
        in_specs=[blk(0), blk(2),
                  pl.BlockSpec((1, CHUNK, 2 * LANES), lambda bi, c: (bi, c, 0)),
                  blk(0),
                  pl.BlockSpec((1, 2 * LANES), lambda bi, c: (0, 0)),
                  pl.BlockSpec((1, 1, MIX_W), lambda bi, c: (lay, 0, 0))],
        out_specs=blk(0),
        out_shape=jax.ShapeDtypeStruct((b, s, MIX_W), BF16),
        scratch_shapes=[pltpu.VMEM((ML_HEADS, ML_DK, ML_DV + LANES), F32), pltpu.VMEM((1, LANES), F32)],
        compiler_params=_params(("parallel", "arbitrary")),
    )(qk, zp, gates, zs, gb, head_norm.reshape(DEPTH, 1, MIX_W))


def _merge_kernel(ya_ref, yb_ref, yc_ref, wa_ref, wb_ref, wc_ref, ga_ref, gb_ref, gc_ref, o_ref):
    acc = ga_ref[...].astype(F32) * jnp.dot(ya_ref[...], wa_ref[0, 0], preferred_element_type=F32)
    acc += gb_ref[...].astype(F32) * jnp.dot(yb_ref[...], wb_ref[0, 0], preferred_element_type=F32)
    acc += gc_ref[...].astype(F32) * jnp.dot(yc_ref[...], wc_ref[0, 0], preferred_element_type=F32)
    o_ref[...] = acc.astype(BF16)


def _merge(ya, yb, yc, w_branch, lay, zs, tm=1024, tn=512):
    m = ya.shape[0]
    yspec = pl.BlockSpec((tm, MIX_W), lambda i, j: (i, 0))
    wspec = lambda k: pl.BlockSpec((1, 1, MIX_W, tn), lambda i, j, k=k: (lay, k, 0, j))
    goff = MIX_W // tn
    gspec = lambda k: pl.BlockSpec((tm, tn), lambda i, j, k=k: (i, goff + k * (D_MODEL // tn) + j))
    return pl.pallas_call(
        _merge_kernel, grid=(m // tm, D_MODEL // tn),
        in_specs=[yspec, yspec, yspec, wspec(0), wspec(1), wspec(2), gspec(0), gspec(1), gspec(2)],
        out_specs=pl.BlockSpec((tm, tn), lambda i, j: (i, j)),
        out_shape=jax.ShapeDtypeStruct((m, D_MODEL), BF16),
        compiler_params=_params(("parallel", "parallel")),
    )(ya, yb, yc, w_branch, w_branch, w_branch, zs, zs, zs)


_REGROUP = ((_OFF[0], _OFF[1], None), (_OFF[4], _OFF[7], None),
            (_OFF[7], _OFF[8], None), (_OFF[10], _OFF[11], None),
            (_OFF[1], _OFF[2], None),
            (_OFF[2], _OFF[3], None), (_OFF[3], _OFF[4], LANES),
            (_OFF[8], _OFF[9], LANES), (_OFF[9], _OFF[10], LANES))
N_REGROUP = sum(p if p else hi - lo for lo, hi, p in _REGROUP)
assert N_REGROUP == COL_LAT + N_LAT


REGROUP_ROWS = 512


def _regroup_kernel(wt_ref, o_ref):
    col = 0
    for lo, hi, padded in _REGROUP:
        n = hi - lo
        if padded:
            x = jnp.concatenate([wt_ref[0, lo:hi, :], jnp.zeros((padded - n, wt_ref.shape[2]), F32)], axis=0)
            o_ref[0, :, col:col + padded] = x.T.astype(BF16)
            col += padded
        else:
            for a in range(0, n, REGROUP_ROWS):
                o_ref[0, :, col + a:col + a + REGROUP_ROWS] = (
                    wt_ref[0, lo + a:lo + a + REGROUP_ROWS, :].T.astype(BF16))
            col += n


def _regroup_mix_in(w, tc=128):
    wt = jnp.swapaxes(w, 1, 2)
    dep, n, k = wt.shape
    return pl.pallas_call(
        _regroup_kernel, grid=(dep, k // tc),
        in_specs=[pl.BlockSpec((1, n, tc), lambda l, i: (l, 0, i))],
        out_specs=pl.BlockSpec((1, tc, N_REGROUP), lambda l, i: (l, i, 0)),
        out_shape=jax.ShapeDtypeStruct((dep, k, N_REGROUP), BF16),
        compiler_params=_params(("parallel", "parallel")),
    )(wt)


def _prep_mixer_weights(w_mix_in, mla_w_uq, mla_w_ukv):
    w_all = _regroup_mix_in(w_mix_in)
    uq = mla_w_uq.reshape(DEPTH, MLA_Q_RANK, MLA_HEADS, MLA_NOPE + MLA_ROPE)
    w_qn = uq[..., :MLA_NOPE].reshape(DEPTH, MLA_Q_RANK, MLA_HEADS * MLA_NOPE)
    w_qr = jnp.pad(uq[..., MLA_NOPE:], ((0, 0), (0, 0), (0, 0), (0, LANES - MLA_ROPE)))
    w_q = jnp.concatenate([w_qn, w_qr.reshape(DEPTH, MLA_Q_RANK, MLA_HEADS * LANES)], axis=-1).astype(BF16)
    ukv = mla_w_ukv.reshape(DEPTH, MLA_KV_RANK, MLA_HEADS, 2, MLA_NOPE)
    w_kv = ukv.transpose(0, 1, 3, 2, 4).reshape(DEPTH, MLA_KV_RANK, 2 * MLA_HEADS * MLA_NOPE).astype(BF16)
    return w_all, w_q, w_kv


def _mixer(h, lay, positions, rope_tabs, w_all, w_q, w_kv, pool_w, pool_scale, mla_q_norm, mla_kv_norm,
           ml_conv_w, ml_conv_b, ml_gate_b, ml_head_norm, w_branch, tm=1024):
    b, s, d = h.shape
    m = b * s
    h2 = h.reshape(m, d)
    cos, sina, sinb = rope_tabs
    tab = lambda a: (a, (tm, LANES), lambda i, j: (i, 0))
    tile = lambda i, j: (i, j)
    layvec = lambda a: (a.reshape(DEPTH, 1, -1), (1, 1, a.shape[-1]), lambda i, j: (lay, 0, 0))

    zp = _mm(h2, [(w_all, lay, COL_PLAIN)], [], [((m, N_PLAIN), BF16, (tm, 512), tile)],
             _plain_epilogue, tm=tm, tn=512, nj=N_PLAIN // 512)[0]
    zs = _mm(h2, [(w_all, lay, COL_SIG)], [], [((m, N_SIG), BF16, (tm, 512), tile)],
             _sigmoid_epilogue, tm=tm, tn=512, nj=N_SIG // 512)[0]
    cqn = _mm(h2, [(w_all, lay, COL_CQ)], [layvec(mla_q_norm)],
              [((m, MLA_Q_RANK), BF16, (tm, MLA_Q_RANK), tile)],
              _rmsw_epilogue, tm=tm, tn=MLA_Q_RANK, nj=1)[0]
    ckvn, kr, gates = _mm(
        h2, [(w_all, lay, COL_LAT)], [layvec(mla_kv_norm), tab(cos), tab(sina), tab(sinb)],
        [((m, MLA_KV_RANK), BF16, (tm, MLA_KV_RANK), tile),
         ((m, LANES), BF16, (tm, LANES), tile),
         ((m, 2 * LANES), F32, (tm, 2 * LANES), tile)],
        _kvlatent_epilogue, tm=tm, tn=N_LAT, nj=1)

    zp3 = zp.reshape(b, s, N_PLAIN)
    ya = _pool(zp3, pool_w, pool_scale, lay, b, s)

    nq = MLA_HEADS * LANES
    q = _mm(cqn, [(w_q, lay, 0)], [tab(cos), tab(sina), tab(sinb)], [((m, 2 * nq), BF16, (tm, 512), tile)],
            functools.partial(_q_epilogue, heads_per_tile=512 // LANES, n_nope_tiles=nq // 512),
            tm=tm, tn=512, nj=2 * nq // 512)[0]
    kv = _mm(ckvn, [(w_kv, lay, 0)], [], [((m, 2 * nq), BF16, (tm, 512), tile)],
             _plain_epilogue, tm=tm, tn=512, nj=2 * nq // 512)[0]
    yb = _attention(q.reshape(b, s, 2 * nq), kv.reshape(b, s, 2 * nq), kr.reshape(b, s, LANES), positions)

    qk = _conv_silu(zp3, ml_conv_w, ml_conv_b, lay, b, s)
    yc = _mlstm(qk, zp3, gates.reshape(b, s, 2 * LANES), zs.reshape(b, s, N_SIG), ml_gate_b[lay],
                ml_head_norm, lay)

    return _merge(ya.reshape(m, MIX_W), yb.reshape(m, MIX_W), yc.reshape(m, MIX_W), w_branch, lay, zs)


def _q_epilogue(accs, e_refs, o_refs, *, heads_per_tile, n_nope_tiles):
    j = pl.program_id(1)

    @pl.when(j < n_nope_tiles)
    def _():
        _scaled_epilogue(accs, (), o_refs, scale=Q_SCALE_LOG2)

    @pl.when(j >= n_nope_tiles)
    def _():
        _qrope_epilogue(accs, e_refs, o_refs, heads_per_tile=heads_per_tile)


def kernel(x, c, positions, w_ada, b_ada, ada_table, ffn_a_w_in, ffn_a_w_out, w_mix_in, pool_w, pool_scale, mla_q_norm, mla_w_uq, mla_kv_norm, mla_w_ukv, ml_conv_w, ml_conv_b, ml_gate_b, ml_head_norm, w_branch, w_out, ffn_b_w_in, ffn_b_w_out, final_norm):
    b, s, d = x.shape
    m = b * s
    mod = _ada(c, w_ada, b_ada, ada_table)
    rope_tabs = _rope_tables(positions)
    fa_in, fa_out = ffn_a_w_in, ffn_a_w_out.astype(BF16)
    fb_in, fb_out = ffn_b_w_in, ffn_b_w_out.astype(BF16)
    w_all, w_q, w_kv = _prep_mixer_weights(w_mix_in, mla_w_uq, mla_w_ukv)
    pool_wb, w_branchb, w_outb = pool_w.astype(BF16), w_branch.astype(BF16), w_out.astype(BF16)
    for l in range(DEPTH):
        md = mod[l]
        h = _normmod(x, md[:, 0], md[:, 1])
        x = _ffn(x, h, fa_in, fa_out, l, md[:, 2])
        h = _normmod(x, md[:, 3], md[:, 4])
        merged = _mixer(h, l, positions, rope_tabs, w_all, w_q, w_kv, pool_wb, pool_scale, mla_q_norm,
                        mla_kv_norm, ml_conv_w, ml_conv_b, ml_gate_b, ml_head_norm, w_branchb)
        x = _resid_mm(merged, w_outb, l, x.reshape(m, d), md[:, 5], 1.0, s).reshape(b, s, d)
        h = _normmod(x, md[:, 6], md[:, 7])
        x = _ffn(x, h, fb_in, fb_out, l, md[:, 8])
    return _finalnorm(x, final_norm)
```

```python
import functools

import numpy as np
import jax
import jax.numpy as jnp
from jax import lax
from jax.experimental import pallas as pl
from jax.experimental.pallas import tpu as pltpu

F32 = jnp.float32
BF16 = jnp.bfloat16

D_MODEL = 4096
DEPTH = 2
CHUNK = 64
EPS = 1e-6
D_FF = 2 * D_MODEL
MIX_W = D_MODEL // 2
N_BRANCH = 3
N_MOD = 9
POOL_WINDOWS = (2, 4, 8, 16)
POOL_GW = MIX_W // len(POOL_WINDOWS)
MLA_NOPE = 128
MLA_ROPE = 64
MLA_V = 128
MLA_HEADS = MIX_W // MLA_V
MLA_Q_RANK = D_MODEL // 4
MLA_KV_RANK = 512
MLA_SCALE = (MLA_NOPE + MLA_ROPE) ** -0.5
Q_SCALE_LOG2 = MLA_SCALE * float(np.log2(np.e))
ROPE_THETA = 10000.0
ML_HEADS = 8
ML_DK = 128
ML_DV = MIX_W // ML_HEADS
CONV_W = 4

_SPLITS = (MIX_W, MLA_Q_RANK, MLA_KV_RANK, MLA_ROPE, ML_HEADS * ML_DK, ML_HEADS * ML_DK,
           ML_HEADS * ML_DV, ML_HEADS * ML_DV, ML_HEADS, ML_HEADS, N_BRANCH * D_MODEL)
_OFF = tuple(int(v) for v in np.cumsum((0,) + _SPLITS))

LANES = 128
HALO = 16
VMEM_LIMIT = 60 * 1024 * 1024

N_PLAIN = 3 * MIX_W
N_SIG = MIX_W + N_BRANCH * D_MODEL
N_LAT = MLA_KV_RANK + 3 * LANES
COL_PLAIN = 0
COL_SIG = COL_PLAIN + N_PLAIN
COL_CQ = COL_SIG + N_SIG
COL_LAT = COL_CQ + MLA_Q_RANK


def _params(sem):
    return pltpu.CompilerParams(dimension_semantics=sem, vmem_limit_bytes=VMEM_LIMIT)


def _mm_body(*refs, nw, ne, no, nk, epilogue):
    x_ref = refs[0]
    w_refs = refs[1:1 + nw]
    e_refs = refs[1 + nw:1 + nw + ne]
    o_refs = refs[1 + nw + ne:1 + nw + ne + no]
    acc_refs = refs[1 + nw + ne + no:]
    if nk == 1:
        accs = [jnp.dot(x_ref[...], w[0], preferred_element_type=F32) for w in w_refs]
        epilogue(accs, e_refs, o_refs)
        return
    k = pl.program_id(2)

    @pl.when(k == 0)
    def _():
        for a, w in zip(acc_refs, w_refs):
            a[...] = jnp.dot(x_ref[...], w[0], preferred_element_type=F32)

    @pl.when(k > 0)
    def _():
        for a, w in zip(acc_refs, w_refs):
            a[...] += jnp.dot(x_ref[...], w[0], preferred_element_type=F32)

    @pl.when(k == nk - 1)
    def _():
        epilogue([a[...] for a in acc_refs], e_refs, o_refs)


def _mm(x, ws, extras, outs, epilogue, *, tm, tn, nj, nk=1):
    m, kdim = x.shape
    tk = kdim // nk
    in_specs = [pl.BlockSpec((tm, tk), lambda i, j, k: (i, k))]
    for _, lay, col in ws:
        assert col % tn == 0
        in_specs.append(pl.BlockSpec((1, tk, tn), lambda i, j, k, lay=lay, off=col // tn: (lay, k, off + j)))
    for _, blk, f in extras:
        in_specs.append(pl.BlockSpec(blk, lambda i, j, k, f=f: f(i, j)))
    out_specs = [pl.BlockSpec(blk, lambda i, j, k, f=f: f(i, j)) for _, _, blk, f in outs]
    out_shape = [jax.ShapeDtypeStruct(s, d) for s, d, _, _ in outs]
    scratch = [pltpu.VMEM((tm, tn), F32) for _ in ws] if nk > 1 else []
    body = functools.partial(_mm_body, nw=len(ws), ne=len(extras), no=len(outs), nk=nk,
                             epilogue=epilogue)
    return pl.pallas_call(
        body, grid=(m // tm, nj, nk), in_specs=in_specs, out_specs=out_specs,
        out_shape=out_shape, scratch_shapes=scratch,
        compiler_params=_params(("parallel", "parallel", "arbitrary")),
    )(x, *[w for w, _, _ in ws], *[a for a, _, _ in extras])


def _rope128(v, cos, sina, sinb):
    return (v * cos + pltpu.roll(v, LANES - MLA_ROPE // 2, 1) * sina
            + pltpu.roll(v, MLA_ROPE // 2, 1) * sinb)


def _ada_kernel(c_ref, w_ref, b_ref, t_ref, o_ref):
    c = c_ref[...]
    s = c * jax.nn.sigmoid(c)
    acc = jnp.dot(s.astype(BF16), w_ref[...].astype(BF16), preferred_element_type=F32) + b_ref[...]
    for l in range(DEPTH):
        o_ref[l] = acc + t_ref[l]


def _ada(c, w_ada, b_ada, ada_table):
    b = c.shape[0]
    rows = 8
    c8 = jnp.zeros((rows, D_MODEL), F32).at[:b].set(c)
    n = N_MOD * D_MODEL
    tn = 1024
    out = pl.pallas_call(
        _ada_kernel, grid=(n // tn,),
        in_specs=[pl.BlockSpec((rows, D_MODEL), lambda j: (0, 0)),
                  pl.BlockSpec((D_MODEL, tn), lambda j: (0, j)),
                  pl.BlockSpec((1, tn), lambda j: (0, j)),
                  pl.BlockSpec((DEPTH, 1, tn), lambda j: (0, 0, j))],
        out_specs=pl.BlockSpec((DEPTH, rows, tn), lambda j: (0, 0, j)),
        out_shape=jax.ShapeDtypeStruct((DEPTH, rows, n), F32),
        compiler_params=_params(("parallel",)),
    )(c8, w_ada, b_ada.reshape(1, n), ada_table.reshape(DEPTH, 1, n))
    return out[:, :b].reshape(DEPTH, b, N_MOD, D_MODEL)


def _normmod_kernel(x_ref, shift_ref, scale_ref, o_ref):
    x = x_ref[0]
    y = x * lax.rsqrt(jnp.mean(x * x, axis=-1, keepdims=True) + EPS)
    o_ref[0] = (y * (1.0 + scale_ref[0]) + shift_ref[0]).astype(o_ref.dtype)


def _finalnorm_kernel(x_ref, w_ref, o_ref):
    x = x_ref[0]
    y = x * lax.rsqrt(jnp.mean(x * x, axis=-1, keepdims=True) + EPS)
    o_ref[0] = y * w_ref[...]


def _normmod(x, shift, scale, ts=512):
    b, s, d = x.shape
    vec = pl.BlockSpec((1, 1, d), lambda bi, i: (bi, 0, 0))
    return pl.pallas_call(
        _normmod_kernel, grid=(b, s // ts),
        in_specs=[pl.BlockSpec((1, ts, d), lambda bi, i: (bi, i, 0)), vec, vec],
        out_specs=pl.BlockSpec((1, ts, d), lambda bi, i: (bi, i, 0)),
        out_shape=jax.ShapeDtypeStruct((b, s, d), BF16),
        compiler_params=_params(("parallel", "parallel")),
    )(x, shift.reshape(b, 1, d), scale.reshape(b, 1, d))


def _finalnorm(x, w, ts=512):
    b, s, d = x.shape
    return pl.pallas_call(
        _finalnorm_kernel, grid=(b, s // ts),
        in_specs=[pl.BlockSpec((1, ts, d), lambda bi, i: (bi, i, 0)),
                  pl.BlockSpec((1, d), lambda bi, i: (0, 0))],
        out_specs=pl.BlockSpec((1, ts, d), lambda bi, i: (bi, i, 0)),
        out_shape=jax.ShapeDtypeStruct((b, s, d), F32),
        compiler_params=_params(("parallel", "parallel")),
    )(x, w.reshape(1, d))


def _resid_epilogue(accs, e_refs, o_refs, *, coef):
    x_ref, g_ref = e_refs
    o_refs[0][...] = x_ref[...] + (coef * g_ref[0]) * accs[0]


def _resid_mm(a, w, lay, x2d, gate, coef, seq, *, tm=1024, tn=512, nk=1):
    m, n = x2d.shape
    per_b = seq // tm
    return _mm(a, [(w, lay, 0)],
               [(x2d, (tm, tn), lambda i, j: (i, j)),
                (gate.reshape(-1, 1, n), (1, 1, tn), lambda i, j: (i // per_b, 0, j))],
               [((m, n), F32, (tm, tn), lambda i, j: (i, j))],
               functools.partial(_resid_epilogue, coef=coef), tm=tm, tn=tn, nj=n // tn, nk=nk)[0]


def _swiglu_kernel(x_ref, w_hbm, o_ref, stage, wbf, sems, *, lay, tn, nj):
    j = pl.program_id(0)
    i = pl.program_id(1)

    def tile_copy(jj, half):
        src = w_hbm.at[lay, :, pl.ds(pl.multiple_of((half * nj + jj) * tn, tn), tn)]
        return pltpu.make_async_copy(src, stage.at[half], sems.at[half])

    @pl.when(i == 0)
    def _():
        @pl.when(j == 0)
        def _():
            tile_copy(j, 0).start()
            tile_copy(j, 1).start()

        for half in range(2):
            tile_copy(j, half).wait()
            wbf[half] = stage[half].astype(BF16)

        @pl.when(j + 1 < nj)
        def _():
            tile_copy(j + 1, 0).start()
            tile_copy(j + 1, 1).start()

    x = x_ref[...]
    g = jnp.dot(x, wbf[0], preferred_element_type=F32)
    u = jnp.dot(x, wbf[1], preferred_element_type=F32)
    o_ref[...] = (g * jax.nn.sigmoid(g) * u).astype(BF16)


def _swiglu_mm(h2, w_in, lay, *, tm=1024, tn=512):
    m, d = h2.shape
    nj = D_FF // tn
    return pl.pallas_call(
        functools.partial(_swiglu_kernel, lay=lay, tn=tn, nj=nj), grid=(nj, m // tm),
        in_specs=[pl.BlockSpec((tm, d), lambda j, i: (i, 0)),
                  pl.BlockSpec(memory_space=pl.ANY)],
        out_specs=pl.BlockSpec((tm, tn), lambda j, i: (i, j)),
        out_shape=jax.ShapeDtypeStruct((m, D_FF), BF16),
        scratch_shapes=[pltpu.VMEM((2, d, tn), F32), pltpu.VMEM((2, d, tn), BF16),
                        pltpu.SemaphoreType.DMA((2,))],
        compiler_params=_params(("arbitrary", "arbitrary")),
    )(h2, w_in)


def _ffn(x, h, w_in, w_out, lay, gate):
    b, s, d = x.shape
    m = b * s
    a = _swiglu_mm(h.reshape(m, d), w_in, lay)
    y = _resid_mm(a, w_out, lay, x.reshape(m, d), gate, 0.5, s, tm=512, tn=512)
    return y.reshape(b, s, d)


def _rope_tab_kernel(p_ref, inv_ref, cos_ref, sina_ref, sinb_ref):
    ang = p_ref[...].astype(F32) * inv_ref[...]
    lane = lax.broadcasted_iota(jnp.int32, ang.shape, 1)
    half = MLA_ROPE // 2
    c = jnp.cos(ang)
    s = jnp.sin(ang)
    cos_ref[...] = jnp.where(lane < MLA_ROPE, c, 0.0)
    sina_ref[...] = jnp.where(lane < half, -s, 0.0)
    sinb_ref[...] = jnp.where(lane >= half, jnp.where(lane < MLA_ROPE, s, 0.0), 0.0)


def _rope_tables(positions, ts=512):
    m = positions.size
    half = MLA_ROPE // 2
    inv = ROPE_THETA ** (-jnp.arange(0, MLA_ROPE, 2, dtype=F32) / MLA_ROPE)
    inv128 = jnp.concatenate([inv, inv, jnp.zeros((LANES - 2 * half,), F32)]).reshape(1, LANES)
    spec = pl.BlockSpec((ts, LANES), lambda i: (i, 0))
    shp = jax.ShapeDtypeStruct((m, LANES), F32)
    return pl.pallas_call(
        _rope_tab_kernel, grid=(m // ts,),
        in_specs=[pl.BlockSpec((ts, 1), lambda i: (i, 0)), pl.BlockSpec((1, LANES), lambda i: (0, 0))],
        out_specs=[spec, spec, spec], out_shape=[shp, shp, shp],
        compiler_params=_params(("parallel",)),
    )(positions.reshape(m, 1), inv128)


def _plain_epilogue(accs, e_refs, o_refs):
    o_refs[0][...] = accs[0].astype(o_refs[0].dtype)


def _sigmoid_epilogue(accs, e_refs, o_refs):
    o_refs[0][...] = jax.nn.sigmoid(accs[0]).astype(o_refs[0].dtype)


def _rmsw_epilogue(accs, e_refs, o_refs):
    a = accs[0]
    y = a * lax.rsqrt(jnp.mean(a * a, axis=-1, keepdims=True) + EPS) * e_refs[0][0]
    o_refs[0][...] = y.astype(o_refs[0].dtype)


def _kvlatent_epilogue(accs, e_refs, o_refs):
    a = accs[0]
    w_ref, cos_ref, sina_ref, sinb_ref = e_refs
    ckv = a[:, :MLA_KV_RANK]
    y = ckv * lax.rsqrt(jnp.mean(ckv * ckv, axis=-1, keepdims=True) + EPS) * w_ref[0]
    o_refs[0][...] = y.astype(BF16)
    kr = a[:, MLA_KV_RANK:MLA_KV_RANK + LANES]
    o_refs[1][...] = _rope128(kr, cos_ref[...], sina_ref[...], sinb_ref[...]).astype(BF16)
    o_refs[2][...] = a[:, MLA_KV_RANK + LANES:]


def _qrope_epilogue(accs, e_refs, o_refs, *, heads_per_tile):
    a = accs[0] * Q_SCALE_LOG2
    cos_ref, sina_ref, sinb_ref = e_refs
    cos, sina, sinb = cos_ref[...], sina_ref[...], sinb_ref[...]
    for c in range(heads_per_tile):
        sl = slice(c * LANES, (c + 1) * LANES)
        o_refs[0][:, sl] = _rope128(a[:, sl], cos, sina, sinb).astype(BF16)


def _scaled_epilogue(accs, e_refs, o_refs, *, scale):
    o_refs[0][...] = (accs[0] * scale).astype(o_refs[0].dtype)


def _band(ts, lo, hi, first_tile):
    t = lax.broadcasted_iota(jnp.int32, (ts, HALO + ts), 0)
    s = lax.broadcasted_iota(jnp.int32, (ts, HALO + ts), 1)
    d = t + HALO - s
    ok = jnp.where(d >= lo, jnp.where(d < hi, 1.0, 0.0), 0.0)
    ok = jnp.where(s < HALO, jnp.where(first_tile, 0.0, ok), ok)
    return ok.astype(BF16)


def _pool_kernel(u_ref, halo_ref, pw_ref, ps_ref, o_ref, *, ts):
    i = pl.program_id(1)
    u = u_ref[0]
    ucat = jnp.concatenate([halo_ref[0], u], axis=0)
    tg = i * ts + lax.broadcasted_iota(jnp.int32, (ts, 1), 0)
    for g, w in enumerate(POOL_WINDOWS):
        sl = slice(g * POOL_GW, (g + 1) * POOL_GW)
        win = jnp.dot(_band(ts, 0, w, i == 0), ucat[:, sl], preferred_element_type=F32)
        cnt = jnp.minimum(tg + 1, w).astype(F32)
        p = win / cnt - u[:, sl].astype(F32)
        y = jnp.dot(p.astype(BF16), pw_ref[0, g], preferred_element_type=F32)
        o_ref[0, :, sl] = (y * ps_ref[0, :, sl]).astype(BF16)


def _pool(zp, pool_w, pool_scale, lay, b, s, ts=256):
    hb = ts // HALO
    ng = len(POOL_WINDOWS)
    return pl.pallas_call(
        functools.partial(_pool_kernel, ts=ts), grid=(b, s // ts),
        in_specs=[pl.BlockSpec((1, ts, MIX_W), lambda bi, i: (bi, i, 0)),
                  pl.BlockSpec((1, HALO, MIX_W), lambda bi, i: (bi, jnp.maximum(i * hb - 1, 0), 0)),
                  pl.BlockSpec((1, ng, POOL_GW, POOL_GW), lambda bi, i: (lay, 0, 0, 0)),
                  pl.BlockSpec((1, 1, MIX_W), lambda bi, i: (lay, 0, 0))],
        out_specs=pl.BlockSpec((1, ts, MIX_W), lambda bi, i: (bi, i, 0)),
        out_shape=jax.ShapeDtypeStruct((b, s, MIX_W), BF16),
        compiler_params=_params(("parallel", "parallel")),
    )(zp, zp, pool_w, pool_scale.reshape(DEPTH, 1, MIX_W))


def _conv_kernel(x_ref, halo_ref, w_ref, b_ref, sc_ref, o_ref, *, ts):
    i = pl.program_id(1)
    x = x_ref[0]
    xcat = jnp.concatenate([halo_ref[0], x], axis=0)
    w = w_ref[0]
    acc = x.astype(F32) * w[CONV_W - 1:CONV_W, :] + b_ref[0]
    for d in range(1, CONV_W):
        xs = jnp.dot(_band(ts, d, d + 1, i == 0), xcat, preferred_element_type=F32)
        acc = acc + xs * w[CONV_W - 1 - d:CONV_W - d, :]
    o_ref[0] = (acc * jax.nn.sigmoid(acc) * sc_ref[...]).astype(BF16)


def _conv_silu(zp, conv_w, conv_b, lay, b, s, ts=256):
    c = 2 * ML_HEADS * ML_DK
    hb = ts // HALO
    post = jnp.concatenate([jnp.full((c // 2,), ML_DK ** -0.5, F32), jnp.ones((c // 2,), F32)])
    return pl.pallas_call(
        functools.partial(_conv_kernel, ts=ts), grid=(b, s // ts),
        in_specs=[pl.BlockSpec((1, ts, c), lambda bi, i: (bi, i, 1)),
                  pl.BlockSpec((1, HALO, c), lambda bi, i: (bi, jnp.maximum(i * hb - 1, 0), 1)),
                  pl.BlockSpec((1, CONV_W, c), lambda bi, i: (lay, 0, 0)),
                  pl.BlockSpec((1, 1, c), lambda bi, i: (lay, 0, 0)),
                  pl.BlockSpec((1, c), lambda bi, i: (0, 0))],
        out_specs=pl.BlockSpec((1, ts, c), lambda bi, i: (bi, i, 0)),
        out_shape=jax.ShapeDtypeStruct((b, s, c), BF16),
        compiler_params=_params(("parallel", "parallel")),
    )(zp, zp, conv_w, conv_b.reshape(DEPTH, 1, c), post.reshape(1, c))


ATTN_HEADS_PER_STEP = 8


def _attn_kernel(qn_ref, qr_ref, kn_ref, kr_ref, v_ref, pq_ref, pk_ref, o_ref, m_sc, l_sc, acc_sc, *, tq):
    i = pl.program_id(2)
    nt = (((1,), (1,)), ((), ()))
    tn = (((0,), (0,)), ((), ()))
    hs = [slice(g * LANES, (g + 1) * LANES) for g in range(ATTN_HEADS_PER_STEP)]

    def block(start, mask, first):
        kr = kr_ref[0, pl.ds(start, tq), :]

        def scores(g):
            sl = hs[g]
            q = jnp.concatenate([qn_ref[0, :, sl], qr_ref[0, :, sl]], axis=-1)
            k = jnp.concatenate([kn_ref[0, pl.ds(start, tq), sl], kr], axis=-1)
            st = lax.dot_general(k, q, nt, preferred_element_type=F32)
            if mask is not None:
                st = jnp.where(mask, st, -jnp.inf)
            return st

        st_next = scores(0)
        for g, sl in enumerate(hs):
            st = st_next
            if g + 1 < len(hs):
                st_next = scores(g + 1)
            v = v_ref[0, pl.ds(start, tq), sl]
            smax = jnp.max(st, axis=0, keepdims=True)
            if first:
                m_new = smax
                p = jnp.exp2(st - m_new)
                l_sc[g] = jnp.sum(p, axis=0, keepdims=True)
                acc_sc[g] = lax.dot_general(v, p.astype(BF16), tn, preferred_element_type=F32)
            else:
                m_old = m_sc[g]
                m_new = jnp.maximum(m_old, smax)
                alpha = jnp.exp2(m_old - m_new)
                p = jnp.exp2(st - m_new)
                l_sc[g] = alpha * l_sc[g] + jnp.sum(p, axis=0, keepdims=True)
                acc_sc[g] = alpha * acc_sc[g] + lax.dot_general(v, p.astype(BF16), tn,
                                                                preferred_element_type=F32)
            m_sc[g] = m_new

    sh = CHUNK.bit_length() - 1
    mask = lax.shift_right_arithmetic(pk_ref[0], sh) <= lax.shift_right_arithmetic(pq_ref[0], sh)
    block(pl.multiple_of(i * tq, tq), mask, True)

    def body(j, carry):
        block(pl.multiple_of(j * tq, tq), None, False)
        return carry

    lax.fori_loop(0, i, body, 0)
    for g, sl in enumerate(hs):
        o_ref[0, :, sl] = (acc_sc[g] / l_sc[g]).T.astype(o_ref.dtype)


def _attention(q, kv, kr, positions, tq=512):
    b, s, _ = q.shape
    G = ATTN_HEADS_PER_STEP
    w = G * LANES
    ng = MLA_HEADS // G
    return pl.pallas_call(
        functools.partial(_attn_kernel, tq=tq), grid=(b, ng, s // tq),
        in_specs=[pl.BlockSpec((1, tq, w), lambda bi, hi, i: (bi, i, hi)),
                  pl.BlockSpec((1, tq, w), lambda bi, hi, i: (bi, i, ng + hi)),
                  pl.BlockSpec((1, s, w), lambda bi, hi, i: (bi, 0, hi)),
                  pl.BlockSpec((1, s, LANES), lambda bi, hi, i: (bi, 0, 0)),
                  pl.BlockSpec((1, s, w), lambda bi, hi, i: (bi, 0, ng + hi)),
                  pl.BlockSpec((1, 1, tq), lambda bi, hi, i: (bi, 0, i)),
                  pl.BlockSpec((1, tq, 1), lambda bi, hi, i: (bi, i, 0))],
        out_specs=pl.BlockSpec((1, tq, w), lambda bi, hi, i: (bi, i, hi)),
        out_shape=jax.ShapeDtypeStruct((b, s, MLA_HEADS * MLA_V), BF16),
        scratch_shapes=[pltpu.VMEM((G, 1, tq), F32), pltpu.VMEM((G, 1, tq), F32),
                        pltpu.VMEM((G, MLA_V, tq), F32)],
        compiler_params=_params(("parallel", "parallel", "arbitrary")),
    )(q, q, kv, kr, kv, positions.reshape(b, 1, s), positions.reshape(b, s, 1))


def _split3_dot(a_bf16, x):
    hi = x.astype(BF16)
    r1 = x - hi.astype(F32)
    mid = r1.astype(BF16)
    lo = (r1 - mid.astype(F32)).astype(BF16)
    return (jnp.dot(a_bf16, hi, preferred_element_type=F32)
            + jnp.dot(a_bf16, mid, preferred_element_type=F32)
            + jnp.dot(a_bf16, lo, preferred_element_type=F32))


def _mlstm_kernel(qk_ref, v_ref, g_ref, og_ref, gb_ref, hn_ref, o_ref, ct_ref, m_ref):
    cidx = pl.program_id(1)

    @pl.when(cidx == 0)
    def _():
        ct_ref[...] = jnp.zeros_like(ct_ref)
        m_ref[...] = jnp.zeros_like(m_ref)

    L = CHUNK
    g = g_ref[0]
    gb = gb_ref[...]
    logi = g[:, :LANES] + gb[:, :LANES]
    xf = g[:, LANES:] + gb[:, LANES:]
    logf = jnp.minimum(xf, 0.0) - jnp.log1p(jnp.exp(-jnp.abs(xf)))
    row = lax.broadcasted_iota(jnp.int32, (L, L), 0)
    col = lax.broadcasted_iota(jnp.int32, (L, L), 1)
    tril = row >= col
    bcum = _split3_dot(jnp.where(tril, 1.0, 0.0).astype(BF16), logf)
    x = logi - bcum
    xt = x.T
    m_row = m_ref[...]
    b_last = bcum[L - 1:L, :]
    m_new = jnp.maximum(b_last + m_row, b_last + jnp.max(x, axis=0, keepdims=True))
    decay = jnp.exp(b_last + m_row - m_new)
    ws_all = jnp.exp(b_last + x - m_new)
    inter_all = bcum + m_row
    ones_col = jnp.where(lax.broadcasted_iota(jnp.int32, (L, LANES), 1) == 0, 1.0, 0.0).astype(BF16)
    nt = (((1,), (1,)), ((), ()))
    tn = (((0,), (0,)), ((), ()))
    def early(h):
        q = qk_ref[0, :, h * ML_DK:(h + 1) * ML_DK]
        k = qk_ref[0, :, (ML_HEADS + h) * ML_DK:(ML_HEADS + h + 1) * ML_DK]
        ct = ct_ref[h]
        qk = lax.dot_general(q, k, nt, preferred_element_type=F32)
        qc = jnp.dot(q, ct.astype(BF16), preferred_element_type=F32)
        dlog = jnp.where(tril, bcum[:, h:h + 1] + xt[h:h + 1, :], -jnp.inf)
        inter = inter_all[:, h:h + 1]
        mt = jnp.maximum(inter, jnp.max(dlog, axis=-1, keepdims=True))
        return k, ct, qk, qc, mt, jnp.exp(inter - mt), jnp.exp(dlog - mt)

    ahead = early(0)
    for h in range(ML_HEADS):
        k, ct, qk, qc, mt, w_inter, e = ahead
        if h + 1 < ML_HEADS:
            ahead = early(h + 1)
        vaug = jnp.concatenate([v_ref[0, :, h * ML_DV:(h + 1) * ML_DV], ones_col], axis=-1)
        num = jnp.dot((qk * e).astype(BF16), vaug, preferred_element_type=F32) + w_inter * qc
        den = num[:, ML_DV:ML_DV + 1]
        hh = num[:, :ML_DV] / jnp.maximum(jnp.abs(den), jnp.exp(-mt))
        wv = (ws_all[:, h:h + 1] * vaug.astype(F32)).astype(BF16)
        ct_ref[h] = decay[:, h:h + 1] * ct + lax.dot_general(k, wv, tn, preferred_element_type=F32)
        hs = slice(h * ML_DV, (h + 1) * ML_DV)
        hn = hh * lax.rsqrt(jnp.mean(hh * hh, axis=-1, keepdims=True) + EPS) * hn_ref[0, :, hs]
        o_ref[0, :, hs] = (og_ref[0, :, hs].astype(F32) * hn).astype(BF16)
    m_ref[...] = m_new


def _mlstm(qk, zp, gates, zs, gate_b, head_norm, lay):
    b, s, _ = qk.shape
    gb = jnp.zeros((1, 2 * LANES), F32)
    gb = gb.at[0, :ML_HEADS].set(gate_b[:ML_HEADS]).at[0, LANES:LANES + ML_HEADS].set(gate_b[ML_HEADS:])
    blk = lambda col: pl.BlockSpec((1, CHUNK, MIX_W), lambda bi, c, col=col: (bi, c, col))
    return pl.pallas_call(
        _mlstm_kernel, grid=(b, s // CHUNK),
        in_specs=[blk(0), blk(2),
                  pl.BlockSpec((1, CHUNK, 2 * LANES), lambda bi, c: (bi, c, 0)),
                  blk(0),
                  pl.BlockSpec((1, 2 * LANES), lambda bi, c: (0, 0)),
                  pl.BlockSpec((1, 1, MIX_W), lambda bi, c: (lay, 0, 0))],
        out_specs=blk(0),
        out_shape=jax.ShapeDtypeStruct((b, s, MIX_W), BF16),
        scratch_shapes=[pltpu.VMEM((ML_HEADS, ML_DK, ML_DV + LANES), F32), pltpu.VMEM((1, LANES), F32)],
        compiler_params=_params(("parallel", "arbitrary")),
    )(qk, zp, gates, zs, gb, head_norm.reshape(DEPTH, 1, MIX_W))


def _merge_kernel(ya_ref, yb_ref, yc_ref, wa_ref, wb_ref, wc_ref, ga_ref, gb_ref, gc_ref, o_ref):
    acc = ga_ref[...].astype(F32) * jnp.dot(ya_ref[...], wa_ref[0, 0], preferred_element_type=F32)
    acc += gb_ref[...].astype(F32) * jnp.dot(yb_ref[...], wb_ref[0, 0], preferred_element_type=F32)
    acc += gc_ref[...].astype(F32) * jnp.dot(yc_ref[...], wc_ref[0, 0], preferred_element_type=F32)
    o_ref[...] = acc.astype(BF16)


def _merge(ya, yb, yc, w_branch, lay, zs, tm=1024, tn=512):
    m = ya.shape[0]
    yspec = pl.BlockSpec((tm, MIX_W), lambda i, j: (i, 0))
    wspec = lambda k: pl.BlockSpec((1, 1, MIX_W, tn), lambda i, j, k=k: (lay, k, 0, j))
    goff = MIX_W // tn
    gspec = lambda k: pl.BlockSpec((tm, tn), lambda i, j, k=k: (i, goff + k * (D_MODEL // tn) + j))
    return pl.pallas_call(
        _merge_kernel, grid=(m // tm, D_MODEL // tn),
        in_specs=[yspec, yspec, yspec, wspec(0), wspec(1), wspec(2), gspec(0), gspec(1), gspec(2)],
        out_specs=pl.BlockSpec((tm, tn), lambda i, j: (i, j)),
        out_shape=jax.ShapeDtypeStruct((m, D_MODEL), BF16),
        compiler_params=_params(("parallel", "parallel")),
    )(ya, yb, yc, w_branch, w_branch, w_branch, zs, zs, zs)


_REGROUP = ((_OFF[0], _OFF[1], None), (_OFF[4], _OFF[7], None),
            (_OFF[7], _OFF[8], None), (_OFF[10], _OFF[11], None),
            (_OFF[1], _OFF[2], None),
            (_OFF[2], _OFF[3], None), (_OFF[3], _OFF[4], LANES),
            (_OFF[8], _OFF[9], LANES), (_OFF[9], _OFF[10], LANES))
N_REGROUP = sum(p if p else hi - lo for lo, hi, p in _REGROUP)
assert N_REGROUP == COL_LAT + N_LAT


REGROUP_ROWS = 512


def _regroup_kernel(wt_ref, o_ref):
    col = 0
    for lo, hi, padded in _REGROUP:
        n = hi - lo
        if padded:
            x = jnp.concatenate([wt_ref[0, lo:hi, :], jnp.zeros((padded - n, wt_ref.shape[2]), F32)], axis=0)
            o_ref[0, :, col:col + padded] = x.T.astype(BF16)
            col += padded
        else:
            for a in range(0, n, REGROUP_ROWS):
                o_ref[0, :, col + a:col + a + REGROUP_ROWS] = (
                    wt_ref[0, lo + a:lo + a + REGROUP_ROWS, :].T.astype(BF16))
            col += n


def _regroup_mix_in(w, tc=128):
    wt = jnp.swapaxes(w, 1, 2)
    dep, n, k = wt.shape
    return pl.pallas_call(
        _regroup_kernel, grid=(dep, k // tc),
        in_specs=[pl.BlockSpec((1, n, tc), lambda l, i: (l, 0, i))],
        out_specs=pl.BlockSpec((1, tc, N_REGROUP), lambda l, i: (l, i, 0)),
        out_shape=jax.ShapeDtypeStruct((dep, k, N_REGROUP), BF16),
        compiler_params=_params(("parallel", "parallel")),
    )(wt)


def _prep_mixer_weights(w_mix_in, mla_w_uq, mla_w_ukv):
    w_all = _regroup_mix_in(w_mix_in)
    uq = mla_w_uq.reshape(DEPTH, MLA_Q_RANK, MLA_HEADS, MLA_NOPE + MLA_ROPE)
    w_qn = uq[..., :MLA_NOPE].reshape(DEPTH, MLA_Q_RANK, MLA_HEADS * MLA_NOPE)
    w_qr = jnp.pad(uq[..., MLA_NOPE:], ((0, 0), (0, 0), (0, 0), (0, LANES - MLA_ROPE)))
    w_q = jnp.concatenate([w_qn, w_qr.reshape(DEPTH, MLA_Q_RANK, MLA_HEADS * LANES)], axis=-1).astype(BF16)
    ukv = mla_w_ukv.reshape(DEPTH, MLA_KV_RANK, MLA_HEADS, 2, MLA_NOPE)
    w_kv = ukv.transpose(0, 1, 3, 2, 4).reshape(DEPTH, MLA_KV_RANK, 2 * MLA_HEADS * MLA_NOPE).astype(BF16)
    return w_all, w_q, w_kv


def _mixer(h, lay, positions, rope_tabs, w_all, w_q, w_kv, pool_w, pool_scale, mla_q_norm, mla_kv_norm,
           ml_conv_w, ml_conv_b, ml_gate_b, ml_head_norm, w_branch, tm=1024):
    b, s, d = h.shape
    m = b * s
    h2 = h.reshape(m, d)
    cos, sina, sinb = rope_tabs
    tab = lambda a: (a, (tm, LANES), lambda i, j: (i, 0))
    tile = lambda i, j: (i, j)
    layvec = lambda a: (a.reshape(DEPTH, 1, -1), (1, 1, a.shape[-1]), lambda i, j: (lay, 0, 0))

    wide = 1024
    zp = _mm(h2, [(w_all, lay, COL_PLAIN)], [], [((m, N_PLAIN), BF16, (tm, wide), tile)],
             _plain_epilogue, tm=tm, tn=wide, nj=N_PLAIN // wide)[0]
    zs = _mm(h2, [(w_all, lay, COL_SIG)], [], [((m, N_SIG), BF16, (tm, wide), tile)],
             _sigmoid_epilogue, tm=tm, tn=wide, nj=N_SIG // wide)[0]
    cqn = _mm(h2, [(w_all, lay, COL_CQ)], [layvec(mla_q_norm)],
              [((m, MLA_Q_RANK), BF16, (tm, MLA_Q_RANK), tile)],
              _rmsw_epilogue, tm=tm, tn=MLA_Q_RANK, nj=1)[0]
    ckvn, kr, gates = _mm(
        h2, [(w_all, lay, COL_LAT)], [layvec(mla_kv_norm), tab(cos), tab(sina), tab(sinb)],
        [((m, MLA_KV_RANK), BF16, (tm, MLA_KV_RANK), tile),
         ((m, LANES), BF16, (tm, LANES), tile),
         ((m, 2 * LANES), F32, (tm, 2 * LANES), tile)],
        _kvlatent_epilogue, tm=tm, tn=N_LAT, nj=1)

    zp3 = zp.reshape(b, s, N_PLAIN)
    ya = _pool(zp3, pool_w, pool_scale, lay, b, s)

    nq = MLA_HEADS * LANES
    q = _mm(cqn, [(w_q, lay, 0)], [tab(cos), tab(sina), tab(sinb)], [((m, 2 * nq), BF16, (tm, nq), tile)],
            functools.partial(_q_epilogue, heads_per_tile=MLA_HEADS, n_nope_tiles=1),
            tm=tm, tn=nq, nj=2)[0]
    kv = _mm(ckvn, [(w_kv, lay, 0)], [], [((m, 2 * nq), BF16, (tm, nq), tile)],
             _plain_epilogue, tm=tm, tn=nq, nj=2)[0]
    yb = _attention(q.reshape(b, s, 2 * nq), kv.reshape(b, s, 2 * nq), kr.reshape(b, s, LANES), positions)

    qk = _conv_silu(zp3, ml_conv_w, ml_conv_b, lay, b, s)
    yc = _mlstm(qk, zp3, gates.reshape(b, s, 2 * LANES), zs.reshape(b, s, N_SIG), ml_gate_b[lay],
                ml_head_norm, lay)

    return _merge(ya.reshape(m, MIX_W), yb.reshape(m, MIX_W), yc.reshape(m, MIX_W), w_branch, lay, zs)


def _q_epilogue(accs, e_refs, o_refs, *, heads_per_tile, n_nope_tiles):
    j = pl.program_id(1)

    @pl.when(j < n_nope_tiles)
    def _():
        _scaled_epilogue(accs, (), o_refs, scale=Q_SCALE_LOG2)

    @pl.when(j >= n_nope_tiles)
    def _():
        _qrope_epilogue(accs, e_refs, o_refs, heads_per_tile=heads_per_tile)


def kernel(x, c, positions, w_ada, b_ada, ada_table, ffn_a_w_in, ffn_a_w_out, w_mix_in, pool_w, pool_scale, mla_q_norm, mla_w_uq, mla_kv_norm, mla_w_ukv, ml_conv_w, ml_conv_b, ml_gate_b, ml_head_norm, w_branch, w_out, ffn_b_w_in, ffn_b_w_out, final_norm):
    b, s, d = x.shape
    m = b * s
    mod = _ada(c, w_ada, b_ada, ada_table)
    rope_tabs = _rope_tables(positions)
    fa_in, fa_out = ffn_a_w_in, ffn_a_w_out.astype(BF16)
    fb_in, fb_out = ffn_b_w_in, ffn_b_w_out.astype(BF16)
    w_all, w_q, w_kv = _prep_mixer_weights(w_mix_in, mla_w_uq, mla_w_ukv)
    pool_wb, w_branchb, w_outb = pool_w.astype(BF16), w_branch.astype(BF16), w_out.astype(BF16)
    for l in range(DEPTH):
        md = mod[l]
        h = _normmod(x, md[:, 0], md[:, 1])
        x = _ffn(x, h, fa_in, fa_out, l, md[:, 2])
        h = _normmod(x, md[:, 3], md[:, 4])
        merged = _mixer(h, l, positions, rope_tabs, w_all, w_q, w_kv, pool_wb, pool_scale, mla_q_norm,
                        mla_kv_norm, ml_conv_w, ml_conv_b, ml_gate_b, ml_head_norm, w_branchb)
        x = _resid_mm(merged, w_outb, l, x.reshape(m, d), md[:, 5], 1.0, s, tn=1024).reshape(b, s, d)
        h = _normmod(x, md[:, 6], md[:, 7])
        x = _ffn(x, h, fb_in, fb_out, l, md[:, 8])
    return _finalnorm(x, final_norm)
```

```python
import functools

import numpy as np
import jax
import jax.numpy as jnp
from jax import lax
from jax.experimental import pallas as pl
from jax.experimental.pallas import tpu as pltpu

F32 = jnp.float32
BF16 = jnp.bfloat16

D_MODEL = 4096
DEPTH = 2
CHUNK = 64
EPS = 1e-6
D_FF = 2 * D_MODEL
MIX_W = D_MODEL // 2
N_BRANCH = 3
N_MOD = 9
POOL_WINDOWS = (2, 4, 8, 16)
POOL_GW = MIX_W // len(POOL_WINDOWS)
MLA_NOPE = 128
MLA_ROPE = 64
MLA_V = 128
MLA_HEADS = MIX_W // MLA_V
MLA_Q_RANK = D_MODEL // 4
MLA_KV_RANK = 512
MLA_SCALE = (MLA_NOPE + MLA_ROPE) ** -0.5
Q_SCALE_LOG2 = MLA_SCALE * float(np.log2(np.e))
ROPE_THETA = 10000.0
ML_HEADS = 8
ML_DK = 128
ML_DV = MIX_W // ML_HEADS
CONV_W = 4

_SPLITS = (MIX_W, MLA_Q_RANK, MLA_KV_RANK, MLA_ROPE, ML_HEADS * ML_DK, ML_HEADS * ML_DK,
           ML_HEADS * ML_DV, ML_HEADS * ML_DV, ML_HEADS, ML_HEADS, N_BRANCH * D_MODEL)
_OFF = tuple(int(v) for v in np.cumsum((0,) + _SPLITS))

LANES = 128
HALO = 16
VMEM_LIMIT = 60 * 1024 * 1024

N_PLAIN = 3 * MIX_W
N_SIG = MIX_W + N_BRANCH * D_MODEL
N_LAT = MLA_KV_RANK + 3 * LANES
COL_PLAIN = 0
COL_SIG = COL_PLAIN + N_PLAIN
COL_CQ = COL_SIG + N_SIG
COL_LAT = COL_CQ + MLA_Q_RANK


def _params(sem):
    return pltpu.CompilerParams(dimension_semantics=sem, vmem_limit_bytes=VMEM_LIMIT)


def _mm_body(*refs, nw, ne, no, nk, epilogue):
    x_ref = refs[0]
    w_refs = refs[1:1 + nw]
    e_refs = refs[1 + nw:1 + nw + ne]
    o_refs = refs[1 + nw + ne:1 + nw + ne + no]
    acc_refs = refs[1 + nw + ne + no:]
    if nk == 1:
        accs = [jnp.dot(x_ref[...], w[0], preferred_element_type=F32) for w in w_refs]
        epilogue(accs, e_refs, o_refs)
        return
    k = pl.program_id(2)

    @pl.when(k == 0)
    def _():
        for a, w in zip(acc_refs, w_refs):
            a[...] = jnp.dot(x_ref[...], w[0], preferred_element_type=F32)

    @pl.when(k > 0)
    def _():
        for a, w in zip(acc_refs, w_refs):
            a[...] += jnp.dot(x_ref[...], w[0], preferred_element_type=F32)

    @pl.when(k == nk - 1)
    def _():
        epilogue([a[...] for a in acc_refs], e_refs, o_refs)


def _mm(x, ws, extras, outs, epilogue, *, tm, tn, nj, nk=1):
    m, kdim = x.shape
    tk = kdim // nk
    in_specs = [pl.BlockSpec((tm, tk), lambda i, j, k: (i, k))]
    for _, lay, col in ws:
        assert col % tn == 0
        in_specs.append(pl.BlockSpec((1, tk, tn), lambda i, j, k, lay=lay, off=col // tn: (lay, k, off + j)))
    for _, blk, f in extras:
        in_specs.append(pl.BlockSpec(blk, lambda i, j, k, f=f: f(i, j)))
    out_specs = [pl.BlockSpec(blk, lambda i, j, k, f=f: f(i, j)) for _, _, blk, f in outs]
    out_shape = [jax.ShapeDtypeStruct(s, d) for s, d, _, _ in outs]
    scratch = [pltpu.VMEM((tm, tn), F32) for _ in ws] if nk > 1 else []
    body = functools.partial(_mm_body, nw=len(ws), ne=len(extras), no=len(outs), nk=nk,
                             epilogue=epilogue)
    return pl.pallas_call(
        body, grid=(m // tm, nj, nk), in_specs=in_specs, out_specs=out_specs,
        out_shape=out_shape, scratch_shapes=scratch,
        compiler_params=_params(("parallel", "parallel", "arbitrary")),
    )(x, *[w for w, _, _ in ws], *[a for a, _, _ in extras])


def _rope128(v, cos, sina, sinb):
    return (v * cos + pltpu.roll(v, LANES - MLA_ROPE // 2, 1) * sina
            + pltpu.roll(v, MLA_ROPE // 2, 1) * sinb)


def _ada_kernel(c_ref, w_ref, b_ref, t_ref, o_ref):
    c = c_ref[...]
    s = c * jax.nn.sigmoid(c)
    acc = jnp.dot(s.astype(BF16), w_ref[...].astype(BF16), preferred_element_type=F32) + b_ref[...]
    for l in range(DEPTH):
        o_ref[l] = acc + t_ref[l]


def _ada(c, w_ada, b_ada, ada_table):
    b = c.shape[0]
    rows = 8
    c8 = jnp.zeros((rows, D_MODEL), F32).at[:b].set(c)
    n = N_MOD * D_MODEL
    tn = 1024
    out = pl.pallas_call(
        _ada_kernel, grid=(n // tn,),
        in_specs=[pl.BlockSpec((rows, D_MODEL), lambda j: (0, 0)),
                  pl.BlockSpec((D_MODEL, tn), lambda j: (0, j)),
                  pl.BlockSpec((1, tn), lambda j: (0, j)),
                  pl.BlockSpec((DEPTH, 1, tn), lambda j: (0, 0, j))],
        out_specs=pl.BlockSpec((DEPTH, rows, tn), lambda j: (0, 0, j)),
        out_shape=jax.ShapeDtypeStruct((DEPTH, rows, n), F32),
        compiler_params=_params(("parallel",)),
    )(c8, w_ada, b_ada.reshape(1, n), ada_table.reshape(DEPTH, 1, n))
    return out[:, :b].reshape(DEPTH, b, N_MOD, D_MODEL)


def _normmod_kernel(x_ref, shift_ref, scale_ref, o_ref):
    x = x_ref[0]
    y = x * lax.rsqrt(jnp.mean(x * x, axis=-1, keepdims=True) + EPS)
    o_ref[0] = (y * (1.0 + scale_ref[0]) + shift_ref[0]).astype(o_ref.dtype)


def _finalnorm_kernel(x_ref, w_ref, o_ref):
    x = x_ref[0]
    y = x * lax.rsqrt(jnp.mean(x * x, axis=-1, keepdims=True) + EPS)
    o_ref[0] = y * w_ref[...]


def _normmod(x, shift, scale, ts=512):
    b, s, d = x.shape
    vec = pl.BlockSpec((1, 1, d), lambda bi, i: (bi, 0, 0))
    return pl.pallas_call(
        _normmod_kernel, grid=(b, s // ts),
        in_specs=[pl.BlockSpec((1, ts, d), lambda bi, i: (bi, i, 0)), vec, vec],
        out_specs=pl.BlockSpec((1, ts, d), lambda bi, i: (bi, i, 0)),
        out_shape=jax.ShapeDtypeStruct((b, s, d), BF16),
        compiler_params=_params(("parallel", "parallel")),
    )(x, shift.reshape(b, 1, d), scale.reshape(b, 1, d))


def _finalnorm(x, w, ts=512):
    b, s, d = x.shape
    return pl.pallas_call(
        _finalnorm_kernel, grid=(b, s // ts),
        in_specs=[pl.BlockSpec((1, ts, d), lambda bi, i: (bi, i, 0)),
                  pl.BlockSpec((1, d), lambda bi, i: (0, 0))],
        out_specs=pl.BlockSpec((1, ts, d), lambda bi, i: (bi, i, 0)),
        out_shape=jax.ShapeDtypeStruct((b, s, d), F32),
        compiler_params=_params(("parallel", "parallel")),
    )(x, w.reshape(1, d))


def _resid_epilogue(accs, e_refs, o_refs, *, coef):
    x_ref, g_ref = e_refs
    o_refs[0][...] = x_ref[...] + (coef * g_ref[0]) * accs[0]


def _ws_body(*refs, nx, nwt, ne, no, pairs, tn, nj, epilogue):
    x_refs = refs[:nx]
    w_hbm = refs[nx:nx + nwt]
    e_refs = refs[nx + nwt:nx + nwt + ne]
    o_refs = refs[nx + nwt + ne:nx + nwt + ne + no]
    stage, wbf, sems = refs[nx + nwt + ne + no:]
    j = pl.program_id(0)
    i = pl.program_id(1)

    def tile_copy(jj, p):
        _, wi, lead, col = pairs[p]
        src = w_hbm[wi].at[(*lead, slice(None), pl.ds(pl.multiple_of(col + jj * tn, tn), tn))]
        return pltpu.make_async_copy(src, stage.at[p], sems.at[p])

    @pl.when(i == 0)
    def _():
        @pl.when(j == 0)
        def _():
            for p in range(len(pairs)):
                tile_copy(j, p).start()

        for p in range(len(pairs)):
            tile_copy(j, p).wait()
            wbf[p] = stage[p].astype(BF16)

        @pl.when(j + 1 < nj)
        def _():
            for p in range(len(pairs)):
                tile_copy(j + 1, p).start()

    accs = [jnp.dot(x_refs[xi][...], wbf[p], preferred_element_type=F32)
            for p, (xi, _, _, _) in enumerate(pairs)]
    epilogue(accs, e_refs, o_refs)


def _ws_mm(xs, wts, pairs, extras, outs, epilogue, *, tm, tn, nj):
    m, kdim = xs[0].shape
    assert all(c % tn == 0 for _, _, _, c in pairs)
    in_specs = [pl.BlockSpec((tm, kdim), lambda j, i: (i, 0)) for _ in xs]
    in_specs += [pl.BlockSpec(memory_space=pl.ANY) for _ in wts]
    in_specs += [pl.BlockSpec(blk, lambda j, i, f=f: f(i, j)) for _, blk, f in extras]
    out_specs = [pl.BlockSpec(blk, lambda j, i, f=f: f(i, j)) for _, _, blk, f in outs]
    out_shape = [jax.ShapeDtypeStruct(sh, dt) for sh, dt, _, _ in outs]
    body = functools.partial(_ws_body, nx=len(xs), nwt=len(wts), ne=len(extras), no=len(outs),
                             pairs=pairs, tn=tn, nj=nj, epilogue=epilogue)
    return pl.pallas_call(
        body, grid=(nj, m // tm), in_specs=in_specs, out_specs=out_specs, out_shape=out_shape,
        scratch_shapes=[pltpu.VMEM((len(pairs), kdim, tn), F32), pltpu.VMEM((len(pairs), kdim, tn), BF16),
                        pltpu.SemaphoreType.DMA((len(pairs),))],
        compiler_params=_params(("arbitrary", "arbitrary")),
    )(*xs, *wts, *[a for a, _, _ in extras])


def _swiglu_epilogue(accs, e_refs, o_refs):
    g, u = accs
    o_refs[0][...] = (g * jax.nn.sigmoid(g) * u).astype(BF16)


def _resid_mm(a, w, lay, x2d, gate, coef, seq, *, tm, tn):
    m, n = x2d.shape
    per_b = seq // tm
    return _ws_mm([a], [w], [(0, 0, (lay,), 0)],
                  [(x2d, (tm, tn), lambda i, j: (i, j)),
                   (gate.reshape(-1, 1, n), (1, 1, tn), lambda i, j: (i // per_b, 0, j))],
                  [((m, n), F32, (tm, tn), lambda i, j: (i, j))],
                  functools.partial(_resid_epilogue, coef=coef), tm=tm, tn=tn, nj=n // tn)[0]


def _swiglu_mm(h2, w_in, lay, *, tm=1024, tn=512):
    m, _ = h2.shape
    return _ws_mm([h2], [w_in], [(0, 0, (lay,), 0), (0, 0, (lay,), D_FF)], [],
                  [((m, D_FF), BF16, (tm, tn), lambda i, j: (i, j))],
                  _swiglu_epilogue, tm=tm, tn=tn, nj=D_FF // tn)[0]


def _ffn(x, h, w_in, w_out, lay, gate):
    b, s, d = x.shape
    m = b * s
    a = _swiglu_mm(h.reshape(m, d), w_in, lay)
    y = _resid_mm(a, w_out, lay, x.reshape(m, d), gate, 0.5, s, tm=512, tn=512)
    return y.reshape(b, s, d)


def _rope_tab_kernel(p_ref, inv_ref, cos_ref, sina_ref, sinb_ref):
    ang = p_ref[...].astype(F32) * inv_ref[...]
    lane = lax.broadcasted_iota(jnp.int32, ang.shape, 1)
    half = MLA_ROPE // 2
    c = jnp.cos(ang)
    s = jnp.sin(ang)
    cos_ref[...] = jnp.where(lane < MLA_ROPE, c, 0.0)
    sina_ref[...] = jnp.where(lane < half, -s, 0.0)
    sinb_ref[...] = jnp.where(lane >= half, jnp.where(lane < MLA_ROPE, s, 0.0), 0.0)


def _rope_tables(positions, ts=512):
    m = positions.size
    half = MLA_ROPE // 2
    inv = ROPE_THETA ** (-jnp.arange(0, MLA_ROPE, 2, dtype=F32) / MLA_ROPE)
    inv128 = jnp.concatenate([inv, inv, jnp.zeros((LANES - 2 * half,), F32)]).reshape(1, LANES)
    spec = pl.BlockSpec((ts, LANES), lambda i: (i, 0))
    shp = jax.ShapeDtypeStruct((m, LANES), F32)
    return pl.pallas_call(
        _rope_tab_kernel, grid=(m // ts,),
        in_specs=[pl.BlockSpec((ts, 1), lambda i: (i, 0)), pl.BlockSpec((1, LANES), lambda i: (0, 0))],
        out_specs=[spec, spec, spec], out_shape=[shp, shp, shp],
        compiler_params=_params(("parallel",)),
    )(positions.reshape(m, 1), inv128)


def _plain_epilogue(accs, e_refs, o_refs):
    o_refs[0][...] = accs[0].astype(o_refs[0].dtype)


def _sigmoid_epilogue(accs, e_refs, o_refs):
    o_refs[0][...] = jax.nn.sigmoid(accs[0]).astype(o_refs[0].dtype)


def _rmsw_epilogue(accs, e_refs, o_refs):
    a = accs[0]
    y = a * lax.rsqrt(jnp.mean(a * a, axis=-1, keepdims=True) + EPS) * e_refs[0][0]
    o_refs[0][...] = y.astype(o_refs[0].dtype)


def _kvlatent_epilogue(accs, e_refs, o_refs):
    a = accs[0]
    w_ref, cos_ref, sina_ref, sinb_ref = e_refs
    ckv = a[:, :MLA_KV_RANK]
    y = ckv * lax.rsqrt(jnp.mean(ckv * ckv, axis=-1, keepdims=True) + EPS) * w_ref[0]
    o_refs[0][...] = y.astype(BF16)
    kr = a[:, MLA_KV_RANK:MLA_KV_RANK + LANES]
    o_refs[1][...] = _rope128(kr, cos_ref[...], sina_ref[...], sinb_ref[...]).astype(BF16)
    o_refs[2][...] = a[:, MLA_KV_RANK + LANES:]


def _qrope_epilogue(accs, e_refs, o_refs, *, heads_per_tile):
    a = accs[0] * Q_SCALE_LOG2
    cos_ref, sina_ref, sinb_ref = e_refs
    cos, sina, sinb = cos_ref[...], sina_ref[...], sinb_ref[...]
    for c in range(heads_per_tile):
        sl = slice(c * LANES, (c + 1) * LANES)
        o_refs[0][:, sl] = _rope128(a[:, sl], cos, sina, sinb).astype(BF16)


def _scaled_epilogue(accs, e_refs, o_refs, *, scale):
    o_refs[0][...] = (accs[0] * scale).astype(o_refs[0].dtype)


def _band(ts, lo, hi, first_tile):
    t = lax.broadcasted_iota(jnp.int32, (ts, HALO + ts), 0)
    s = lax.broadcasted_iota(jnp.int32, (ts, HALO + ts), 1)
    d = t + HALO - s
    ok = jnp.where(d >= lo, jnp.where(d < hi, 1.0, 0.0), 0.0)
    ok = jnp.where(s < HALO, jnp.where(first_tile, 0.0, ok), ok)
    return ok.astype(BF16)


def _pool_kernel(u_ref, halo_ref, pw_ref, ps_ref, o_ref, *, ts):
    i = pl.program_id(1)
    u = u_ref[0]
    ucat = jnp.concatenate([halo_ref[0], u], axis=0)
    tg = i * ts + lax.broadcasted_iota(jnp.int32, (ts, 1), 0)
    for g, w in enumerate(POOL_WINDOWS):
        sl = slice(g * POOL_GW, (g + 1) * POOL_GW)
        win = jnp.dot(_band(ts, 0, w, i == 0), ucat[:, sl], preferred_element_type=F32)
        cnt = jnp.minimum(tg + 1, w).astype(F32)
        p = win / cnt - u[:, sl].astype(F32)
        y = jnp.dot(p.astype(BF16), pw_ref[0, g], preferred_element_type=F32)
        o_ref[0, :, sl] = (y * ps_ref[0, :, sl]).astype(BF16)


def _pool(zp, pool_w, pool_scale, lay, b, s, ts=256):
    hb = ts // HALO
    ng = len(POOL_WINDOWS)
    return pl.pallas_call(
        functools.partial(_pool_kernel, ts=ts), grid=(b, s // ts),
        in_specs=[pl.BlockSpec((1, ts, MIX_W), lambda bi, i: (bi, i, 0)),
                  pl.BlockSpec((1, HALO, MIX_W), lambda bi, i: (bi, jnp.maximum(i * hb - 1, 0), 0)),
                  pl.BlockSpec((1, ng, POOL_GW, POOL_GW), lambda bi, i: (lay, 0, 0, 0)),
                  pl.BlockSpec((1, 1, MIX_W), lambda bi, i: (lay, 0, 0))],
        out_specs=pl.BlockSpec((1, ts, MIX_W), lambda bi, i: (bi, i, 0)),
        out_shape=jax.ShapeDtypeStruct((b, s, MIX_W), BF16),
        compiler_params=_params(("parallel", "parallel")),
    )(zp, zp, pool_w, pool_scale.reshape(DEPTH, 1, MIX_W))


def _conv_kernel(x_ref, halo_ref, w_ref, b_ref, sc_ref, o_ref, *, ts):
    i = pl.program_id(1)
    x = x_ref[0]
    xcat = jnp.concatenate([halo_ref[0], x], axis=0)
    w = w_ref[0]
    acc = x.astype(F32) * w[CONV_W - 1:CONV_W, :] + b_ref[0]
    for d in range(1, CONV_W):
        xs = jnp.dot(_band(ts, d, d + 1, i == 0), xcat, preferred_element_type=F32)
        acc = acc + xs * w[CONV_W - 1 - d:CONV_W - d, :]
    o_ref[0] = (acc * jax.nn.sigmoid(acc) * sc_ref[...]).astype(BF16)


def _conv_silu(zp, conv_w, conv_b, lay, b, s, ts=256):
    c = 2 * ML_HEADS * ML_DK
    hb = ts // HALO
    post = jnp.concatenate([jnp.full((c // 2,), ML_DK ** -0.5, F32), jnp.ones((c // 2,), F32)])
    return pl.pallas_call(
        functools.partial(_conv_kernel, ts=ts), grid=(b, s // ts),
        in_specs=[pl.BlockSpec((1, ts, c), lambda bi, i: (bi, i, 1)),
                  pl.BlockSpec((1, HALO, c), lambda bi, i: (bi, jnp.maximum(i * hb - 1, 0), 1)),
                  pl.BlockSpec((1, CONV_W, c), lambda bi, i: (lay, 0, 0)),
                  pl.BlockSpec((1, 1, c), lambda bi, i: (lay, 0, 0)),
                  pl.BlockSpec((1, c), lambda bi, i: (0, 0))],
        out_specs=pl.BlockSpec((1, ts, c), lambda bi, i: (bi, i, 0)),
        out_shape=jax.ShapeDtypeStruct((b, s, c), BF16),
        compiler_params=_params(("parallel", "parallel")),
    )(zp, zp, conv_w, conv_b.reshape(DEPTH, 1, c), post.reshape(1, c))


ATTN_HEADS_PER_STEP = 8


def _attn_kernel(qn_ref, qr_ref, kn_ref, kr_ref, v_ref, pq_ref, pk_ref, o_ref, m_sc, l_sc, acc_sc, *, tq):
    i = pl.program_id(2)
    nt = (((1,), (1,)), ((), ()))
    tn = (((0,), (0,)), ((), ()))
    hs = [slice(g * LANES, (g + 1) * LANES) for g in range(ATTN_HEADS_PER_STEP)]

    def block(start, mask, first):
        kr = kr_ref[0, pl.ds(start, tq), :]

        def scores(g):
            sl = hs[g]
            q = jnp.concatenate([qn_ref[0, :, sl], qr_ref[0, :, sl]], axis=-1)
            k = jnp.concatenate([kn_ref[0, pl.ds(start, tq), sl], kr], axis=-1)
            st = lax.dot_general(k, q, nt, preferred_element_type=F32)
            if mask is not None:
                st = jnp.where(mask, st, -jnp.inf)
            return st

        st_next = scores(0)
        for g, sl in enumerate(hs):
            st = st_next
            if g + 1 < len(hs):
                st_next = scores(g + 1)
            v = v_ref[0, pl.ds(start, tq), sl]
            smax = jnp.max(st, axis=0, keepdims=True)
            if first:
                m_new = smax
                p = jnp.exp2(st - m_new)
                l_sc[g] = jnp.sum(p, axis=0, keepdims=True)
                acc_sc[g] = lax.dot_general(v, p.astype(BF16), tn, preferred_element_type=F32)
            else:
                m_old = m_sc[g]
                m_new = jnp.maximum(m_old, smax)
                alpha = jnp.exp2(m_old - m_new)
                p = jnp.exp2(st - m_new)
                l_sc[g] = alpha * l_sc[g] + jnp.sum(p, axis=0, keepdims=True)
                acc_sc[g] = alpha * acc_sc[g] + lax.dot_general(v, p.astype(BF16), tn,
                                                                preferred_element_type=F32)
            m_sc[g] = m_new

    sh = CHUNK.bit_length() - 1
    mask = lax.shift_right_arithmetic(pk_ref[0], sh) <= lax.shift_right_arithmetic(pq_ref[0], sh)
    block(pl.multiple_of(i * tq, tq), mask, True)

    def body(j, carry):
        block(pl.multiple_of(j * tq, tq), None, False)
        return carry

    lax.fori_loop(0, i, body, 0)
    for g, sl in enumerate(hs):
        o_ref[0, :, sl] = (acc_sc[g] / l_sc[g]).T.astype(o_ref.dtype)


def _attention(q, kv, kr, positions, tq=512):
    b, s, _ = q.shape
    G = ATTN_HEADS_PER_STEP
    w = G * LANES
    ng = MLA_HEADS // G
    return pl.pallas_call(
        functools.partial(_attn_kernel, tq=tq), grid=(b, ng, s // tq),
        in_specs=[pl.BlockSpec((1, tq, w), lambda bi, hi, i: (bi, i, hi)),
                  pl.BlockSpec((1, tq, w), lambda bi, hi, i: (bi, i, ng + hi)),
                  pl.BlockSpec((1, s, w), lambda bi, hi, i: (bi, 0, hi)),
                  pl.BlockSpec((1, s, LANES), lambda bi, hi, i: (bi, 0, 0)),
                  pl.BlockSpec((1, s, w), lambda bi, hi, i: (bi, 0, ng + hi)),
                  pl.BlockSpec((1, 1, tq), lambda bi, hi, i: (bi, 0, i)),
                  pl.BlockSpec((1, tq, 1), lambda bi, hi, i: (bi, i, 0))],
        out_specs=pl.BlockSpec((1, tq, w), lambda bi, hi, i: (bi, i, hi)),
        out_shape=jax.ShapeDtypeStruct((b, s, MLA_HEADS * MLA_V), BF16),
        scratch_shapes=[pltpu.VMEM((G, 1, tq), F32), pltpu.VMEM((G, 1, tq), F32),
                        pltpu.VMEM((G, MLA_V, tq), F32)],
        compiler_params=_params(("parallel", "parallel", "arbitrary")),
    )(q, q, kv, kr, kv, positions.reshape(b, 1, s), positions.reshape(b, s, 1))


def _split3_dot(a_bf16, x):
    hi = x.astype(BF16)
    r1 = x - hi.astype(F32)
    mid = r1.astype(BF16)
    lo = (r1 - mid.astype(F32)).astype(BF16)
    return (jnp.dot(a_bf16, hi, preferred_element_type=F32)
            + jnp.dot(a_bf16, mid, preferred_element_type=F32)
            + jnp.dot(a_bf16, lo, preferred_element_type=F32))


def _mlstm_kernel(qk_ref, v_ref, g_ref, og_ref, gb_ref, hn_ref, o_ref, ct_ref, m_ref):
    cidx = pl.program_id(1)

    @pl.when(cidx == 0)
    def _():
        ct_ref[...] = jnp.zeros_like(ct_ref)
        m_ref[...] = jnp.zeros_like(m_ref)

    L = CHUNK
    g = g_ref[0]
    gb = gb_ref[...]
    logi = g[:, :LANES] + gb[:, :LANES]
    xf = g[:, LANES:] + gb[:, LANES:]
    logf = jnp.minimum(xf, 0.0) - jnp.log1p(jnp.exp(-jnp.abs(xf)))
    row = lax.broadcasted_iota(jnp.int32, (L, L), 0)
    col = lax.broadcasted_iota(jnp.int32, (L, L), 1)
    tril = row >= col
    bcum = _split3_dot(jnp.where(tril, 1.0, 0.0).astype(BF16), logf)
    x = logi - bcum
    xt = x.T
    m_row = m_ref[...]
    b_last = bcum[L - 1:L, :]
    m_new = jnp.maximum(b_last + m_row, b_last + jnp.max(x, axis=0, keepdims=True))
    decay = jnp.exp(b_last + m_row - m_new)
    ws_all = jnp.exp(b_last + x - m_new)
    inter_all = bcum + m_row
    ones_col = jnp.where(lax.broadcasted_iota(jnp.int32, (L, LANES), 1) == 0, 1.0, 0.0).astype(BF16)
    nt = (((1,), (1,)), ((), ()))
    tn = (((0,), (0,)), ((), ()))
    def early(h):
        q = qk_ref[0, :, h * ML_DK:(h + 1) * ML_DK]
        k = qk_ref[0, :, (ML_HEADS + h) * ML_DK:(ML_HEADS + h + 1) * ML_DK]
        ct = ct_ref[h]
        qk = lax.dot_general(q, k, nt, preferred_element_type=F32)
        qc = jnp.dot(q, ct.astype(BF16), preferred_element_type=F32)
        dlog = jnp.where(tril, bcum[:, h:h + 1] + xt[h:h + 1, :], -jnp.inf)
        inter = inter_all[:, h:h + 1]
        mt = jnp.maximum(inter, jnp.max(dlog, axis=-1, keepdims=True))
        return k, ct, qk, qc, mt, jnp.exp(inter - mt), jnp.exp(dlog - mt)

    ahead = early(0)
    for h in range(ML_HEADS):
        k, ct, qk, qc, mt, w_inter, e = ahead
        if h + 1 < ML_HEADS:
            ahead = early(h + 1)
        vaug = jnp.concatenate([v_ref[0, :, h * ML_DV:(h + 1) * ML_DV], ones_col], axis=-1)
        num = jnp.dot((qk * e).astype(BF16), vaug, preferred_element_type=F32) + w_inter * qc
        den = num[:, ML_DV:ML_DV + 1]
        hh = num[:, :ML_DV] / jnp.maximum(jnp.abs(den), jnp.exp(-mt))
        wv = (ws_all[:, h:h + 1] * vaug.astype(F32)).astype(BF16)
        ct_ref[h] = decay[:, h:h + 1] * ct + lax.dot_general(k, wv, tn, preferred_element_type=F32)
        hs = slice(h * ML_DV, (h + 1) * ML_DV)
        hn = hh * lax.rsqrt(jnp.mean(hh * hh, axis=-1, keepdims=True) + EPS) * hn_ref[0, :, hs]
        o_ref[0, :, hs] = (og_ref[0, :, hs].astype(F32) * hn).astype(BF16)
    m_ref[...] = m_new


def _mlstm(qk, zp, gates, zs, gate_b, head_norm, lay):
    b, s, _ = qk.shape
    gb = jnp.zeros((1, 2 * LANES), F32)
    gb = gb.at[0, :ML_HEADS].set(gate_b[:ML_HEADS]).at[0, LANES:LANES + ML_HEADS].set(gate_b[ML_HEADS:])
    blk = lambda col: pl.BlockSpec((1, CHUNK, MIX_W), lambda bi, c, col=col: (bi, c, col))
    return pl.pallas_call(
        _mlstm_kernel, grid=(b, s // CHUNK),
        in_specs=[blk(0), blk(2),
                  pl.BlockSpec((1, CHUNK, 2 * LANES), lambda bi, c: (bi, c, 0)),
                  blk(0),
                  pl.BlockSpec((1, 2 * LANES), lambda bi, c: (0, 0)),
                  pl.BlockSpec((1, 1, MIX_W), lambda bi, c: (lay, 0, 0))],
        out_specs=blk(0),
        out_shape=jax.ShapeDtypeStruct((b, s, MIX_W), BF16),
        scratch_shapes=[pltpu.VMEM((ML_HEADS, ML_DK, ML_DV + LANES), F32), pltpu.VMEM((1, LANES), F32)],
        compiler_params=_params(("parallel", "arbitrary")),
    )(qk, zp, gates, zs, gb, head_norm.reshape(DEPTH, 1, MIX_W))


def _merge_epilogue(accs, e_refs, o_refs):
    acc = e_refs[0][...].astype(F32) * accs[0]
    acc += e_refs[1][...].astype(F32) * accs[1]
    acc += e_refs[2][...].astype(F32) * accs[2]
    o_refs[0][...] = acc.astype(BF16)


def _merge(ya, yb, yc, w_branch, lay, zs, tm=1024, tn=512):
    m = ya.shape[0]
    goff = MIX_W // tn
    gate = lambda k: (zs, (tm, tn), lambda i, j, k=k: (i, goff + k * (D_MODEL // tn) + j))
    return _ws_mm([ya, yb, yc], [w_branch], [(k, 0, (lay, k), 0) for k in range(N_BRANCH)],
                  [gate(k) for k in range(N_BRANCH)],
                  [((m, D_MODEL), BF16, (tm, tn), lambda i, j: (i, j))],
                  _merge_epilogue, tm=tm, tn=tn, nj=D_MODEL // tn)[0]


_REGROUP = ((_OFF[0], _OFF[1], None), (_OFF[4], _OFF[7], None),
            (_OFF[7], _OFF[8], None), (_OFF[10], _OFF[11], None),
            (_OFF[1], _OFF[2], None),
            (_OFF[2], _OFF[3], None), (_OFF[3], _OFF[4], LANES),
            (_OFF[8], _OFF[9], LANES), (_OFF[9], _OFF[10], LANES))
N_REGROUP = sum(p if p else hi - lo for lo, hi, p in _REGROUP)
assert N_REGROUP == COL_LAT + N_LAT


REGROUP_ROWS = 512


def _regroup_kernel(wt_ref, o_ref):
    col = 0
    for lo, hi, padded in _REGROUP:
        n = hi - lo
        if padded:
            x = jnp.concatenate([wt_ref[0, lo:hi, :], jnp.zeros((padded - n, wt_ref.shape[2]), F32)], axis=0)
            o_ref[0, :, col:col + padded] = x.T.astype(BF16)
            col += padded
        else:
            for a in range(0, n, REGROUP_ROWS):
                o_ref[0, :, col + a:col + a + REGROUP_ROWS] = (
                    wt_ref[0, lo + a:lo + a + REGROUP_ROWS, :].T.astype(BF16))
            col += n


def _regroup_mix_in(w, tc=128):
    wt = jnp.swapaxes(w, 1, 2)
    dep, n, k = wt.shape
    return pl.pallas_call(
        _regroup_kernel, grid=(dep, k // tc),
        in_specs=[pl.BlockSpec((1, n, tc), lambda l, i: (l, 0, i))],
        out_specs=pl.BlockSpec((1, tc, N_REGROUP), lambda l, i: (l, i, 0)),
        out_shape=jax.ShapeDtypeStruct((dep, k, N_REGROUP), BF16),
        compiler_params=_params(("parallel", "parallel")),
    )(wt)


def _prep_mixer_weights(w_mix_in, mla_w_uq, mla_w_ukv):
    w_all = _regroup_mix_in(w_mix_in)
    uq = mla_w_uq.reshape(DEPTH, MLA_Q_RANK, MLA_HEADS, MLA_NOPE + MLA_ROPE)
    w_qn = uq[..., :MLA_NOPE].reshape(DEPTH, MLA_Q_RANK, MLA_HEADS * MLA_NOPE)
    w_qr = jnp.pad(uq[..., MLA_NOPE:], ((0, 0), (0, 0), (0, 0), (0, LANES - MLA_ROPE)))
    w_q = jnp.concatenate([w_qn, w_qr.reshape(DEPTH, MLA_Q_RANK, MLA_HEADS * LANES)], axis=-1).astype(BF16)
    ukv = mla_w_ukv.reshape(DEPTH, MLA_KV_RANK, MLA_HEADS, 2, MLA_NOPE)
    w_kv = ukv.transpose(0, 1, 3, 2, 4).reshape(DEPTH, MLA_KV_RANK, 2 * MLA_HEADS * MLA_NOPE).astype(BF16)
    return w_all, w_q, w_kv


def _mixer(h, lay, positions, rope_tabs, w_all, w_q, w_kv, pool_w, pool_scale, mla_q_norm, mla_kv_norm,
           ml_conv_w, ml_conv_b, ml_gate_b, ml_head_norm, w_branch, tm=1024):
    b, s, d = h.shape
    m = b * s
    h2 = h.reshape(m, d)
    cos, sina, sinb = rope_tabs
    tab = lambda a: (a, (tm, LANES), lambda i, j: (i, 0))
    tile = lambda i, j: (i, j)
    layvec = lambda a: (a.reshape(DEPTH, 1, -1), (1, 1, a.shape[-1]), lambda i, j: (lay, 0, 0))

    wide = 1024
    zp = _mm(h2, [(w_all, lay, COL_PLAIN)], [], [((m, N_PLAIN), BF16, (tm, wide), tile)],
             _plain_epilogue, tm=tm, tn=wide, nj=N_PLAIN // wide)[0]
    zs = _mm(h2, [(w_all, lay, COL_SIG)], [], [((m, N_SIG), BF16, (tm, wide), tile)],
             _sigmoid_epilogue, tm=tm, tn=wide, nj=N_SIG // wide)[0]
    cqn = _mm(h2, [(w_all, lay, COL_CQ)], [layvec(mla_q_norm)],
              [((m, MLA_Q_RANK), BF16, (tm, MLA_Q_RANK), tile)],
              _rmsw_epilogue, tm=tm, tn=MLA_Q_RANK, nj=1)[0]
    ckvn, kr, gates = _mm(
        h2, [(w_all, lay, COL_LAT)], [layvec(mla_kv_norm), tab(cos), tab(sina), tab(sinb)],
        [((m, MLA_KV_RANK), BF16, (tm, MLA_KV_RANK), tile),
         ((m, LANES), BF16, (tm, LANES), tile),
         ((m, 2 * LANES), F32, (tm, 2 * LANES), tile)],
        _kvlatent_epilogue, tm=tm, tn=N_LAT, nj=1)

    zp3 = zp.reshape(b, s, N_PLAIN)
    ya = _pool(zp3, pool_w, pool_scale, lay, b, s)

    nq = MLA_HEADS * LANES
    q = _mm(cqn, [(w_q, lay, 0)], [tab(cos), tab(sina), tab(sinb)], [((m, 2 * nq), BF16, (tm, nq), tile)],
            functools.partial(_q_epilogue, heads_per_tile=MLA_HEADS, n_nope_tiles=1),
            tm=tm, tn=nq, nj=2)[0]
    kv = _mm(ckvn, [(w_kv, lay, 0)], [], [((m, 2 * nq), BF16, (tm, nq), tile)],
             _plain_epilogue, tm=tm, tn=nq, nj=2)[0]
    yb = _attention(q.reshape(b, s, 2 * nq), kv.reshape(b, s, 2 * nq), kr.reshape(b, s, LANES), positions)

    qk = _conv_silu(zp3, ml_conv_w, ml_conv_b, lay, b, s)
    yc = _mlstm(qk, zp3, gates.reshape(b, s, 2 * LANES), zs.reshape(b, s, N_SIG), ml_gate_b[lay],
                ml_head_norm, lay)

    return _merge(ya.reshape(m, MIX_W), yb.reshape(m, MIX_W), yc.reshape(m, MIX_W), w_branch, lay, zs)


def _q_epilogue(accs, e_refs, o_refs, *, heads_per_tile, n_nope_tiles):
    j = pl.program_id(1)

    @pl.when(j < n_nope_tiles)
    def _():
        _scaled_epilogue(accs, (), o_refs, scale=Q_SCALE_LOG2)

    @pl.when(j >= n_nope_tiles)
    def _():
        _qrope_epilogue(accs, e_refs, o_refs, heads_per_tile=heads_per_tile)


def kernel(x, c, positions, w_ada, b_ada, ada_table, ffn_a_w_in, ffn_a_w_out, w_mix_in, pool_w, pool_scale, mla_q_norm, mla_w_uq, mla_kv_norm, mla_w_ukv, ml_conv_w, ml_conv_b, ml_gate_b, ml_head_norm, w_branch, w_out, ffn_b_w_in, ffn_b_w_out, final_norm):
    b, s, d = x.shape
    m = b * s
    mod = _ada(c, w_ada, b_ada, ada_table)
    rope_tabs = _rope_tables(positions)
    w_all, w_q, w_kv = _prep_mixer_weights(w_mix_in, mla_w_uq, mla_w_ukv)
    pool_wb = pool_w.astype(BF16)
    for l in range(DEPTH):
        md = mod[l]
        h = _normmod(x, md[:, 0], md[:, 1])
        x = _ffn(x, h, ffn_a_w_in, ffn_a_w_out, l, md[:, 2])
        h = _normmod(x, md[:, 3], md[:, 4])
        merged = _mixer(h, l, positions, rope_tabs, w_all, w_q, w_kv, pool_wb, pool_scale, mla_q_norm,
                        mla_kv_norm, ml_conv_w, ml_conv_b, ml_gate_b, ml_head_norm, w_branch)
        x = _resid_mm(merged, w_out, l, x.reshape(m, d), md[:, 5], 1.0, s, tm=1024, tn=512).reshape(b, s, d)
        h = _normmod(x, md[:, 6], md[:, 7])
        x = _ffn(x, h, ffn_b_w_in, ffn_b_w_out, l, md[:, 8])
    return _finalnorm(x, final_norm)
```

```python
import functools

import numpy as np
import jax
import jax.numpy as jnp
from jax import lax
from jax.experimental import pallas as pl
from jax.experimental.pallas import tpu as pltpu

F32 = jnp.float32
BF16 = jnp.bfloat16

D_MODEL = 4096
DEPTH = 2
CHUNK = 64
EPS = 1e-6
D_FF = 2 * D_MODEL
MIX_W = D_MODEL // 2
N_BRANCH = 3
N_MOD = 9
POOL_WINDOWS = (2, 4, 8, 16)
POOL_GW = MIX_W // len(POOL_WINDOWS)
MLA_NOPE = 128
MLA_ROPE = 64
MLA_V = 128
MLA_HEADS = MIX_W // MLA_V
MLA_Q_RANK = D_MODEL // 4
MLA_KV_RANK = 512
MLA_SCALE = (MLA_NOPE + MLA_ROPE) ** -0.5
Q_SCALE_LOG2 = MLA_SCALE * float(np.log2(np.e))
ROPE_THETA = 10000.0
ML_HEADS = 8
ML_DK = 128
ML_DV = MIX_W // ML_HEADS
CONV_W = 4

_SPLITS = (MIX_W, MLA_Q_RANK, MLA_KV_RANK, MLA_ROPE, ML_HEADS * ML_DK, ML_HEADS * ML_DK,
           ML_HEADS * ML_DV, ML_HEADS * ML_DV, ML_HEADS, ML_HEADS, N_BRANCH * D_MODEL)
_OFF = tuple(int(v) for v in np.cumsum((0,) + _SPLITS))

LANES = 128
HALO = 16
VMEM_LIMIT = 60 * 1024 * 1024

N_PLAIN = 3 * MIX_W
N_SIG = MIX_W + N_BRANCH * D_MODEL
N_LAT = MLA_KV_RANK + 3 * LANES
COL_PLAIN = 0
COL_SIG = COL_PLAIN + N_PLAIN
COL_CQ = COL_SIG + N_SIG
COL_LAT = COL_CQ + MLA_Q_RANK


def _params(sem):
    return pltpu.CompilerParams(dimension_semantics=sem, vmem_limit_bytes=VMEM_LIMIT)


def _mm_body(*refs, nw, ne, no, epilogue):
    x_ref = refs[0]
    w_refs = refs[1:1 + nw]
    e_refs = refs[1 + nw:1 + nw + ne]
    o_refs = refs[1 + nw + ne:1 + nw + ne + no]
    accs = [jnp.dot(x_ref[...], w[0], preferred_element_type=F32) for w in w_refs]
    epilogue(accs, e_refs, o_refs)


def _mm(x, ws, extras, outs, epilogue, *, tm, tn, nj):
    m, kdim = x.shape
    in_specs = [pl.BlockSpec((tm, kdim), lambda i, j: (i, 0))]
    for _, lay, col in ws:
        assert col % tn == 0
        in_specs.append(pl.BlockSpec((1, kdim, tn), lambda i, j, lay=lay, off=col // tn: (lay, 0, off + j)))
    for _, blk, f in extras:
        in_specs.append(pl.BlockSpec(blk, f))
    out_specs = [pl.BlockSpec(blk, f) for _, _, blk, f in outs]
    out_shape = [jax.ShapeDtypeStruct(s, d) for s, d, _, _ in outs]
    body = functools.partial(_mm_body, nw=len(ws), ne=len(extras), no=len(outs), epilogue=epilogue)
    return pl.pallas_call(
        body, grid=(m // tm, nj), in_specs=in_specs, out_specs=out_specs, out_shape=out_shape,
        compiler_params=_params(("parallel", "parallel")),
    )(x, *[w for w, _, _ in ws], *[a for a, _, _ in extras])


def _sigmoid(x):
    return 0.5 * jnp.tanh(0.5 * x) + 0.5


def _rope128(v, cos, sina, sinb):
    return (v * cos + pltpu.roll(v, LANES - MLA_ROPE // 2, 1) * sina
            + pltpu.roll(v, MLA_ROPE // 2, 1) * sinb)


def _ada_kernel(c_ref, w_ref, b_ref, t_ref, o_ref):
    c = c_ref[...]
    s = c * jax.nn.sigmoid(c)
    acc = jnp.dot(s.astype(BF16), w_ref[...].astype(BF16), preferred_element_type=F32) + b_ref[...]
    for l in range(DEPTH):
        o_ref[l] = acc + t_ref[l]


def _ada(c, w_ada, b_ada, ada_table):
    b = c.shape[0]
    rows = 8
    c8 = jnp.zeros((rows, D_MODEL), F32).at[:b].set(c)
    n = N_MOD * D_MODEL
    tn = 1024
    out = pl.pallas_call(
        _ada_kernel, grid=(n // tn,),
        in_specs=[pl.BlockSpec((rows, D_MODEL), lambda j: (0, 0)),
                  pl.BlockSpec((D_MODEL, tn), lambda j: (0, j)),
                  pl.BlockSpec((1, tn), lambda j: (0, j)),
                  pl.BlockSpec((DEPTH, 1, tn), lambda j: (0, 0, j))],
        out_specs=pl.BlockSpec((DEPTH, rows, tn), lambda j: (0, 0, j)),
        out_shape=jax.ShapeDtypeStruct((DEPTH, rows, n), F32),
        compiler_params=_params(("parallel",)),
    )(c8, w_ada, b_ada.reshape(1, n), ada_table.reshape(DEPTH, 1, n))
    return out[:, :b].reshape(DEPTH, b, N_MOD, D_MODEL)


def _normmod_kernel(x_ref, shift_ref, scale_ref, o_ref):
    x = x_ref[0]
    y = x * lax.rsqrt(jnp.mean(x * x, axis=-1, keepdims=True) + EPS)
    o_ref[0] = (y * (1.0 + scale_ref[0]) + shift_ref[0]).astype(o_ref.dtype)


def _finalnorm_kernel(x_ref, w_ref, o_ref):
    x = x_ref[0]
    y = x * lax.rsqrt(jnp.mean(x * x, axis=-1, keepdims=True) + EPS)
    o_ref[0] = y * w_ref[...]


def _normmod(x, shift, scale, ts=512):
    b, s, d = x.shape
    vec = pl.BlockSpec((1, 1, d), lambda bi, i: (bi, 0, 0))
    return pl.pallas_call(
        _normmod_kernel, grid=(b, s // ts),
        in_specs=[pl.BlockSpec((1, ts, d), lambda bi, i: (bi, i, 0)), vec, vec],
        out_specs=pl.BlockSpec((1, ts, d), lambda bi, i: (bi, i, 0)),
        out_shape=jax.ShapeDtypeStruct((b, s, d), BF16),
        compiler_params=_params(("parallel", "parallel")),
    )(x, shift.reshape(b, 1, d), scale.reshape(b, 1, d))


def _finalnorm(x, w, ts=512):
    b, s, d = x.shape
    return pl.pallas_call(
        _finalnorm_kernel, grid=(b, s // ts),
        in_specs=[pl.BlockSpec((1, ts, d), lambda bi, i: (bi, i, 0)),
                  pl.BlockSpec((1, d), lambda bi, i: (0, 0))],
        out_specs=pl.BlockSpec((1, ts, d), lambda bi, i: (bi, i, 0)),
        out_shape=jax.ShapeDtypeStruct((b, s, d), F32),
        compiler_params=_params(("parallel", "parallel")),
    )(x, w.reshape(1, d))


def _resid_epilogue(accs, e_refs, o_refs, *, coef):
    x_ref, g_ref = e_refs
    o_refs[0][...] = x_ref[...] + (coef * g_ref[0]) * accs[0]


def _ws_body(*refs, nx, nwt, ne, no, pairs, tn, nj, epilogue):
    x_refs = refs[:nx]
    w_hbm = refs[nx:nx + nwt]
    e_refs = refs[nx + nwt:nx + nwt + ne]
    o_refs = refs[nx + nwt + ne:nx + nwt + ne + no]
    stage, wbf, sems = refs[nx + nwt + ne + no:]
    j = pl.program_id(0)
    i = pl.program_id(1)

    def tile_copy(jj, p):
        _, wi, lead, col = pairs[p]
        src = w_hbm[wi].at[(*lead, slice(None), pl.ds(pl.multiple_of(col + jj * tn, tn), tn))]
        return pltpu.make_async_copy(src, stage.at[p], sems.at[p])

    @pl.when(i == 0)
    def _():
        @pl.when(j == 0)
        def _():
            for p in range(len(pairs)):
                tile_copy(j, p).start()

        for p in range(len(pairs)):
            tile_copy(j, p).wait()
            wbf[p] = stage[p].astype(BF16)

        @pl.when(j + 1 < nj)
        def _():
            for p in range(len(pairs)):
                tile_copy(j + 1, p).start()

    accs = [jnp.dot(x_refs[xi][...], wbf[p], preferred_element_type=F32)
            for p, (xi, _, _, _) in enumerate(pairs)]
    epilogue(accs, e_refs, o_refs)


def _ws_mm(xs, wts, pairs, extras, outs, epilogue, *, tm, tn, nj):
    m, kdim = xs[0].shape
    assert all(c % tn == 0 for _, _, _, c in pairs)
    in_specs = [pl.BlockSpec((tm, kdim), lambda j, i: (i, 0)) for _ in xs]
    in_specs += [pl.BlockSpec(memory_space=pl.ANY) for _ in wts]
    in_specs += [pl.BlockSpec(blk, lambda j, i, f=f: f(i, j)) for _, blk, f in extras]
    out_specs = [pl.BlockSpec(blk, lambda j, i, f=f: f(i, j)) for _, _, blk, f in outs]
    out_shape = [jax.ShapeDtypeStruct(sh, dt) for sh, dt, _, _ in outs]
    body = functools.partial(_ws_body, nx=len(xs), nwt=len(wts), ne=len(extras), no=len(outs),
                             pairs=pairs, tn=tn, nj=nj, epilogue=epilogue)
    return pl.pallas_call(
        body, grid=(nj, m // tm), in_specs=in_specs, out_specs=out_specs, out_shape=out_shape,
        scratch_shapes=[pltpu.VMEM((len(pairs), kdim, tn), F32), pltpu.VMEM((len(pairs), kdim, tn), BF16),
                        pltpu.SemaphoreType.DMA((len(pairs),))],
        compiler_params=_params(("arbitrary", "arbitrary")),
    )(*xs, *wts, *[a for a, _, _ in extras])


def _swiglu_epilogue(accs, e_refs, o_refs):
    g, u = accs
    o_refs[0][...] = (g * _sigmoid(g) * u).astype(BF16)


def _resid_mm(a, w, lay, x2d, gate, coef, seq, *, tm, tn):
    m, n = x2d.shape
    per_b = seq // tm
    return _ws_mm([a], [w], [(0, 0, (lay,), 0)],
                  [(x2d, (tm, tn), lambda i, j: (i, j)),
                   (gate.reshape(-1, 1, n), (1, 1, tn), lambda i, j: (i // per_b, 0, j))],
                  [((m, n), F32, (tm, tn), lambda i, j: (i, j))],
                  functools.partial(_resid_epilogue, coef=coef), tm=tm, tn=tn, nj=n // tn)[0]


def _swiglu_mm(h2, w_in, lay, *, tm=1024, tn=512):
    m, _ = h2.shape
    return _ws_mm([h2], [w_in], [(0, 0, (lay,), 0), (0, 0, (lay,), D_FF)], [],
                  [((m, D_FF), BF16, (tm, tn), lambda i, j: (i, j))],
                  _swiglu_epilogue, tm=tm, tn=tn, nj=D_FF // tn)[0]


def _ffn(x, h, w_in, w_out, lay, gate):
    b, s, d = x.shape
    m = b * s
    a = _swiglu_mm(h.reshape(m, d), w_in, lay)
    y = _resid_mm(a, w_out, lay, x.reshape(m, d), gate, 0.5, s, tm=512, tn=512)
    return y.reshape(b, s, d)


def _rope_tab_kernel(p_ref, inv_ref, cos_ref, sina_ref, sinb_ref):
    ang = p_ref[...].astype(F32) * inv_ref[...]
    lane = lax.broadcasted_iota(jnp.int32, ang.shape, 1)
    half = MLA_ROPE // 2
    c = jnp.cos(ang)
    s = jnp.sin(ang)
    cos_ref[...] = jnp.where(lane < MLA_ROPE, c, 0.0)
    sina_ref[...] = jnp.where(lane < half, -s, 0.0)
    sinb_ref[...] = jnp.where(lane >= half, jnp.where(lane < MLA_ROPE, s, 0.0), 0.0)


def _rope_tables(positions, ts=512):
    m = positions.size
    half = MLA_ROPE // 2
    inv = ROPE_THETA ** (-jnp.arange(0, MLA_ROPE, 2, dtype=F32) / MLA_ROPE)
    inv128 = jnp.concatenate([inv, inv, jnp.zeros((LANES - 2 * half,), F32)]).reshape(1, LANES)
    spec = pl.BlockSpec((ts, LANES), lambda i: (i, 0))
    shp = jax.ShapeDtypeStruct((m, LANES), F32)
    return pl.pallas_call(
        _rope_tab_kernel, grid=(m // ts,),
        in_specs=[pl.BlockSpec((ts, 1), lambda i: (i, 0)), pl.BlockSpec((1, LANES), lambda i: (0, 0))],
        out_specs=[spec, spec, spec], out_shape=[shp, shp, shp],
        compiler_params=_params(("parallel",)),
    )(positions.reshape(m, 1), inv128)


def _plain_epilogue(accs, e_refs, o_refs):
    o_refs[0][...] = accs[0].astype(o_refs[0].dtype)


def _sigmoid_epilogue(accs, e_refs, o_refs):
    o_refs[0][...] = _sigmoid(accs[0]).astype(o_refs[0].dtype)


def _rmsw_epilogue(accs, e_refs, o_refs):
    a = accs[0]
    y = a * lax.rsqrt(jnp.mean(a * a, axis=-1, keepdims=True) + EPS) * e_refs[0][0]
    o_refs[0][...] = y.astype(o_refs[0].dtype)


def _kvlatent_epilogue(accs, e_refs, o_refs):
    a = accs[0]
    w_ref, cos_ref, sina_ref, sinb_ref = e_refs
    ckv = a[:, :MLA_KV_RANK]
    y = ckv * lax.rsqrt(jnp.mean(ckv * ckv, axis=-1, keepdims=True) + EPS) * w_ref[0]
    o_refs[0][...] = y.astype(BF16)
    kr = a[:, MLA_KV_RANK:MLA_KV_RANK + LANES]
    o_refs[1][...] = _rope128(kr, cos_ref[...], sina_ref[...], sinb_ref[...]).astype(BF16)
    o_refs[2][...] = a[:, MLA_KV_RANK + LANES:]


def _qrope_epilogue(accs, e_refs, o_refs, *, heads_per_tile):
    a = accs[0] * Q_SCALE_LOG2
    cos_ref, sina_ref, sinb_ref = e_refs
    cos, sina, sinb = cos_ref[...], sina_ref[...], sinb_ref[...]
    for c in range(heads_per_tile):
        sl = slice(c * LANES, (c + 1) * LANES)
        o_refs[0][:, sl] = _rope128(a[:, sl], cos, sina, sinb).astype(BF16)


def _scaled_epilogue(accs, e_refs, o_refs, *, scale):
    o_refs[0][...] = (accs[0] * scale).astype(o_refs[0].dtype)


def _band(ts, lo, hi, first_tile):
    t = lax.broadcasted_iota(jnp.int32, (ts, HALO + ts), 0)
    s = lax.broadcasted_iota(jnp.int32, (ts, HALO + ts), 1)
    d = t + HALO - s
    ok = jnp.where(d >= lo, jnp.where(d < hi, 1.0, 0.0), 0.0)
    ok = jnp.where(s < HALO, jnp.where(first_tile, 0.0, ok), ok)
    return ok.astype(BF16)


def _pool_kernel(u_ref, halo_ref, pw_ref, ps_ref, o_ref, *, ts):
    i = pl.program_id(1)
    u = u_ref[0]
    ucat = jnp.concatenate([halo_ref[0], u], axis=0)
    tg = i * ts + lax.broadcasted_iota(jnp.int32, (ts, 1), 0)
    for g, w in enumerate(POOL_WINDOWS):
        sl = slice(g * POOL_GW, (g + 1) * POOL_GW)
        win = jnp.dot(_band(ts, 0, w, i == 0), ucat[:, sl], preferred_element_type=F32)
        cnt = jnp.minimum(tg + 1, w).astype(F32)
        p = win / cnt - u[:, sl].astype(F32)
        y = jnp.dot(p.astype(BF16), pw_ref[0, g], preferred_element_type=F32)
        o_ref[0, :, sl] = (y * ps_ref[0, :, sl]).astype(BF16)


def _pool(zp, pool_w, pool_scale, lay, b, s, ts=256):
    hb = ts // HALO
    ng = len(POOL_WINDOWS)
    return pl.pallas_call(
        functools.partial(_pool_kernel, ts=ts), grid=(b, s // ts),
        in_specs=[pl.BlockSpec((1, ts, MIX_W), lambda bi, i: (bi, i, 0)),
                  pl.BlockSpec((1, HALO, MIX_W), lambda bi, i: (bi, jnp.maximum(i * hb - 1, 0), 0)),
                  pl.BlockSpec((1, ng, POOL_GW, POOL_GW), lambda bi, i: (lay, 0, 0, 0)),
                  pl.BlockSpec((1, 1, MIX_W), lambda bi, i: (lay, 0, 0))],
        out_specs=pl.BlockSpec((1, ts, MIX_W), lambda bi, i: (bi, i, 0)),
        out_shape=jax.ShapeDtypeStruct((b, s, MIX_W), BF16),
        compiler_params=_params(("parallel", "parallel")),
    )(zp, zp, pool_w, pool_scale.reshape(DEPTH, 1, MIX_W))


def _conv_kernel(x_ref, halo_ref, w_ref, b_ref, sc_ref, o_ref, *, ts):
    i = pl.program_id(1)
    x = x_ref[0]
    xcat = jnp.concatenate([halo_ref[0], x], axis=0)
    w = w_ref[0]
    acc = x.astype(F32) * w[CONV_W - 1:CONV_W, :] + b_ref[0]
    for d in range(1, CONV_W):
        xs = jnp.dot(_band(ts, d, d + 1, i == 0), xcat, preferred_element_type=F32)
        acc = acc + xs * w[CONV_W - 1 - d:CONV_W - d, :]
    o_ref[0] = (acc * jax.nn.sigmoid(acc) * sc_ref[...]).astype(BF16)


def _conv_silu(zp, conv_w, conv_b, lay, b, s, ts=256):
    c = 2 * ML_HEADS * ML_DK
    hb = ts // HALO
    post = jnp.concatenate([jnp.full((c // 2,), ML_DK ** -0.5, F32), jnp.ones((c // 2,), F32)])
    return pl.pallas_call(
        functools.partial(_conv_kernel, ts=ts), grid=(b, s // ts),
        in_specs=[pl.BlockSpec((1, ts, c), lambda bi, i: (bi, i, 1)),
                  pl.BlockSpec((1, HALO, c), lambda bi, i: (bi, jnp.maximum(i * hb - 1, 0), 1)),
                  pl.BlockSpec((1, CONV_W, c), lambda bi, i: (lay, 0, 0)),
                  pl.BlockSpec((1, 1, c), lambda bi, i: (lay, 0, 0)),
                  pl.BlockSpec((1, c), lambda bi, i: (0, 0))],
        out_specs=pl.BlockSpec((1, ts, c), lambda bi, i: (bi, i, 0)),
        out_shape=jax.ShapeDtypeStruct((b, s, c), BF16),
        compiler_params=_params(("parallel", "parallel")),
    )(zp, zp, conv_w, conv_b.reshape(DEPTH, 1, c), post.reshape(1, c))


ATTN_HEADS_PER_STEP = 8


def _attn_kernel(qn_ref, qr_ref, kn_ref, kr_ref, v_ref, pq_ref, pk_ref, o_ref, m_sc, l_sc, acc_sc, *, tq):
    i = pl.program_id(2)
    nt = (((1,), (1,)), ((), ()))
    tn = (((0,), (0,)), ((), ()))
    hs = [slice(g * LANES, (g + 1) * LANES) for g in range(ATTN_HEADS_PER_STEP)]

    def block(start, mask, first):
        kr = kr_ref[0, pl.ds(start, tq), :]

        def scores(g):
            sl = hs[g]
            q = jnp.concatenate([qn_ref[0, :, sl], qr_ref[0, :, sl]], axis=-1)
            k = jnp.concatenate([kn_ref[0, pl.ds(start, tq), sl], kr], axis=-1)
            st = lax.dot_general(k, q, nt, preferred_element_type=F32)
            if mask is not None:
                st = jnp.where(mask, st, -jnp.inf)
            return st

        st_next = scores(0)
        for g, sl in enumerate(hs):
            st = st_next
            if g + 1 < len(hs):
                st_next = scores(g + 1)
            v = v_ref[0, pl.ds(start, tq), sl]
            smax = jnp.max(st, axis=0, keepdims=True)
            if first:
                m_new = smax
                p = jnp.exp2(st - m_new)
                l_sc[g] = jnp.sum(p, axis=0, keepdims=True)
                acc_sc[g] = lax.dot_general(v, p.astype(BF16), tn, preferred_element_type=F32)
            else:
                m_old = m_sc[g]
                m_new = jnp.maximum(m_old, smax)
                alpha = jnp.exp2(m_old - m_new)
                p = jnp.exp2(st - m_new)
                l_sc[g] = alpha * l_sc[g] + jnp.sum(p, axis=0, keepdims=True)
                acc_sc[g] = alpha * acc_sc[g] + lax.dot_general(v, p.astype(BF16), tn,
                                                                preferred_element_type=F32)
            m_sc[g] = m_new

    sh = CHUNK.bit_length() - 1
    mask = lax.shift_right_arithmetic(pk_ref[0], sh) <= lax.shift_right_arithmetic(pq_ref[0], sh)
    block(pl.multiple_of(i * tq, tq), mask, True)

    def body(j, carry):
        block(pl.multiple_of(j * tq, tq), None, False)
        return carry

    lax.fori_loop(0, i, body, 0)
    for g, sl in enumerate(hs):
        o_ref[0, :, sl] = (acc_sc[g] / l_sc[g]).T.astype(o_ref.dtype)


def _attention(q, kv, kr, positions, tq=512):
    b, s, _ = q.shape
    G = ATTN_HEADS_PER_STEP
    w = G * LANES
    ng = MLA_HEADS // G
    return pl.pallas_call(
        functools.partial(_attn_kernel, tq=tq), grid=(b, ng, s // tq),
        in_specs=[pl.BlockSpec((1, tq, w), lambda bi, hi, i: (bi, i, hi)),
                  pl.BlockSpec((1, tq, w), lambda bi, hi, i: (bi, i, ng + hi)),
                  pl.BlockSpec((1, s, w), lambda bi, hi, i: (bi, 0, hi)),
                  pl.BlockSpec((1, s, LANES), lambda bi, hi, i: (bi, 0, 0)),
                  pl.BlockSpec((1, s, w), lambda bi, hi, i: (bi, 0, ng + hi)),
                  pl.BlockSpec((1, 1, tq), lambda bi, hi, i: (bi, 0, i)),
                  pl.BlockSpec((1, tq, 1), lambda bi, hi, i: (bi, i, 0))],
        out_specs=pl.BlockSpec((1, tq, w), lambda bi, hi, i: (bi, i, hi)),
        out_shape=jax.ShapeDtypeStruct((b, s, MLA_HEADS * MLA_V), BF16),
        scratch_shapes=[pltpu.VMEM((G, 1, tq), F32), pltpu.VMEM((G, 1, tq), F32),
                        pltpu.VMEM((G, MLA_V, tq), F32)],
        compiler_params=_params(("parallel", "parallel", "arbitrary")),
    )(q, q, kv, kr, kv, positions.reshape(b, 1, s), positions.reshape(b, s, 1))


def _split3_dot(a_bf16, x):
    hi = x.astype(BF16)
    r1 = x - hi.astype(F32)
    mid = r1.astype(BF16)
    lo = (r1 - mid.astype(F32)).astype(BF16)
    return (jnp.dot(a_bf16, hi, preferred_element_type=F32)
            + jnp.dot(a_bf16, mid, preferred_element_type=F32)
            + jnp.dot(a_bf16, lo, preferred_element_type=F32))


def _mlstm_kernel(qk_ref, v_ref, g_ref, og_ref, gb_ref, hn_ref, o_ref, ct_ref, m_ref):
    cidx = pl.program_id(1)

    @pl.when(cidx == 0)
    def _():
        ct_ref[...] = jnp.zeros_like(ct_ref)
        m_ref[...] = jnp.zeros_like(m_ref)

    L = CHUNK
    g = g_ref[0]
    gb = gb_ref[...]
    logi = g[:, :LANES] + gb[:, :LANES]
    xf = g[:, LANES:] + gb[:, LANES:]
    logf = jnp.minimum(xf, 0.0) - jnp.log1p(jnp.exp(-jnp.abs(xf)))
    row = lax.broadcasted_iota(jnp.int32, (L, L), 0)
    col = lax.broadcasted_iota(jnp.int32, (L, L), 1)
    tril = row >= col
    bcum = _split3_dot(jnp.where(tril, 1.0, 0.0).astype(BF16), logf)
    x = logi - bcum
    xt = x.T
    m_row = m_ref[...]
    b_last = bcum[L - 1:L, :]
    m_new = jnp.maximum(b_last + m_row, b_last + jnp.max(x, axis=0, keepdims=True))
    decay = jnp.exp(b_last + m_row - m_new)
    ws_all = jnp.exp(b_last + x - m_new)
    inter_all = bcum + m_row
    ones_col = jnp.ones((L, LANES), BF16)
    nt = (((1,), (1,)), ((), ()))
    tn = (((0,), (0,)), ((), ()))

    def early(h):
        q = qk_ref[0, :, h * ML_DK:(h + 1) * ML_DK]
        k = qk_ref[0, :, (ML_HEADS + h) * ML_DK:(ML_HEADS + h + 1) * ML_DK]
        ct = ct_ref[h]
        qk = lax.dot_general(q, k, nt, preferred_element_type=F32)
        qc = jnp.dot(q, ct.astype(BF16), preferred_element_type=F32)
        dlog = jnp.where(tril, bcum[:, h:h + 1] + xt[h:h + 1, :], -jnp.inf)
        inter = inter_all[:, h:h + 1]
        mt = jnp.maximum(inter, jnp.max(dlog, axis=-1, keepdims=True))
        return k, ct, qk, qc, mt, jnp.exp(inter - mt), jnp.exp(dlog - mt)

    ahead = early(0)
    hh_all = []
    for h in range(ML_HEADS):
        k, ct, qk, qc, mt, w_inter, e = ahead
        if h + 1 < ML_HEADS:
            ahead = early(h + 1)
        vaug = jnp.concatenate([v_ref[0, :, h * ML_DV:(h + 1) * ML_DV], ones_col], axis=-1)
        num = jnp.dot((qk * e).astype(BF16), vaug, preferred_element_type=F32) + w_inter * qc
        den = jnp.maximum(jnp.abs(num[:, ML_DV:]), jnp.exp(-mt))
        hh_all.append(num[:, :ML_DV] / jnp.concatenate([den, den], axis=-1))
        wv = (ws_all[:, h:h + 1] * vaug.astype(F32)).astype(BF16)
        ct_ref[h] = decay[:, h:h + 1] * ct + lax.dot_general(k, wv, tn, preferred_element_type=F32)
    for h, hh in enumerate(hh_all):
        hs = slice(h * ML_DV, (h + 1) * ML_DV)
        hn = hh * lax.rsqrt(jnp.mean(hh * hh, axis=-1, keepdims=True) + EPS) * hn_ref[0, :, hs]
        o_ref[0, :, hs] = (og_ref[0, :, hs].astype(F32) * hn).astype(BF16)
    m_ref[...] = m_new


def _mlstm(qk, zp, gates, zs, gate_b, head_norm, lay):
    b, s, _ = qk.shape
    gb = jnp.zeros((1, 2 * LANES), F32)
    gb = gb.at[0, :ML_HEADS].set(gate_b[:ML_HEADS]).at[0, LANES:LANES + ML_HEADS].set(gate_b[ML_HEADS:])
    blk = lambda col: pl.BlockSpec((1, CHUNK, MIX_W), lambda bi, c, col=col: (bi, c, col))
    return pl.pallas_call(
        _mlstm_kernel, grid=(b, s // CHUNK),
        in_specs=[blk(0), blk(2),
                  pl.BlockSpec((1, CHUNK, 2 * LANES), lambda bi, c: (bi, c, 0)),
                  blk(0),
                  pl.BlockSpec((1, 2 * LANES), lambda bi, c: (0, 0)),
                  pl.BlockSpec((1, 1, MIX_W), lambda bi, c: (lay, 0, 0))],
        out_specs=blk(0),
        out_shape=jax.ShapeDtypeStruct((b, s, MIX_W), BF16),
        scratch_shapes=[pltpu.VMEM((ML_HEADS, ML_DK, ML_DV + LANES), F32), pltpu.VMEM((1, LANES), F32)],
        compiler_params=_params(("parallel", "arbitrary")),
    )(qk, zp, gates, zs, gb, head_norm.reshape(DEPTH, 1, MIX_W))


def _merge_epilogue(accs, e_refs, o_refs):
    acc = e_refs[0][...].astype(F32) * accs[0]
    acc += e_refs[1][...].astype(F32) * accs[1]
    acc += e_refs[2][...].astype(F32) * accs[2]
    o_refs[0][...] = acc.astype(BF16)


def _merge(ya, yb, yc, w_branch, lay, zs, tm=1024, tn=512):
    m = ya.shape[0]
    goff = MIX_W // tn
    gate = lambda k: (zs, (tm, tn), lambda i, j, k=k: (i, goff + k * (D_MODEL // tn) + j))
    return _ws_mm([ya, yb, yc], [w_branch], [(k, 0, (lay, k), 0) for k in range(N_BRANCH)],
                  [gate(k) for k in range(N_BRANCH)],
                  [((m, D_MODEL), BF16, (tm, tn), lambda i, j: (i, j))],
                  _merge_epilogue, tm=tm, tn=tn, nj=D_MODEL // tn)[0]


_REGROUP = ((_OFF[0], _OFF[1], None), (_OFF[4], _OFF[7], None),
            (_OFF[7], _OFF[8], None), (_OFF[10], _OFF[11], None),
            (_OFF[1], _OFF[2], None),
            (_OFF[2], _OFF[3], None), (_OFF[3], _OFF[4], LANES),
            (_OFF[8], _OFF[9], LANES), (_OFF[9], _OFF[10], LANES))
N_REGROUP = sum(p if p else hi - lo for lo, hi, p in _REGROUP)
assert N_REGROUP == COL_LAT + N_LAT


REGROUP_ROWS = 512


def _regroup_kernel(wt_ref, o_ref):
    col = 0
    for lo, hi, padded in _REGROUP:
        n = hi - lo
        if padded:
            x = jnp.concatenate([wt_ref[0, lo:hi, :], jnp.zeros((padded - n, wt_ref.shape[2]), F32)], axis=0)
            o_ref[0, :, col:col + padded] = x.T.astype(BF16)
            col += padded
        else:
            for a in range(0, n, REGROUP_ROWS):
                o_ref[0, :, col + a:col + a + REGROUP_ROWS] = (
                    wt_ref[0, lo + a:lo + a + REGROUP_ROWS, :].T.astype(BF16))
            col += n


def _regroup_mix_in(w, tc=128):
    wt = jnp.swapaxes(w, 1, 2)
    dep, n, k = wt.shape
    return pl.pallas_call(
        _regroup_kernel, grid=(dep, k // tc),
        in_specs=[pl.BlockSpec((1, n, tc), lambda l, i: (l, 0, i))],
        out_specs=pl.BlockSpec((1, tc, N_REGROUP), lambda l, i: (l, i, 0)),
        out_shape=jax.ShapeDtypeStruct((dep, k, N_REGROUP), BF16),
        compiler_params=_params(("parallel", "parallel")),
    )(wt)


def _prep_mixer_weights(w_mix_in, mla_w_uq, mla_w_ukv):
    w_all = _regroup_mix_in(w_mix_in)
    uq = mla_w_uq.reshape(DEPTH, MLA_Q_RANK, MLA_HEADS, MLA_NOPE + MLA_ROPE)
    w_qn = uq[..., :MLA_NOPE].reshape(DEPTH, MLA_Q_RANK, MLA_HEADS * MLA_NOPE)
    w_qr = jnp.pad(uq[..., MLA_NOPE:], ((0, 0), (0, 0), (0, 0), (0, LANES - MLA_ROPE)))
    w_q = jnp.concatenate([w_qn, w_qr.reshape(DEPTH, MLA_Q_RANK, MLA_HEADS * LANES)], axis=-1).astype(BF16)
    ukv = mla_w_ukv.reshape(DEPTH, MLA_KV_RANK, MLA_HEADS, 2, MLA_NOPE)
    w_kv = ukv.transpose(0, 1, 3, 2, 4).reshape(DEPTH, MLA_KV_RANK, 2 * MLA_HEADS * MLA_NOPE).astype(BF16)
    return w_all, w_q, w_kv


def _mixer(h, lay, positions, rope_tabs, w_all, w_q, w_kv, pool_w, pool_scale, mla_q_norm, mla_kv_norm,
           ml_conv_w, ml_conv_b, ml_gate_b, ml_head_norm, w_branch, tm=1024):
    b, s, d = h.shape
    m = b * s
    h2 = h.reshape(m, d)
    cos, sina, sinb = rope_tabs
    tab = lambda a: (a, (tm, LANES), lambda i, j: (i, 0))
    tile = lambda i, j: (i, j)
    layvec = lambda a: (a.reshape(DEPTH, 1, -1), (1, 1, a.shape[-1]), lambda i, j: (lay, 0, 0))

    wide = 1024
    zp = _mm(h2, [(w_all, lay, COL_PLAIN)], [], [((m, N_PLAIN), BF16, (tm, wide), tile)],
             _plain_epilogue, tm=tm, tn=wide, nj=N_PLAIN // wide)[0]
    zs = _mm(h2, [(w_all, lay, COL_SIG)], [], [((m, N_SIG), BF16, (tm, wide), tile)],
             _sigmoid_epilogue, tm=tm, tn=wide, nj=N_SIG // wide)[0]
    cqn = _mm(h2, [(w_all, lay, COL_CQ)], [layvec(mla_q_norm)],
              [((m, MLA_Q_RANK), BF16, (tm, MLA_Q_RANK), tile)],
              _rmsw_epilogue, tm=tm, tn=MLA_Q_RANK, nj=1)[0]
    ckvn, kr, gates = _mm(
        h2, [(w_all, lay, COL_LAT)], [layvec(mla_kv_norm), tab(cos), tab(sina), tab(sinb)],
        [((m, MLA_KV_RANK), BF16, (tm, MLA_KV_RANK), tile),
         ((m, LANES), BF16, (tm, LANES), tile),
         ((m, 2 * LANES), F32, (tm, 2 * LANES), tile)],
        _kvlatent_epilogue, tm=tm, tn=N_LAT, nj=1)

    zp3 = zp.reshape(b, s, N_PLAIN)
    ya = _pool(zp3, pool_w, pool_scale, lay, b, s)

    nq = MLA_HEADS * LANES
    q = _mm(cqn, [(w_q, lay, 0)], [tab(cos), tab(sina), tab(sinb)], [((m, 2 * nq), BF16, (tm, nq), tile)],
            functools.partial(_q_epilogue, heads_per_tile=MLA_HEADS, n_nope_tiles=1),
            tm=tm, tn=nq, nj=2)[0]
    kv = _mm(ckvn, [(w_kv, lay, 0)], [], [((m, 2 * nq), BF16, (tm, nq), tile)],
             _plain_epilogue, tm=tm, tn=nq, nj=2)[0]
    yb = _attention(q.reshape(b, s, 2 * nq), kv.reshape(b, s, 2 * nq), kr.reshape(b, s, LANES), positions)

    qk = _conv_silu(zp3, ml_conv_w, ml_conv_b, lay, b, s)
    yc = _mlstm(qk, zp3, gates.reshape(b, s, 2 * LANES), zs.reshape(b, s, N_SIG), ml_gate_b[lay],
                ml_head_norm, lay)

    return _merge(ya.reshape(m, MIX_W), yb.reshape(m, MIX_W), yc.reshape(m, MIX_W), w_branch, lay, zs)


def _q_epilogue(accs, e_refs, o_refs, *, heads_per_tile, n_nope_tiles):
    j = pl.program_id(1)

    @pl.when(j < n_nope_tiles)
    def _():
        _scaled_epilogue(accs, (), o_refs, scale=Q_SCALE_LOG2)

    @pl.when(j >= n_nope_tiles)
    def _():
        _qrope_epilogue(accs, e_refs, o_refs, heads_per_tile=heads_per_tile)


def kernel(x, c, positions, w_ada, b_ada, ada_table, ffn_a_w_in, ffn_a_w_out, w_mix_in, pool_w, pool_scale, mla_q_norm, mla_w_uq, mla_kv_norm, mla_w_ukv, ml_conv_w, ml_conv_b, ml_gate_b, ml_head_norm, w_branch, w_out, ffn_b_w_in, ffn_b_w_out, final_norm):
    b, s, d = x.shape
    m = b * s
    mod = _ada(c, w_ada, b_ada, ada_table)
    rope_tabs = _rope_tables(positions)
    w_all, w_q, w_kv = _prep_mixer_weights(w_mix_in, mla_w_uq, mla_w_ukv)
    pool_wb = pool_w.astype(BF16)
    for l in range(DEPTH):
        md = mod[l]
        h = _normmod(x, md[:, 0], md[:, 1])
        x = _ffn(x, h, ffn_a_w_in, ffn_a_w_out, l, md[:, 2])
        h = _normmod(x, md[:, 3], md[:, 4])
        merged = _mixer(h, l, positions, rope_tabs, w_all, w_q, w_kv, pool_wb, pool_scale, mla_q_norm,
                        mla_kv_norm, ml_conv_w, ml_conv_b, ml_gate_b, ml_head_norm, w_branch)
        x = _resid_mm(merged, w_out, l, x.reshape(m, d), md[:, 5], 1.0, s, tm=1024, tn=512).reshape(b, s, d)
        h = _normmod(x, md[:, 6], md[:, 7])
        x = _ffn(x, h, ffn_b_w_in, ffn_b_w_out, l, md[:, 8])
    return _finalnorm(x, final_norm)
```

```python
import functools

import numpy as np
import jax
import jax.numpy as jnp
from jax import lax
from jax.experimental import pallas as pl
from jax.experimental.pallas import tpu as pltpu

F32 = jnp.float32
BF16 = jnp.bfloat16

D_MODEL = 4096
DEPTH = 2
CHUNK = 64
EPS = 1e-6
D_FF = 2 * D_MODEL
MIX_W = D_MODEL // 2
N_BRANCH = 3
N_MOD = 9
POOL_WINDOWS = (2, 4, 8, 16)
POOL_GW = MIX_W // len(POOL_WINDOWS)
MLA_NOPE = 128
MLA_ROPE = 64
MLA_V = 128
MLA_HEADS = MIX_W // MLA_V
MLA_Q_RANK = D_MODEL // 4
MLA_KV_RANK = 512
MLA_SCALE = (MLA_NOPE + MLA_ROPE) ** -0.5
Q_SCALE_LOG2 = MLA_SCALE * float(np.log2(np.e))
ROPE_THETA = 10000.0
ML_HEADS = 8
ML_DK = 128
ML_DV = MIX_W // ML_HEADS
CONV_W = 4

_SPLITS = (MIX_W, MLA_Q_RANK, MLA_KV_RANK, MLA_ROPE, ML_HEADS * ML_DK, ML_HEADS * ML_DK,
           ML_HEADS * ML_DV, ML_HEADS * ML_DV, ML_HEADS, ML_HEADS, N_BRANCH * D_MODEL)
_OFF = tuple(int(v) for v in np.cumsum((0,) + _SPLITS))

LANES = 128
HALO = 16
VMEM_LIMIT = 60 * 1024 * 1024

N_PLAIN = 3 * MIX_W
N_SIG = MIX_W + N_BRANCH * D_MODEL
N_LAT = MLA_KV_RANK + 3 * LANES


def _params(sem):
    return pltpu.CompilerParams(dimension_semantics=sem, vmem_limit_bytes=VMEM_LIMIT)


def _mm_body(*refs, nw, ne, no, epilogue):
    x_ref = refs[0]
    w_refs = refs[1:1 + nw]
    e_refs = refs[1 + nw:1 + nw + ne]
    o_refs = refs[1 + nw + ne:1 + nw + ne + no]
    accs = [jnp.dot(x_ref[...], w[0], preferred_element_type=F32) for w in w_refs]
    epilogue(accs, e_refs, o_refs)


def _mm(x, ws, extras, outs, epilogue, *, tm, tn, nj):
    m, kdim = x.shape
    in_specs = [pl.BlockSpec((tm, kdim), lambda i, j: (i, 0))]
    for _, lay, col in ws:
        assert col % tn == 0
        in_specs.append(pl.BlockSpec((1, kdim, tn), lambda i, j, lay=lay, off=col // tn: (lay, 0, off + j)))
    for _, blk, f in extras:
        in_specs.append(pl.BlockSpec(blk, f))
    out_specs = [pl.BlockSpec(blk, f) for _, _, blk, f in outs]
    out_shape = [jax.ShapeDtypeStruct(s, d) for s, d, _, _ in outs]
    body = functools.partial(_mm_body, nw=len(ws), ne=len(extras), no=len(outs), epilogue=epilogue)
    return pl.pallas_call(
        body, grid=(m // tm, nj), in_specs=in_specs, out_specs=out_specs, out_shape=out_shape,
        compiler_params=_params(("parallel", "parallel")),
    )(x, *[w for w, _, _ in ws], *[a for a, _, _ in extras])


def _sigmoid(x):
    return 0.5 * jnp.tanh(0.5 * x) + 0.5


def _rope128(v, cos, sina, sinb):
    return (v * cos + pltpu.roll(v, LANES - MLA_ROPE // 2, 1) * sina
            + pltpu.roll(v, MLA_ROPE // 2, 1) * sinb)


def _ada_kernel(c_ref, w_ref, b_ref, t_ref, o_ref):
    c = c_ref[...]
    s = c * jax.nn.sigmoid(c)
    acc = jnp.dot(s.astype(BF16), w_ref[...].astype(BF16), preferred_element_type=F32) + b_ref[...]
    for l in range(DEPTH):
        o_ref[l] = acc + t_ref[l]


def _ada(c, w_ada, b_ada, ada_table):
    b = c.shape[0]
    rows = 8
    c8 = jnp.zeros((rows, D_MODEL), F32).at[:b].set(c)
    n = N_MOD * D_MODEL
    tn = 1024
    out = pl.pallas_call(
        _ada_kernel, grid=(n // tn,),
        in_specs=[pl.BlockSpec((rows, D_MODEL), lambda j: (0, 0)),
                  pl.BlockSpec((D_MODEL, tn), lambda j: (0, j)),
                  pl.BlockSpec((1, tn), lambda j: (0, j)),
                  pl.BlockSpec((DEPTH, 1, tn), lambda j: (0, 0, j))],
        out_specs=pl.BlockSpec((DEPTH, rows, tn), lambda j: (0, 0, j)),
        out_shape=jax.ShapeDtypeStruct((DEPTH, rows, n), F32),
        compiler_params=_params(("parallel",)),
    )(c8, w_ada, b_ada.reshape(1, n), ada_table.reshape(DEPTH, 1, n))
    return out[:, :b].reshape(DEPTH, b, N_MOD, D_MODEL)


def _normmod_kernel(x_ref, shift_ref, scale_ref, o_ref):
    x = x_ref[0]
    y = x * lax.rsqrt(jnp.mean(x * x, axis=-1, keepdims=True) + EPS)
    o_ref[0] = (y * (1.0 + scale_ref[0]) + shift_ref[0]).astype(o_ref.dtype)


def _finalnorm_kernel(x_ref, w_ref, o_ref):
    x = x_ref[0]
    y = x * lax.rsqrt(jnp.mean(x * x, axis=-1, keepdims=True) + EPS)
    o_ref[0] = y * w_ref[...]


def _normmod(x, shift, scale, ts=512):
    b, s, d = x.shape
    vec = pl.BlockSpec((1, 1, d), lambda bi, i: (bi, 0, 0))
    return pl.pallas_call(
        _normmod_kernel, grid=(b, s // ts),
        in_specs=[pl.BlockSpec((1, ts, d), lambda bi, i: (bi, i, 0)), vec, vec],
        out_specs=pl.BlockSpec((1, ts, d), lambda bi, i: (bi, i, 0)),
        out_shape=jax.ShapeDtypeStruct((b, s, d), BF16),
        compiler_params=_params(("parallel", "parallel")),
    )(x, shift.reshape(b, 1, d), scale.reshape(b, 1, d))


def _finalnorm(x, w, ts=512):
    b, s, d = x.shape
    return pl.pallas_call(
        _finalnorm_kernel, grid=(b, s // ts),
        in_specs=[pl.BlockSpec((1, ts, d), lambda bi, i: (bi, i, 0)),
                  pl.BlockSpec((1, d), lambda bi, i: (0, 0))],
        out_specs=pl.BlockSpec((1, ts, d), lambda bi, i: (bi, i, 0)),
        out_shape=jax.ShapeDtypeStruct((b, s, d), F32),
        compiler_params=_params(("parallel", "parallel")),
    )(x, w.reshape(1, d))


def _resid_epilogue(accs, e_refs, o_refs, *, coef):
    x_ref, g_ref = e_refs
    o_refs[0][...] = x_ref[...] + (coef * g_ref[0]) * accs[0]


def _ws_body(*refs, nx, nwt, ne, no, pairs, tn, nj, epilogue):
    x_refs = refs[:nx]
    w_hbm = refs[nx:nx + nwt]
    e_refs = refs[nx + nwt:nx + nwt + ne]
    o_refs = refs[nx + nwt + ne:nx + nwt + ne + no]
    stage, wbf, sems = refs[nx + nwt + ne + no:]
    j = pl.program_id(0)
    i = pl.program_id(1)

    def tile_copy(jj, p):
        _, wi, lead, col = pairs[p]
        src = w_hbm[wi].at[(*lead, slice(None), pl.ds(pl.multiple_of(col + jj * tn, tn), tn))]
        return pltpu.make_async_copy(src, stage.at[p], sems.at[p])

    @pl.when(i == 0)
    def _():
        @pl.when(j == 0)
        def _():
            for p in range(len(pairs)):
                tile_copy(j, p).start()

        for p in range(len(pairs)):
            tile_copy(j, p).wait()
            wbf[p] = stage[p].astype(BF16)

        @pl.when(j + 1 < nj)
        def _():
            for p in range(len(pairs)):
                tile_copy(j + 1, p).start()

    accs = [jnp.dot(x_refs[xi][...], wbf[p], preferred_element_type=F32)
            for p, (xi, _, _, _) in enumerate(pairs)]
    epilogue(accs, e_refs, o_refs)


def _ws_mm(xs, wts, pairs, extras, outs, epilogue, *, tm, tn, nj):
    m, kdim = xs[0].shape
    assert all(c % tn == 0 for _, _, _, c in pairs)
    in_specs = [pl.BlockSpec((tm, kdim), lambda j, i: (i, 0)) for _ in xs]
    in_specs += [pl.BlockSpec(memory_space=pl.ANY) for _ in wts]
    in_specs += [pl.BlockSpec(blk, lambda j, i, f=f: f(i, j)) for _, blk, f in extras]
    out_specs = [pl.BlockSpec(blk, lambda j, i, f=f: f(i, j)) for _, _, blk, f in outs]
    out_shape = [jax.ShapeDtypeStruct(sh, dt) for sh, dt, _, _ in outs]
    body = functools.partial(_ws_body, nx=len(xs), nwt=len(wts), ne=len(extras), no=len(outs),
                             pairs=pairs, tn=tn, nj=nj, epilogue=epilogue)
    return pl.pallas_call(
        body, grid=(nj, m // tm), in_specs=in_specs, out_specs=out_specs, out_shape=out_shape,
        scratch_shapes=[pltpu.VMEM((len(pairs), kdim, tn), F32), pltpu.VMEM((len(pairs), kdim, tn), BF16),
                        pltpu.SemaphoreType.DMA((len(pairs),))],
        compiler_params=_params(("arbitrary", "arbitrary")),
    )(*xs, *wts, *[a for a, _, _ in extras])


def _wst_body(*refs, ne, no, lay, tn, nj, segs, parts, epilogue):
    x_ref, wt_hbm = refs[0], refs[1]
    e_refs = refs[2:2 + ne]
    o_refs = refs[2 + ne:2 + ne + no]
    stage, wbf, sems = refs[2 + ne + no:]
    j = pl.program_id(0)
    i = pl.program_id(1)

    def copies(jj):
        if parts is not None:
            return [pltpu.make_async_copy(wt_hbm.at[lay, pl.ds(src, n), :], stage.at[pl.ds(dst, n), :],
                                          sems.at[c]) for c, (src, n, dst) in enumerate(parts)]
        row = segs[0][1] + jj * tn
        for first_tile, row0 in segs[1:]:
            row = jnp.where(jj >= first_tile, row0 + (jj - first_tile) * tn, row)
        src = wt_hbm.at[lay, pl.ds(pl.multiple_of(row, 8), tn), :]
        return [pltpu.make_async_copy(src, stage, sems.at[0])]

    @pl.when(i == 0)
    def _():
        @pl.when(j == 0)
        def _():
            if parts is not None:
                covered = sorted((dst, dst + n) for _, n, dst in parts)
                for lo, hi in zip([0] + [b for _, b in covered], [a for a, _ in covered] + [tn]):
                    if hi > lo:
                        stage[lo:hi, :] = jnp.zeros((hi - lo, stage.shape[1]), F32)
            for cp in copies(j):
                cp.start()

        for cp in copies(j):
            cp.wait()
        wbf[...] = stage[...].astype(BF16)

        @pl.when(j + 1 < nj)
        def _():
            for cp in copies(j + 1):
                cp.start()

    acc = lax.dot_general(x_ref[...], wbf[...], (((1,), (1,)), ((), ())), preferred_element_type=F32)
    epilogue([acc], e_refs, o_refs)


def _wst_mm(x, wt, lay, extras, outs, epilogue, *, tm, tn, nj, segs=None, parts=None):
    m, kdim = x.shape
    assert (segs is None) != (parts is None) and (parts is None or nj == 1)
    in_specs = [pl.BlockSpec((tm, kdim), lambda j, i: (i, 0)), pl.BlockSpec(memory_space=pl.ANY)]
    in_specs += [pl.BlockSpec(blk, lambda j, i, f=f: f(i, j)) for _, blk, f in extras]
    out_specs = [pl.BlockSpec(blk, lambda j, i, f=f: f(i, j)) for _, _, blk, f in outs]
    out_shape = [jax.ShapeDtypeStruct(sh, dt) for sh, dt, _, _ in outs]
    body = functools.partial(_wst_body, ne=len(extras), no=len(outs), lay=lay, tn=tn, nj=nj,
                             segs=segs, parts=parts, epilogue=epilogue)
    return pl.pallas_call(
        body, grid=(nj, m // tm), in_specs=in_specs, out_specs=out_specs, out_shape=out_shape,
        scratch_shapes=[pltpu.VMEM((tn, kdim), F32), pltpu.VMEM((tn, kdim), BF16),
                        pltpu.SemaphoreType.DMA((len(parts) if parts else 1,))],
        compiler_params=_params(("arbitrary", "arbitrary")),
    )(x, wt, *[a for a, _, _ in extras])


def _swiglu_epilogue(accs, e_refs, o_refs):
    g, u = accs
    o_refs[0][...] = (g * _sigmoid(g) * u).astype(BF16)


def _resid_mm(a, w, lay, x2d, gate, coef, seq, *, tm, tn):
    m, n = x2d.shape
    per_b = seq // tm
    return _ws_mm([a], [w], [(0, 0, (lay,), 0)],
                  [(x2d, (tm, tn), lambda i, j: (i, j)),
                   (gate.reshape(-1, 1, n), (1, 1, tn), lambda i, j: (i // per_b, 0, j))],
                  [((m, n), F32, (tm, tn), lambda i, j: (i, j))],
                  functools.partial(_resid_epilogue, coef=coef), tm=tm, tn=tn, nj=n // tn)[0]


def _swiglu_mm(h2, w_in, lay, *, tm=1024, tn=512):
    m, _ = h2.shape
    return _ws_mm([h2], [w_in], [(0, 0, (lay,), 0), (0, 0, (lay,), D_FF)], [],
                  [((m, D_FF), BF16, (tm, tn), lambda i, j: (i, j))],
                  _swiglu_epilogue, tm=tm, tn=tn, nj=D_FF // tn)[0]


def _ffn(x, h, w_in, w_out, lay, gate):
    b, s, d = x.shape
    m = b * s
    a = _swiglu_mm(h.reshape(m, d), w_in, lay)
    y = _resid_mm(a, w_out, lay, x.reshape(m, d), gate, 0.5, s, tm=512, tn=512)
    return y.reshape(b, s, d)


def _rope_tab_kernel(p_ref, inv_ref, cos_ref, sina_ref, sinb_ref):
    ang = p_ref[...].astype(F32) * inv_ref[...]
    lane = lax.broadcasted_iota(jnp.int32, ang.shape, 1)
    half = MLA_ROPE // 2
    c = jnp.cos(ang)
    s = jnp.sin(ang)
    cos_ref[...] = jnp.where(lane < MLA_ROPE, c, 0.0)
    sina_ref[...] = jnp.where(lane < half, -s, 0.0)
    sinb_ref[...] = jnp.where(lane >= half, jnp.where(lane < MLA_ROPE, s, 0.0), 0.0)


def _rope_tables(positions, ts=512):
    m = positions.size
    half = MLA_ROPE // 2
    inv = ROPE_THETA ** (-jnp.arange(0, MLA_ROPE, 2, dtype=F32) / MLA_ROPE)
    inv128 = jnp.concatenate([inv, inv, jnp.zeros((LANES - 2 * half,), F32)]).reshape(1, LANES)
    spec = pl.BlockSpec((ts, LANES), lambda i: (i, 0))
    shp = jax.ShapeDtypeStruct((m, LANES), F32)
    return pl.pallas_call(
        _rope_tab_kernel, grid=(m // ts,),
        in_specs=[pl.BlockSpec((ts, 1), lambda i: (i, 0)), pl.BlockSpec((1, LANES), lambda i: (0, 0))],
        out_specs=[spec, spec, spec], out_shape=[shp, shp, shp],
        compiler_params=_params(("parallel",)),
    )(positions.reshape(m, 1), inv128)


def _plain_epilogue(accs, e_refs, o_refs):
    o_refs[0][...] = accs[0].astype(o_refs[0].dtype)


def _sigmoid_epilogue(accs, e_refs, o_refs):
    o_refs[0][...] = _sigmoid(accs[0]).astype(o_refs[0].dtype)


def _rmsw_epilogue(accs, e_refs, o_refs):
    a = accs[0]
    y = a * lax.rsqrt(jnp.mean(a * a, axis=-1, keepdims=True) + EPS) * e_refs[0][0]
    o_refs[0][...] = y.astype(o_refs[0].dtype)


def _kvlatent_epilogue(accs, e_refs, o_refs):
    a = accs[0]
    w_ref, cos_ref, sina_ref, sinb_ref = e_refs
    ckv = a[:, :MLA_KV_RANK]
    y = ckv * lax.rsqrt(jnp.mean(ckv * ckv, axis=-1, keepdims=True) + EPS) * w_ref[0]
    o_refs[0][...] = y.astype(BF16)
    kr = a[:, MLA_KV_RANK:MLA_KV_RANK + LANES]
    o_refs[1][...] = _rope128(kr, cos_ref[...], sina_ref[...], sinb_ref[...]).astype(BF16)
    o_refs[2][...] = a[:, MLA_KV_RANK + LANES:]


def _qrope_epilogue(accs, e_refs, o_refs, *, heads_per_tile):
    a = accs[0] * Q_SCALE_LOG2
    cos_ref, sina_ref, sinb_ref = e_refs
    cos, sina, sinb = cos_ref[...], sina_ref[...], sinb_ref[...]
    for c in range(heads_per_tile):
        sl = slice(c * LANES, (c + 1) * LANES)
        o_refs[0][:, sl] = _rope128(a[:, sl], cos, sina, sinb).astype(BF16)


def _scaled_epilogue(accs, e_refs, o_refs, *, scale):
    o_refs[0][...] = (accs[0] * scale).astype(o_refs[0].dtype)


def _band(ts, lo, hi, first_tile):
    t = lax.broadcasted_iota(jnp.int32, (ts, HALO + ts), 0)
    s = lax.broadcasted_iota(jnp.int32, (ts, HALO + ts), 1)
    d = t + HALO - s
    ok = jnp.where(d >= lo, jnp.where(d < hi, 1.0, 0.0), 0.0)
    ok = jnp.where(s < HALO, jnp.where(first_tile, 0.0, ok), ok)
    return ok.astype(BF16)


def _pool_kernel(u_ref, halo_ref, pw_ref, ps_ref, o_ref, *, ts):
    i = pl.program_id(1)
    u = u_ref[0]
    ucat = jnp.concatenate([halo_ref[0], u], axis=0)
    tg = i * ts + lax.broadcasted_iota(jnp.int32, (ts, 1), 0)
    for g, w in enumerate(POOL_WINDOWS):
        sl = slice(g * POOL_GW, (g + 1) * POOL_GW)
        win = jnp.dot(_band(ts, 0, w, i == 0), ucat[:, sl], preferred_element_type=F32)
        cnt = jnp.minimum(tg + 1, w).astype(F32)
        p = win / cnt - u[:, sl].astype(F32)
        y = jnp.dot(p.astype(BF16), pw_ref[0, g], preferred_element_type=F32)
        o_ref[0, :, sl] = (y * ps_ref[0, :, sl]).astype(BF16)


def _pool(zp, pool_w, pool_scale, lay, b, s, ts=256):
    hb = ts // HALO
    ng = len(POOL_WINDOWS)
    return pl.pallas_call(
        functools.partial(_pool_kernel, ts=ts), grid=(b, s // ts),
        in_specs=[pl.BlockSpec((1, ts, MIX_W), lambda bi, i: (bi, i, 0)),
                  pl.BlockSpec((1, HALO, MIX_W), lambda bi, i: (bi, jnp.maximum(i * hb - 1, 0), 0)),
                  pl.BlockSpec((1, ng, POOL_GW, POOL_GW), lambda bi, i: (lay, 0, 0, 0)),
                  pl.BlockSpec((1, 1, MIX_W), lambda bi, i: (lay, 0, 0))],
        out_specs=pl.BlockSpec((1, ts, MIX_W), lambda bi, i: (bi, i, 0)),
        out_shape=jax.ShapeDtypeStruct((b, s, MIX_W), BF16),
        compiler_params=_params(("parallel", "parallel")),
    )(zp, zp, pool_w, pool_scale.reshape(DEPTH, 1, MIX_W))


def _conv_kernel(x_ref, halo_ref, w_ref, b_ref, sc_ref, o_ref, *, ts):
    i = pl.program_id(1)
    x = x_ref[0]
    xcat = jnp.concatenate([halo_ref[0], x], axis=0)
    w = w_ref[0]
    acc = x.astype(F32) * w[CONV_W - 1:CONV_W, :] + b_ref[0]
    for d in range(1, CONV_W):
        xs = jnp.dot(_band(ts, d, d + 1, i == 0), xcat, preferred_element_type=F32)
        acc = acc + xs * w[CONV_W - 1 - d:CONV_W - d, :]
    o_ref[0] = (acc * jax.nn.sigmoid(acc) * sc_ref[...]).astype(BF16)


def _conv_silu(zp, conv_w, conv_b, lay, b, s, ts=256):
    c = 2 * ML_HEADS * ML_DK
    hb = ts // HALO
    post = jnp.concatenate([jnp.full((c // 2,), ML_DK ** -0.5, F32), jnp.ones((c // 2,), F32)])
    return pl.pallas_call(
        functools.partial(_conv_kernel, ts=ts), grid=(b, s // ts),
        in_specs=[pl.BlockSpec((1, ts, c), lambda bi, i: (bi, i, 1)),
                  pl.BlockSpec((1, HALO, c), lambda bi, i: (bi, jnp.maximum(i * hb - 1, 0), 1)),
                  pl.BlockSpec((1, CONV_W, c), lambda bi, i: (lay, 0, 0)),
                  pl.BlockSpec((1, 1, c), lambda bi, i: (lay, 0, 0)),
                  pl.BlockSpec((1, c), lambda bi, i: (0, 0))],
        out_specs=pl.BlockSpec((1, ts, c), lambda bi, i: (bi, i, 0)),
        out_shape=jax.ShapeDtypeStruct((b, s, c), BF16),
        compiler_params=_params(("parallel", "parallel")),
    )(zp, zp, conv_w, conv_b.reshape(DEPTH, 1, c), post.reshape(1, c))


ATTN_HEADS_PER_STEP = 8


def _attn_kernel(qn_ref, qr_ref, kn_ref, kr_ref, v_ref, pq_ref, pk_ref, o_ref, m_sc, l_sc, acc_sc, *, tq):
    i = pl.program_id(2)
    nt = (((1,), (1,)), ((), ()))
    tn = (((0,), (0,)), ((), ()))
    hs = [slice(g * LANES, (g + 1) * LANES) for g in range(ATTN_HEADS_PER_STEP)]

    def block(start, mask, first):
        kr = kr_ref[0, pl.ds(start, tq), :]

        def scores(g):
            sl = hs[g]
            q = jnp.concatenate([qn_ref[0, :, sl], qr_ref[0, :, sl]], axis=-1)
            k = jnp.concatenate([kn_ref[0, pl.ds(start, tq), sl], kr], axis=-1)
            st = lax.dot_general(k, q, nt, preferred_element_type=F32)
            if mask is not None:
                st = jnp.where(mask, st, -jnp.inf)
            return st

        st_next = scores(0)
        for g, sl in enumerate(hs):
            st = st_next
            if g + 1 < len(hs):
                st_next = scores(g + 1)
            v = v_ref[0, pl.ds(start, tq), sl]
            smax = jnp.max(st, axis=0, keepdims=True)
            if first:
                m_new = smax
                p = jnp.exp2(st - m_new)
                l_sc[g] = jnp.sum(p, axis=0, keepdims=True)
                acc_sc[g] = lax.dot_general(v, p.astype(BF16), tn, preferred_element_type=F32)
            else:
                m_old = m_sc[g]
                m_new = jnp.maximum(m_old, smax)
                alpha = jnp.exp2(m_old - m_new)
                p = jnp.exp2(st - m_new)
                l_sc[g] = alpha * l_sc[g] + jnp.sum(p, axis=0, keepdims=True)
                acc_sc[g] = alpha * acc_sc[g] + lax.dot_general(v, p.astype(BF16), tn,
                                                                preferred_element_type=F32)
            m_sc[g] = m_new

    sh = CHUNK.bit_length() - 1
    mask = lax.shift_right_arithmetic(pk_ref[0], sh) <= lax.shift_right_arithmetic(pq_ref[0], sh)
    block(pl.multiple_of(i * tq, tq), mask, True)

    def body(j, carry):
        block(pl.multiple_of(j * tq, tq), None, False)
        return carry

    lax.fori_loop(0, i, body, 0)
    for g, sl in enumerate(hs):
        o_ref[0, :, sl] = (acc_sc[g] / l_sc[g]).T.astype(o_ref.dtype)


def _attention(q, kv, kr, positions, tq=512):
    b, s, _ = q.shape
    G = ATTN_HEADS_PER_STEP
    w = G * LANES
    ng = MLA_HEADS // G
    return pl.pallas_call(
        functools.partial(_attn_kernel, tq=tq), grid=(b, ng, s // tq),
        in_specs=[pl.BlockSpec((1, tq, w), lambda bi, hi, i: (bi, i, hi)),
                  pl.BlockSpec((1, tq, w), lambda bi, hi, i: (bi, i, ng + hi)),
                  pl.BlockSpec((1, s, w), lambda bi, hi, i: (bi, 0, hi)),
                  pl.BlockSpec((1, s, LANES), lambda bi, hi, i: (bi, 0, 0)),
                  pl.BlockSpec((1, s, w), lambda bi, hi, i: (bi, 0, ng + hi)),
                  pl.BlockSpec((1, 1, tq), lambda bi, hi, i: (bi, 0, i)),
                  pl.BlockSpec((1, tq, 1), lambda bi, hi, i: (bi, i, 0))],
        out_specs=pl.BlockSpec((1, tq, w), lambda bi, hi, i: (bi, i, hi)),
        out_shape=jax.ShapeDtypeStruct((b, s, MLA_HEADS * MLA_V), BF16),
        scratch_shapes=[pltpu.VMEM((G, 1, tq), F32), pltpu.VMEM((G, 1, tq), F32),
                        pltpu.VMEM((G, MLA_V, tq), F32)],
        compiler_params=_params(("parallel", "parallel", "arbitrary")),
    )(q, q, kv, kr, kv, positions.reshape(b, 1, s), positions.reshape(b, s, 1))


def _split3_dot(a_bf16, x):
    hi = x.astype(BF16)
    r1 = x - hi.astype(F32)
    mid = r1.astype(BF16)
    lo = (r1 - mid.astype(F32)).astype(BF16)
    return (jnp.dot(a_bf16, hi, preferred_element_type=F32)
            + jnp.dot(a_bf16, mid, preferred_element_type=F32)
            + jnp.dot(a_bf16, lo, preferred_element_type=F32))


def _mlstm_kernel(qk_ref, v_ref, g_ref, og_ref, gb_ref, hn_ref, o_ref, ct_ref, m_ref):
    cidx = pl.program_id(1)

    @pl.when(cidx == 0)
    def _():
        ct_ref[...] = jnp.zeros_like(ct_ref)
        m_ref[...] = jnp.zeros_like(m_ref)

    L = CHUNK
    g = g_ref[0]
    gb = gb_ref[...]
    logi = g[:, :LANES] + gb[:, :LANES]
    xf = g[:, LANES:] + gb[:, LANES:]
    logf = jnp.minimum(xf, 0.0) - jnp.log1p(jnp.exp(-jnp.abs(xf)))
    row = lax.broadcasted_iota(jnp.int32, (L, L), 0)
    col = lax.broadcasted_iota(jnp.int32, (L, L), 1)
    tril = row >= col
    bcum = _split3_dot(jnp.where(tril, 1.0, 0.0).astype(BF16), logf)
    x = logi - bcum
    xt = x.T
    m_row = m_ref[...]
    b_last = bcum[L - 1:L, :]
    m_new = jnp.maximum(b_last + m_row, b_last + jnp.max(x, axis=0, keepdims=True))
    decay = jnp.exp(b_last + m_row - m_new)
    ws_all = jnp.exp(b_last + x - m_new)
    inter_all = bcum + m_row
    ones_col = jnp.ones((L, LANES), BF16)
    nt = (((1,), (1,)), ((), ()))
    tn = (((0,), (0,)), ((), ()))

    def early(h):
        q = qk_ref[0, :, h * ML_DK:(h + 1) * ML_DK]
        k = qk_ref[0, :, (ML_HEADS + h) * ML_DK:(ML_HEADS + h + 1) * ML_DK]
        ct = ct_ref[h]
        qk = lax.dot_general(q, k, nt, preferred_element_type=F32)
        qc = jnp.dot(q, ct.astype(BF16), preferred_element_type=F32)
        dlog = jnp.where(tril, bcum[:, h:h + 1] + xt[h:h + 1, :], -jnp.inf)
        inter = inter_all[:, h:h + 1]
        mt = jnp.maximum(inter, jnp.max(dlog, axis=-1, keepdims=True))
        return k, ct, qk, qc, mt, jnp.exp(inter - mt), jnp.exp(dlog - mt)

    ahead = early(0)
    hh_all = []
    for h in range(ML_HEADS):
        k, ct, qk, qc, mt, w_inter, e = ahead
        if h + 1 < ML_HEADS:
            ahead = early(h + 1)
        vaug = jnp.concatenate([v_ref[0, :, h * ML_DV:(h + 1) * ML_DV], ones_col], axis=-1)
        num = jnp.dot((qk * e).astype(BF16), vaug, preferred_element_type=F32) + w_inter * qc
        den = jnp.maximum(jnp.abs(num[:, ML_DV:]), jnp.exp(-mt))
        hh_all.append(num[:, :ML_DV] / jnp.concatenate([den, den], axis=-1))
        wv = (ws_all[:, h:h + 1] * vaug.astype(F32)).astype(BF16)
        ct_ref[h] = decay[:, h:h + 1] * ct + lax.dot_general(k, wv, tn, preferred_element_type=F32)
    for h, hh in enumerate(hh_all):
        hs = slice(h * ML_DV, (h + 1) * ML_DV)
        hn = hh * lax.rsqrt(jnp.mean(hh * hh, axis=-1, keepdims=True) + EPS) * hn_ref[0, :, hs]
        o_ref[0, :, hs] = (og_ref[0, :, hs].astype(F32) * hn).astype(BF16)
    m_ref[...] = m_new


def _mlstm(qk, zp, gates, zs, gate_b, head_norm, lay):
    b, s, _ = qk.shape
    gb = jnp.zeros((1, 2 * LANES), F32)
    gb = gb.at[0, :ML_HEADS].set(gate_b[:ML_HEADS]).at[0, LANES:LANES + ML_HEADS].set(gate_b[ML_HEADS:])
    blk = lambda col: pl.BlockSpec((1, CHUNK, MIX_W), lambda bi, c, col=col: (bi, c, col))
    return pl.pallas_call(
        _mlstm_kernel, grid=(b, s // CHUNK),
        in_specs=[blk(0), blk(2),
                  pl.BlockSpec((1, CHUNK, 2 * LANES), lambda bi, c: (bi, c, 0)),
                  blk(0),
                  pl.BlockSpec((1, 2 * LANES), lambda bi, c: (0, 0)),
                  pl.BlockSpec((1, 1, MIX_W), lambda bi, c: (lay, 0, 0))],
        out_specs=blk(0),
        out_shape=jax.ShapeDtypeStruct((b, s, MIX_W), BF16),
        scratch_shapes=[pltpu.VMEM((ML_HEADS, ML_DK, ML_DV + LANES), F32), pltpu.VMEM((1, LANES), F32)],
        compiler_params=_params(("parallel", "arbitrary")),
    )(qk, zp, gates, zs, gb, head_norm.reshape(DEPTH, 1, MIX_W))


def _merge_epilogue(accs, e_refs, o_refs):
    acc = e_refs[0][...].astype(F32) * accs[0]
    acc += e_refs[1][...].astype(F32) * accs[1]
    acc += e_refs[2][...].astype(F32) * accs[2]
    o_refs[0][...] = acc.astype(BF16)


def _merge(ya, yb, yc, w_branch, lay, zs, tm=1024, tn=512):
    m = ya.shape[0]
    goff = MIX_W // tn
    gate = lambda k: (zs, (tm, tn), lambda i, j, k=k: (i, goff + k * (D_MODEL // tn) + j))
    return _ws_mm([ya, yb, yc], [w_branch], [(k, 0, (lay, k), 0) for k in range(N_BRANCH)],
                  [gate(k) for k in range(N_BRANCH)],
                  [((m, D_MODEL), BF16, (tm, tn), lambda i, j: (i, j))],
                  _merge_epilogue, tm=tm, tn=tn, nj=D_MODEL // tn)[0]


def _prep_mixer_weights(w_mix_in, mla_w_uq, mla_w_ukv):
    w_t = jnp.swapaxes(w_mix_in, 1, 2)
    uq = mla_w_uq.reshape(DEPTH, MLA_Q_RANK, MLA_HEADS, MLA_NOPE + MLA_ROPE)
    w_qn = uq[..., :MLA_NOPE].reshape(DEPTH, MLA_Q_RANK, MLA_HEADS * MLA_NOPE)
    w_qr = jnp.pad(uq[..., MLA_NOPE:], ((0, 0), (0, 0), (0, 0), (0, LANES - MLA_ROPE)))
    w_q = jnp.concatenate([w_qn, w_qr.reshape(DEPTH, MLA_Q_RANK, MLA_HEADS * LANES)], axis=-1).astype(BF16)
    ukv = mla_w_ukv.reshape(DEPTH, MLA_KV_RANK, MLA_HEADS, 2, MLA_NOPE)
    w_kv = ukv.transpose(0, 1, 3, 2, 4).reshape(DEPTH, MLA_KV_RANK, 2 * MLA_HEADS * MLA_NOPE).astype(BF16)
    return w_t, w_q, w_kv


def _mixer(h, lay, positions, rope_tabs, w_t, w_q, w_kv, pool_w, pool_scale, mla_q_norm, mla_kv_norm,
           ml_conv_w, ml_conv_b, ml_gate_b, ml_head_norm, w_branch, tm=1024):
    b, s, d = h.shape
    m = b * s
    h2 = h.reshape(m, d)
    cos, sina, sinb = rope_tabs
    tab = lambda a: (a, (tm, LANES), lambda i, j: (i, 0))
    tile = lambda i, j: (i, j)
    layvec = lambda a: (a.reshape(DEPTH, 1, -1), (1, 1, a.shape[-1]), lambda i, j: (lay, 0, 0))

    o = _OFF
    wide = 1024
    zp = _wst_mm(h2, w_t, lay, [], [((m, N_PLAIN), BF16, (tm, wide), tile)], _plain_epilogue,
                 tm=tm, tn=wide, nj=N_PLAIN // wide, segs=((0, o[0]), (MIX_W // wide, o[4])))[0]
    zs = _wst_mm(h2, w_t, lay, [], [((m, N_SIG), BF16, (tm, wide), tile)], _sigmoid_epilogue,
                 tm=tm, tn=wide, nj=N_SIG // wide, segs=((0, o[7]), (MIX_W // wide, o[10])))[0]
    cqn = _wst_mm(h2, w_t, lay, [layvec(mla_q_norm)], [((m, MLA_Q_RANK), BF16, (tm, MLA_Q_RANK), tile)],
                  _rmsw_epilogue, tm=tm, tn=MLA_Q_RANK, nj=1, segs=((0, o[1]),))[0]
    ckvn, kr, gates = _wst_mm(
        h2, w_t, lay, [layvec(mla_kv_norm), tab(cos), tab(sina), tab(sinb)],
        [((m, MLA_KV_RANK), BF16, (tm, MLA_KV_RANK), tile),
         ((m, LANES), BF16, (tm, LANES), tile),
         ((m, 2 * LANES), F32, (tm, 2 * LANES), tile)],
        _kvlatent_epilogue, tm=tm, tn=N_LAT, nj=1,
        parts=((o[2], MLA_KV_RANK, 0), (o[3], MLA_ROPE, MLA_KV_RANK),
               (o[8], ML_HEADS, MLA_KV_RANK + LANES), (o[9], ML_HEADS, MLA_KV_RANK + 2 * LANES)))

    zp3 = zp.reshape(b, s, N_PLAIN)
    ya = _pool(zp3, pool_w, pool_scale, lay, b, s)

    nq = MLA_HEADS * LANES
    q = _mm(cqn, [(w_q, lay, 0)], [tab(cos), tab(sina), tab(sinb)], [((m, 2 * nq), BF16, (tm, nq), tile)],
            functools.partial(_q_epilogue, heads_per_tile=MLA_HEADS, n_nope_tiles=1),
            tm=tm, tn=nq, nj=2)[0]
    kv = _mm(ckvn, [(w_kv, lay, 0)], [], [((m, 2 * nq), BF16, (tm, nq), tile)],
             _plain_epilogue, tm=tm, tn=nq, nj=2)[0]
    yb = _attention(q.reshape(b, s, 2 * nq), kv.reshape(b, s, 2 * nq), kr.reshape(b, s, LANES), positions)

    qk = _conv_silu(zp3, ml_conv_w, ml_conv_b, lay, b, s)
    yc = _mlstm(qk, zp3, gates.reshape(b, s, 2 * LANES), zs.reshape(b, s, N_SIG), ml_gate_b[lay],
                ml_head_norm, lay)

    return _merge(ya.reshape(m, MIX_W), yb.reshape(m, MIX_W), yc.reshape(m, MIX_W), w_branch, lay, zs)


def _q_epilogue(accs, e_refs, o_refs, *, heads_per_tile, n_nope_tiles):
    j = pl.program_id(1)

    @pl.when(j < n_nope_tiles)
    def _():
        _scaled_epilogue(accs, (), o_refs, scale=Q_SCALE_LOG2)

    @pl.when(j >= n_nope_tiles)
    def _():
        _qrope_epilogue(accs, e_refs, o_refs, heads_per_tile=heads_per_tile)


def kernel(x, c, positions, w_ada, b_ada, ada_table, ffn_a_w_in, ffn_a_w_out, w_mix_in, pool_w, pool_scale, mla_q_norm, mla_w_uq, mla_kv_norm, mla_w_ukv, ml_conv_w, ml_conv_b, ml_gate_b, ml_head_norm, w_branch, w_out, ffn_b_w_in, ffn_b_w_out, final_norm):
    b, s, d = x.shape
    m = b * s
    mod = _ada(c, w_ada, b_ada, ada_table)
    rope_tabs = _rope_tables(positions)
    w_t, w_q, w_kv = _prep_mixer_weights(w_mix_in, mla_w_uq, mla_w_ukv)
    pool_wb = pool_w.astype(BF16)
    for l in range(DEPTH):
        md = mod[l]
        h = _normmod(x, md[:, 0], md[:, 1])
        x = _ffn(x, h, ffn_a_w_in, ffn_a_w_out, l, md[:, 2])
        h = _normmod(x, md[:, 3], md[:, 4])
        merged = _mixer(h, l, positions, rope_tabs, w_t, w_q, w_kv, pool_wb, pool_scale, mla_q_norm,
                        mla_kv_norm, ml_conv_w, ml_conv_b, ml_gate_b, ml_head_norm, w_branch)
        x = _resid_mm(merged, w_out, l, x.reshape(m, d), md[:, 5], 1.0, s, tm=1024, tn=512).reshape(b, s, d)
        h = _normmod(x, md[:, 6], md[:, 7])
        x = _ffn(x, h, ffn_b_w_in, ffn_b_w_out, l, md[:, 8])
    return _finalnorm(x, final_norm)
```

```python
import functools

import numpy as np
import jax
import jax.numpy as jnp
from jax import lax
from jax.experimental import pallas as pl
from jax.experimental.pallas import tpu as pltpu

F32 = jnp.float32
BF16 = jnp.bfloat16

D_MODEL = 4096
DEPTH = 2
CHUNK = 64
EPS = 1e-6
D_FF = 2 * D_MODEL
MIX_W = D_MODEL // 2
N_BRANCH = 3
N_MOD = 9
POOL_WINDOWS = (2, 4, 8, 16)
POOL_GW = MIX_W // len(POOL_WINDOWS)
MLA_NOPE = 128
MLA_ROPE = 64
MLA_V = 128
MLA_HEADS = MIX_W // MLA_V
MLA_Q_RANK = D_MODEL // 4
MLA_KV_RANK = 512
MLA_SCALE = (MLA_NOPE + MLA_ROPE) ** -0.5
Q_SCALE_LOG2 = MLA_SCALE * float(np.log2(np.e))
ROPE_THETA = 10000.0
ML_HEADS = 8
ML_DK = 128
ML_DV = MIX_W // ML_HEADS
CONV_W = 4

_SPLITS = (MIX_W, MLA_Q_RANK, MLA_KV_RANK, MLA_ROPE, ML_HEADS * ML_DK, ML_HEADS * ML_DK,
           ML_HEADS * ML_DV, ML_HEADS * ML_DV, ML_HEADS, ML_HEADS, N_BRANCH * D_MODEL)
_OFF = tuple(int(v) for v in np.cumsum((0,) + _SPLITS))

LANES = 128
HALO = 16
VMEM_LIMIT = 60 * 1024 * 1024

N_PLAIN = 3 * MIX_W
N_SIG = MIX_W + N_BRANCH * D_MODEL
N_LAT = MLA_KV_RANK + 3 * LANES


def _params(sem):
    return pltpu.CompilerParams(dimension_semantics=sem, vmem_limit_bytes=VMEM_LIMIT)


def _mm_body(*refs, nw, ne, no, epilogue):
    x_ref = refs[0]
    w_refs = refs[1:1 + nw]
    e_refs = refs[1 + nw:1 + nw + ne]
    o_refs = refs[1 + nw + ne:1 + nw + ne + no]
    accs = [jnp.dot(x_ref[...], w[0], preferred_element_type=F32) for w in w_refs]
    epilogue(accs, e_refs, o_refs)


def _mm(x, ws, extras, outs, epilogue, *, tm, tn, nj):
    m, kdim = x.shape
    in_specs = [pl.BlockSpec((tm, kdim), lambda i, j: (i, 0))]
    for _, lay, col in ws:
        assert col % tn == 0
        in_specs.append(pl.BlockSpec((1, kdim, tn), lambda i, j, lay=lay, off=col // tn: (lay, 0, off + j)))
    for _, blk, f in extras:
        in_specs.append(pl.BlockSpec(blk, f))
    out_specs = [pl.BlockSpec(blk, f) for _, _, blk, f in outs]
    out_shape = [jax.ShapeDtypeStruct(s, d) for s, d, _, _ in outs]
    body = functools.partial(_mm_body, nw=len(ws), ne=len(extras), no=len(outs), epilogue=epilogue)
    return pl.pallas_call(
        body, grid=(m // tm, nj), in_specs=in_specs, out_specs=out_specs, out_shape=out_shape,
        compiler_params=_params(("parallel", "parallel")),
    )(x, *[w for w, _, _ in ws], *[a for a, _, _ in extras])


def _sigmoid(x):
    return 0.5 * jnp.tanh(0.5 * x) + 0.5


def _rope128(v, cos, sina, sinb):
    return (v * cos + pltpu.roll(v, LANES - MLA_ROPE // 2, 1) * sina
            + pltpu.roll(v, MLA_ROPE // 2, 1) * sinb)


def _ada_kernel(c_ref, w_ref, b_ref, t_ref, o_ref):
    c = c_ref[...]
    s = c * jax.nn.sigmoid(c)
    acc = jnp.dot(s.astype(BF16), w_ref[...].astype(BF16), preferred_element_type=F32) + b_ref[...]
    for l in range(DEPTH):
        o_ref[l] = acc + t_ref[l]


def _ada(c, w_ada, b_ada, ada_table):
    b = c.shape[0]
    rows = 8
    c8 = jnp.zeros((rows, D_MODEL), F32).at[:b].set(c)
    n = N_MOD * D_MODEL
    tn = 1024
    out = pl.pallas_call(
        _ada_kernel, grid=(n // tn,),
        in_specs=[pl.BlockSpec((rows, D_MODEL), lambda j: (0, 0)),
                  pl.BlockSpec((D_MODEL, tn), lambda j: (0, j)),
                  pl.BlockSpec((1, tn), lambda j: (0, j)),
                  pl.BlockSpec((DEPTH, 1, tn), lambda j: (0, 0, j))],
        out_specs=pl.BlockSpec((DEPTH, rows, tn), lambda j: (0, 0, j)),
        out_shape=jax.ShapeDtypeStruct((DEPTH, rows, n), F32),
        compiler_params=_params(("parallel",)),
    )(c8, w_ada, b_ada.reshape(1, n), ada_table.reshape(DEPTH, 1, n))
    return out[:, :b].reshape(DEPTH, b, N_MOD, D_MODEL)


def _normmod_kernel(x_ref, shift_ref, scale_ref, o_ref):
    x = x_ref[0]
    y = x * lax.rsqrt(jnp.mean(x * x, axis=-1, keepdims=True) + EPS)
    o_ref[0] = (y * (1.0 + scale_ref[0]) + shift_ref[0]).astype(o_ref.dtype)


def _finalnorm_kernel(x_ref, w_ref, o_ref):
    x = x_ref[0]
    y = x * lax.rsqrt(jnp.mean(x * x, axis=-1, keepdims=True) + EPS)
    o_ref[0] = y * w_ref[...]


def _normmod(x, shift, scale, ts=512):
    b, s, d = x.shape
    vec = pl.BlockSpec((1, 1, d), lambda bi, i: (bi, 0, 0))
    return pl.pallas_call(
        _normmod_kernel, grid=(b, s // ts),
        in_specs=[pl.BlockSpec((1, ts, d), lambda bi, i: (bi, i, 0)), vec, vec],
        out_specs=pl.BlockSpec((1, ts, d), lambda bi, i: (bi, i, 0)),
        out_shape=jax.ShapeDtypeStruct((b, s, d), BF16),
        compiler_params=_params(("parallel", "parallel")),
    )(x, shift.reshape(b, 1, d), scale.reshape(b, 1, d))


def _finalnorm(x, w, ts=512):
    b, s, d = x.shape
    return pl.pallas_call(
        _finalnorm_kernel, grid=(b, s // ts),
        in_specs=[pl.BlockSpec((1, ts, d), lambda bi, i: (bi, i, 0)),
                  pl.BlockSpec((1, d), lambda bi, i: (0, 0))],
        out_specs=pl.BlockSpec((1, ts, d), lambda bi, i: (bi, i, 0)),
        out_shape=jax.ShapeDtypeStruct((b, s, d), F32),
        compiler_params=_params(("parallel", "parallel")),
    )(x, w.reshape(1, d))


def _resid_epilogue(accs, e_refs, o_refs, *, coef):
    x_ref, g_ref = e_refs
    o_refs[0][...] = x_ref[...] + (coef * g_ref[0]) * accs[0]


def _ws_body(*refs, nx, nwt, ne, no, pairs, tn, nj, epilogue):
    x_refs = refs[:nx]
    w_hbm = refs[nx:nx + nwt]
    e_refs = refs[nx + nwt:nx + nwt + ne]
    o_refs = refs[nx + nwt + ne:nx + nwt + ne + no]
    stage, wbf, sems = refs[nx + nwt + ne + no:]
    j = pl.program_id(0)
    i = pl.program_id(1)

    def tile_copy(jj, p):
        _, wi, lead, col = pairs[p]
        src = w_hbm[wi].at[(*lead, slice(None), pl.ds(pl.multiple_of(col + jj * tn, tn), tn))]
        return pltpu.make_async_copy(src, stage.at[p], sems.at[p])

    @pl.when(i == 0)
    def _():
        @pl.when(j == 0)
        def _():
            for p in range(len(pairs)):
                tile_copy(j, p).start()

        for p in range(len(pairs)):
            tile_copy(j, p).wait()
            wbf[p] = stage[p].astype(BF16)

        @pl.when(j + 1 < nj)
        def _():
            for p in range(len(pairs)):
                tile_copy(j + 1, p).start()

    accs = [jnp.dot(x_refs[xi][...], wbf[p], preferred_element_type=F32)
            for p, (xi, _, _, _) in enumerate(pairs)]
    epilogue(accs, e_refs, o_refs)


def _ws_mm(xs, wts, pairs, extras, outs, epilogue, *, tm, tn, nj):
    m, kdim = xs[0].shape
    assert all(c % tn == 0 for _, _, _, c in pairs)
    in_specs = [pl.BlockSpec((tm, kdim), lambda j, i: (i, 0)) for _ in xs]
    in_specs += [pl.BlockSpec(memory_space=pl.ANY) for _ in wts]
    in_specs += [pl.BlockSpec(blk, lambda j, i, f=f: f(i, j)) for _, blk, f in extras]
    out_specs = [pl.BlockSpec(blk, lambda j, i, f=f: f(i, j)) for _, _, blk, f in outs]
    out_shape = [jax.ShapeDtypeStruct(sh, dt) for sh, dt, _, _ in outs]
    body = functools.partial(_ws_body, nx=len(xs), nwt=len(wts), ne=len(extras), no=len(outs),
                             pairs=pairs, tn=tn, nj=nj, epilogue=epilogue)
    return pl.pallas_call(
        body, grid=(nj, m // tm), in_specs=in_specs, out_specs=out_specs, out_shape=out_shape,
        scratch_shapes=[pltpu.VMEM((len(pairs), kdim, tn), F32), pltpu.VMEM((len(pairs), kdim, tn), BF16),
                        pltpu.SemaphoreType.DMA((len(pairs),))],
        compiler_params=_params(("arbitrary", "arbitrary")),
    )(*xs, *wts, *[a for a, _, _ in extras])


def _wst_body(*refs, ne, no, lay, tn, nj, segs, parts, epilogue):
    x_ref, wt_hbm = refs[0], refs[1]
    e_refs = refs[2:2 + ne]
    o_refs = refs[2 + ne:2 + ne + no]
    stage, wbf, sems = refs[2 + ne + no:]
    j = pl.program_id(0)
    i = pl.program_id(1)

    def copies(jj):
        if parts is not None:
            return [pltpu.make_async_copy(wt_hbm.at[lay, pl.ds(src, n), :], stage.at[pl.ds(dst, n), :],
                                          sems.at[c]) for c, (src, n, dst) in enumerate(parts)]
        row = segs[0][1] + jj * tn
        for first_tile, row0 in segs[1:]:
            row = jnp.where(jj >= first_tile, row0 + (jj - first_tile) * tn, row)
        src = wt_hbm.at[lay, pl.ds(pl.multiple_of(row, 8), tn), :]
        return [pltpu.make_async_copy(src, stage, sems.at[0])]

    @pl.when(i == 0)
    def _():
        @pl.when(j == 0)
        def _():
            if parts is not None:
                covered = sorted((dst, dst + n) for _, n, dst in parts)
                for lo, hi in zip([0] + [b for _, b in covered], [a for a, _ in covered] + [tn]):
                    if hi > lo:
                        stage[lo:hi, :] = jnp.zeros((hi - lo, stage.shape[1]), F32)
            for cp in copies(j):
                cp.start()

        for cp in copies(j):
            cp.wait()
        wbf[...] = stage[...].astype(BF16)

        @pl.when(j + 1 < nj)
        def _():
            for cp in copies(j + 1):
                cp.start()

    acc = lax.dot_general(x_ref[...], wbf[...], (((1,), (1,)), ((), ())), preferred_element_type=F32)
    epilogue([acc], e_refs, o_refs)


def _wst_mm(x, wt, lay, extras, outs, epilogue, *, tm, tn, nj, segs=None, parts=None):
    m, kdim = x.shape
    assert (segs is None) != (parts is None) and (parts is None or nj == 1)
    in_specs = [pl.BlockSpec((tm, kdim), lambda j, i: (i, 0)), pl.BlockSpec(memory_space=pl.ANY)]
    in_specs += [pl.BlockSpec(blk, lambda j, i, f=f: f(i, j)) for _, blk, f in extras]
    out_specs = [pl.BlockSpec(blk, lambda j, i, f=f: f(i, j)) for _, _, blk, f in outs]
    out_shape = [jax.ShapeDtypeStruct(sh, dt) for sh, dt, _, _ in outs]
    body = functools.partial(_wst_body, ne=len(extras), no=len(outs), lay=lay, tn=tn, nj=nj,
                             segs=segs, parts=parts, epilogue=epilogue)
    return pl.pallas_call(
        body, grid=(nj, m // tm), in_specs=in_specs, out_specs=out_specs, out_shape=out_shape,
        scratch_shapes=[pltpu.VMEM((tn, kdim), F32), pltpu.VMEM((tn, kdim), BF16),
                        pltpu.SemaphoreType.DMA((len(parts) if parts else 1,))],
        compiler_params=_params(("arbitrary", "arbitrary")),
    )(x, wt, *[a for a, _, _ in extras])


def _swiglu_epilogue(accs, e_refs, o_refs):
    g, u = accs
    o_refs[0][...] = (g * _sigmoid(g) * u).astype(BF16)


def _resid_mm(a, w, lay, x2d, gate, coef, seq, *, tm, tn):
    m, n = x2d.shape
    per_b = seq // tm
    return _ws_mm([a], [w], [(0, 0, (lay,), 0)],
                  [(x2d, (tm, tn), lambda i, j: (i, j)),
                   (gate.reshape(-1, 1, n), (1, 1, tn), lambda i, j: (i // per_b, 0, j))],
                  [((m, n), F32, (tm, tn), lambda i, j: (i, j))],
                  functools.partial(_resid_epilogue, coef=coef), tm=tm, tn=tn, nj=n // tn)[0]


def _swiglu_mm(h2, w_in, lay, *, tm=1024, tn=512):
    m, _ = h2.shape
    return _ws_mm([h2], [w_in], [(0, 0, (lay,), 0), (0, 0, (lay,), D_FF)], [],
                  [((m, D_FF), BF16, (tm, tn), lambda i, j: (i, j))],
                  _swiglu_epilogue, tm=tm, tn=tn, nj=D_FF // tn)[0]


def _ffn(x, h, w_in, w_out, lay, gate):
    b, s, d = x.shape
    m = b * s
    a = _swiglu_mm(h.reshape(m, d), w_in, lay)
    y = _resid_mm(a, w_out, lay, x.reshape(m, d), gate, 0.5, s, tm=512, tn=512)
    return y.reshape(b, s, d)


def _rope_tab_kernel(p_ref, inv_ref, cos_ref, sina_ref, sinb_ref):
    ang = p_ref[...].astype(F32) * inv_ref[...]
    lane = lax.broadcasted_iota(jnp.int32, ang.shape, 1)
    half = MLA_ROPE // 2
    c = jnp.cos(ang)
    s = jnp.sin(ang)
    cos_ref[...] = jnp.where(lane < MLA_ROPE, c, 0.0)
    sina_ref[...] = jnp.where(lane < half, -s, 0.0)
    sinb_ref[...] = jnp.where(lane >= half, jnp.where(lane < MLA_ROPE, s, 0.0), 0.0)


def _rope_tables(positions, ts=512):
    m = positions.size
    half = MLA_ROPE // 2
    inv = ROPE_THETA ** (-jnp.arange(0, MLA_ROPE, 2, dtype=F32) / MLA_ROPE)
    inv128 = jnp.concatenate([inv, inv, jnp.zeros((LANES - 2 * half,), F32)]).reshape(1, LANES)
    spec = pl.BlockSpec((ts, LANES), lambda i: (i, 0))
    shp = jax.ShapeDtypeStruct((m, LANES), F32)
    return pl.pallas_call(
        _rope_tab_kernel, grid=(m // ts,),
        in_specs=[pl.BlockSpec((ts, 1), lambda i: (i, 0)), pl.BlockSpec((1, LANES), lambda i: (0, 0))],
        out_specs=[spec, spec, spec], out_shape=[shp, shp, shp],
        compiler_params=_params(("parallel",)),
    )(positions.reshape(m, 1), inv128)


def _plain_epilogue(accs, e_refs, o_refs):
    o_refs[0][...] = accs[0].astype(o_refs[0].dtype)


def _sigmoid_epilogue(accs, e_refs, o_refs):
    o_refs[0][...] = _sigmoid(accs[0]).astype(o_refs[0].dtype)


def _rmsw_epilogue(accs, e_refs, o_refs):
    a = accs[0]
    y = a * lax.rsqrt(jnp.mean(a * a, axis=-1, keepdims=True) + EPS) * e_refs[0][0]
    o_refs[0][...] = y.astype(o_refs[0].dtype)


def _kvlatent_epilogue(accs, e_refs, o_refs):
    a = accs[0]
    w_ref, cos_ref, sina_ref, sinb_ref = e_refs
    ckv = a[:, :MLA_KV_RANK]
    y = ckv * lax.rsqrt(jnp.mean(ckv * ckv, axis=-1, keepdims=True) + EPS) * w_ref[0]
    o_refs[0][...] = y.astype(BF16)
    kr = a[:, MLA_KV_RANK:MLA_KV_RANK + LANES]
    o_refs[1][...] = _rope128(kr, cos_ref[...], sina_ref[...], sinb_ref[...]).astype(BF16)
    o_refs[2][...] = a[:, MLA_KV_RANK + LANES:]


def _qrope_epilogue(accs, e_refs, o_refs, *, heads_per_tile):
    a = accs[0] * Q_SCALE_LOG2
    cos_ref, sina_ref, sinb_ref = e_refs
    cos, sina, sinb = cos_ref[...], sina_ref[...], sinb_ref[...]
    for c in range(heads_per_tile):
        sl = slice(c * LANES, (c + 1) * LANES)
        o_refs[0][:, sl] = _rope128(a[:, sl], cos, sina, sinb).astype(BF16)


def _scaled_epilogue(accs, e_refs, o_refs, *, scale):
    o_refs[0][...] = (accs[0] * scale).astype(o_refs[0].dtype)


def _band(ts, lo, hi, first_tile):
    t = lax.broadcasted_iota(jnp.int32, (ts, HALO + ts), 0)
    s = lax.broadcasted_iota(jnp.int32, (ts, HALO + ts), 1)
    d = t + HALO - s
    ok = jnp.where(d >= lo, jnp.where(d < hi, 1.0, 0.0), 0.0)
    ok = jnp.where(s < HALO, jnp.where(first_tile, 0.0, ok), ok)
    return ok.astype(BF16)


def _pool_kernel(u_ref, halo_ref, pw_ref, ps_ref, o_ref, *, ts):
    i = pl.program_id(1)
    u = u_ref[0]
    ucat = jnp.concatenate([halo_ref[0], u], axis=0)
    tg = i * ts + lax.broadcasted_iota(jnp.int32, (ts, 1), 0)
    for g, w in enumerate(POOL_WINDOWS):
        sl = slice(g * POOL_GW, (g + 1) * POOL_GW)
        win = jnp.dot(_band(ts, 0, w, i == 0), ucat[:, sl], preferred_element_type=F32)
        cnt = jnp.minimum(tg + 1, w).astype(F32)
        p = win / cnt - u[:, sl].astype(F32)
        y = jnp.dot(p.astype(BF16), pw_ref[0, g], preferred_element_type=F32)
        o_ref[0, :, sl] = (y * ps_ref[0, :, sl]).astype(BF16)


def _pool(zp, pool_w, pool_scale, lay, b, s, ts=256):
    hb = ts // HALO
    ng = len(POOL_WINDOWS)
    return pl.pallas_call(
        functools.partial(_pool_kernel, ts=ts), grid=(b, s // ts),
        in_specs=[pl.BlockSpec((1, ts, MIX_W), lambda bi, i: (bi, i, 0)),
                  pl.BlockSpec((1, HALO, MIX_W), lambda bi, i: (bi, jnp.maximum(i * hb - 1, 0), 0)),
                  pl.BlockSpec((1, ng, POOL_GW, POOL_GW), lambda bi, i: (lay, 0, 0, 0)),
                  pl.BlockSpec((1, 1, MIX_W), lambda bi, i: (lay, 0, 0))],
        out_specs=pl.BlockSpec((1, ts, MIX_W), lambda bi, i: (bi, i, 0)),
        out_shape=jax.ShapeDtypeStruct((b, s, MIX_W), BF16),
        compiler_params=_params(("parallel", "parallel")),
    )(zp, zp, pool_w, pool_scale.reshape(DEPTH, 1, MIX_W))


def _conv_kernel(x_ref, halo_ref, w_ref, b_ref, sc_ref, o_ref, *, ts):
    i = pl.program_id(1)
    x = x_ref[0]
    xcat = jnp.concatenate([halo_ref[0], x], axis=0)
    w = w_ref[0]
    acc = x.astype(F32) * w[CONV_W - 1:CONV_W, :] + b_ref[0]
    for d in range(1, CONV_W):
        xs = jnp.dot(_band(ts, d, d + 1, i == 0), xcat, preferred_element_type=F32)
        acc = acc + xs * w[CONV_W - 1 - d:CONV_W - d, :]
    o_ref[0] = (acc * jax.nn.sigmoid(acc) * sc_ref[...]).astype(BF16)


def _conv_silu(zp, conv_w, conv_b, lay, b, s, ts=256):
    c = 2 * ML_HEADS * ML_DK
    hb = ts // HALO
    post = jnp.concatenate([jnp.full((c // 2,), ML_DK ** -0.5, F32), jnp.ones((c // 2,), F32)])
    return pl.pallas_call(
        functools.partial(_conv_kernel, ts=ts), grid=(b, s // ts),
        in_specs=[pl.BlockSpec((1, ts, c), lambda bi, i: (bi, i, 1)),
                  pl.BlockSpec((1, HALO, c), lambda bi, i: (bi, jnp.maximum(i * hb - 1, 0), 1)),
                  pl.BlockSpec((1, CONV_W, c), lambda bi, i: (lay, 0, 0)),
                  pl.BlockSpec((1, 1, c), lambda bi, i: (lay, 0, 0)),
                  pl.BlockSpec((1, c), lambda bi, i: (0, 0))],
        out_specs=pl.BlockSpec((1, ts, c), lambda bi, i: (bi, i, 0)),
        out_shape=jax.ShapeDtypeStruct((b, s, c), BF16),
        compiler_params=_params(("parallel", "parallel")),
    )(zp, zp, conv_w, conv_b.reshape(DEPTH, 1, c), post.reshape(1, c))


ATTN_HEADS_PER_STEP = 8


def _attn_kernel(qn_ref, qr_ref, kn_ref, kr_ref, v_ref, pq_ref, pk_ref, o_ref, m_sc, l_sc, acc_sc, qt_sc, vt_sc,
                 *, tq):
    i = pl.program_id(2)
    hs = [slice(g * LANES, (g + 1) * LANES) for g in range(ATTN_HEADS_PER_STEP)]

    for g, sl in enumerate(hs):
        qt_sc[g, :LANES, :] = qn_ref[0, :, sl].astype(F32).T.astype(BF16)
        qt_sc[g, LANES:, :] = qr_ref[0, :, sl].astype(F32).T.astype(BF16)

    @pl.when(i == 0)
    def _():
        for g, sl in enumerate(hs):
            vt_sc[g] = v_ref[0, :, sl].astype(F32).T.astype(BF16)

    def block(start, mask, first):
        kr = kr_ref[0, pl.ds(start, tq), :]

        def scores(g):
            k = jnp.concatenate([kn_ref[0, pl.ds(start, tq), hs[g]], kr], axis=-1)
            st = jnp.dot(k, qt_sc[g], preferred_element_type=F32)
            if mask is not None:
                st = jnp.where(mask, st, -jnp.inf)
            return st

        st_next = scores(0)
        for g, sl in enumerate(hs):
            st = st_next
            if g + 1 < len(hs):
                st_next = scores(g + 1)
            vt = vt_sc[g, :, pl.ds(start, tq)]
            smax = jnp.max(st, axis=0, keepdims=True)
            if first:
                m_new = smax
                p = jnp.exp2(st - m_new)
                l_sc[g] = jnp.sum(p, axis=0, keepdims=True)
                acc_sc[g] = jnp.dot(vt, p.astype(BF16), preferred_element_type=F32)
            else:
                m_old = m_sc[g]
                m_new = jnp.maximum(m_old, smax)
                alpha = jnp.exp2(m_old - m_new)
                p = jnp.exp2(st - m_new)
                l_sc[g] = alpha * l_sc[g] + jnp.sum(p, axis=0, keepdims=True)
                acc_sc[g] = alpha * acc_sc[g] + jnp.dot(vt, p.astype(BF16), preferred_element_type=F32)
            m_sc[g] = m_new

    sh = CHUNK.bit_length() - 1
    mask = lax.shift_right_arithmetic(pk_ref[0], sh) <= lax.shift_right_arithmetic(pq_ref[0], sh)
    block(pl.multiple_of(i * tq, tq), mask, True)

    def body(j, carry):
        block(pl.multiple_of(j * tq, tq), None, False)
        return carry

    lax.fori_loop(0, i, body, 0)
    for g, sl in enumerate(hs):
        o_ref[0, :, sl] = (acc_sc[g] / l_sc[g]).T.astype(o_ref.dtype)


def _attention(q, kv, kr, positions, tq=512):
    b, s, _ = q.shape
    G = ATTN_HEADS_PER_STEP
    w = G * LANES
    ng = MLA_HEADS // G
    return pl.pallas_call(
        functools.partial(_attn_kernel, tq=tq), grid=(b, ng, s // tq),
        in_specs=[pl.BlockSpec((1, tq, w), lambda bi, hi, i: (bi, i, hi)),
                  pl.BlockSpec((1, tq, w), lambda bi, hi, i: (bi, i, ng + hi)),
                  pl.BlockSpec((1, s, w), lambda bi, hi, i: (bi, 0, hi)),
                  pl.BlockSpec((1, s, LANES), lambda bi, hi, i: (bi, 0, 0)),
                  pl.BlockSpec((1, s, w), lambda bi, hi, i: (bi, 0, ng + hi)),
                  pl.BlockSpec((1, 1, tq), lambda bi, hi, i: (bi, 0, i)),
                  pl.BlockSpec((1, tq, 1), lambda bi, hi, i: (bi, i, 0))],
        out_specs=pl.BlockSpec((1, tq, w), lambda bi, hi, i: (bi, i, hi)),
        out_shape=jax.ShapeDtypeStruct((b, s, MLA_HEADS * MLA_V), BF16),
        scratch_shapes=[pltpu.VMEM((G, 1, tq), F32), pltpu.VMEM((G, 1, tq), F32),
                        pltpu.VMEM((G, MLA_V, tq), F32),
                        pltpu.VMEM((G, 2 * LANES, tq), BF16), pltpu.VMEM((G, MLA_V, s), BF16)],
        compiler_params=_params(("parallel", "parallel", "arbitrary")),
    )(q, q, kv, kr, kv, positions.reshape(b, 1, s), positions.reshape(b, s, 1))


def _split3_dot(a_bf16, x):
    hi = x.astype(BF16)
    r1 = x - hi.astype(F32)
    mid = r1.astype(BF16)
    lo = (r1 - mid.astype(F32)).astype(BF16)
    return (jnp.dot(a_bf16, hi, preferred_element_type=F32)
            + jnp.dot(a_bf16, mid, preferred_element_type=F32)
            + jnp.dot(a_bf16, lo, preferred_element_type=F32))


def _mlstm_kernel(qk_ref, v_ref, g_ref, og_ref, gb_ref, hn_ref, o_ref, ct_ref, m_ref):
    cidx = pl.program_id(1)

    @pl.when(cidx == 0)
    def _():
        ct_ref[...] = jnp.zeros_like(ct_ref)
        m_ref[...] = jnp.zeros_like(m_ref)

    L = CHUNK
    g = g_ref[0]
    gb = gb_ref[...]
    logi = g[:, :LANES] + gb[:, :LANES]
    xf = g[:, LANES:] + gb[:, LANES:]
    logf = jnp.minimum(xf, 0.0) - jnp.log1p(jnp.exp(-jnp.abs(xf)))
    row = lax.broadcasted_iota(jnp.int32, (L, L), 0)
    col = lax.broadcasted_iota(jnp.int32, (L, L), 1)
    tril = row >= col
    bcum = _split3_dot(jnp.where(tril, 1.0, 0.0).astype(BF16), logf)
    x = logi - bcum
    xt = x.T
    m_row = m_ref[...]
    b_last = bcum[L - 1:L, :]
    m_new = jnp.maximum(b_last + m_row, b_last + jnp.max(x, axis=0, keepdims=True))
    decay = jnp.exp(b_last + m_row - m_new)
    ws_all = jnp.exp(b_last + x - m_new)
    inter_all = bcum + m_row
    ones_col = jnp.ones((L, LANES), BF16)
    nt = (((1,), (1,)), ((), ()))
    tn = (((0,), (0,)), ((), ()))

    def early(h):
        q = qk_ref[0, :, h * ML_DK:(h + 1) * ML_DK]
        k = qk_ref[0, :, (ML_HEADS + h) * ML_DK:(ML_HEADS + h + 1) * ML_DK]
        ct = ct_ref[h]
        qk = lax.dot_general(q, k, nt, preferred_element_type=F32)
        qc = jnp.dot(q, ct.astype(BF16), preferred_element_type=F32)
        dlog = jnp.where(tril, bcum[:, h:h + 1] + xt[h:h + 1, :], -jnp.inf)
        inter = inter_all[:, h:h + 1]
        mt = jnp.maximum(inter, jnp.max(dlog, axis=-1, keepdims=True))
        return k, ct, qk, qc, mt, jnp.exp(inter - mt), jnp.exp(dlog - mt)

    ahead = early(0)
    hh_all = []
    for h in range(ML_HEADS):
        k, ct, qk, qc, mt, w_inter, e = ahead
        if h + 1 < ML_HEADS:
            ahead = early(h + 1)
        vaug = jnp.concatenate([v_ref[0, :, h * ML_DV:(h + 1) * ML_DV], ones_col], axis=-1)
        num = jnp.dot((qk * e).astype(BF16), vaug, preferred_element_type=F32) + w_inter * qc
        den = jnp.maximum(jnp.abs(num[:, ML_DV:]), jnp.exp(-mt))
        hh_all.append(num[:, :ML_DV] / jnp.concatenate([den, den], axis=-1))
        wv = (ws_all[:, h:h + 1] * vaug.astype(F32)).astype(BF16)
        ct_ref[h] = decay[:, h:h + 1] * ct + lax.dot_general(k, wv, tn, preferred_element_type=F32)
    for h, hh in enumerate(hh_all):
        hs = slice(h * ML_DV, (h + 1) * ML_DV)
        hn = hh * lax.rsqrt(jnp.mean(hh * hh, axis=-1, keepdims=True) + EPS) * hn_ref[0, :, hs]
        o_ref[0, :, hs] = (og_ref[0, :, hs].astype(F32) * hn).astype(BF16)
    m_ref[...] = m_new


def _mlstm(qk, zp, gates, zs, gate_b, head_norm, lay):
    b, s, _ = qk.shape
    gb = jnp.zeros((1, 2 * LANES), F32)
    gb = gb.at[0, :ML_HEADS].set(gate_b[:ML_HEADS]).at[0, LANES:LANES + ML_HEADS].set(gate_b[ML_HEADS:])
    blk = lambda col: pl.BlockSpec((1, CHUNK, MIX_W), lambda bi, c, col=col: (bi, c, col))
    return pl.pallas_call(
        _mlstm_kernel, grid=(b, s // CHUNK),
        in_specs=[blk(0), blk(2),
                  pl.BlockSpec((1, CHUNK, 2 * LANES), lambda bi, c: (bi, c, 0)),
                  blk(0),
                  pl.BlockSpec((1, 2 * LANES), lambda bi, c: (0, 0)),
                  pl.BlockSpec((1, 1, MIX_W), lambda bi, c: (lay, 0, 0))],
        out_specs=blk(0),
        out_shape=jax.ShapeDtypeStruct((b, s, MIX_W), BF16),
        scratch_shapes=[pltpu.VMEM((ML_HEADS, ML_DK, ML_DV + LANES), F32), pltpu.VMEM((1, LANES), F32)],
        compiler_params=_params(("parallel", "arbitrary")),
    )(qk, zp, gates, zs, gb, head_norm.reshape(DEPTH, 1, MIX_W))


def _merge_epilogue(accs, e_refs, o_refs):
    acc = e_refs[0][...].astype(F32) * accs[0]
    acc += e_refs[1][...].astype(F32) * accs[1]
    acc += e_refs[2][...].astype(F32) * accs[2]
    o_refs[0][...] = acc.astype(BF16)


def _merge(ya, yb, yc, w_branch, lay, zs, tm=1024, tn=512):
    m = ya.shape[0]
    goff = MIX_W // tn
    gate = lambda k: (zs, (tm, tn), lambda i, j, k=k: (i, goff + k * (D_MODEL // tn) + j))
    return _ws_mm([ya, yb, yc], [w_branch], [(k, 0, (lay, k), 0) for k in range(N_BRANCH)],
                  [gate(k) for k in range(N_BRANCH)],
                  [((m, D_MODEL), BF16, (tm, tn), lambda i, j: (i, j))],
                  _merge_epilogue, tm=tm, tn=tn, nj=D_MODEL // tn)[0]


def _prep_mixer_weights(w_mix_in, mla_w_uq, mla_w_ukv):
    w_t = jnp.swapaxes(w_mix_in, 1, 2)
    uq = mla_w_uq.reshape(DEPTH, MLA_Q_RANK, MLA_HEADS, MLA_NOPE + MLA_ROPE)
    w_qn = uq[..., :MLA_NOPE].reshape(DEPTH, MLA_Q_RANK, MLA_HEADS * MLA_NOPE)
    w_qr = jnp.pad(uq[..., MLA_NOPE:], ((0, 0), (0, 0), (0, 0), (0, LANES - MLA_ROPE)))
    w_q = jnp.concatenate([w_qn, w_qr.reshape(DEPTH, MLA_Q_RANK, MLA_HEADS * LANES)], axis=-1).astype(BF16)
    ukv = mla_w_ukv.reshape(DEPTH, MLA_KV_RANK, MLA_HEADS, 2, MLA_NOPE)
    w_kv = ukv.transpose(0, 1, 3, 2, 4).reshape(DEPTH, MLA_KV_RANK, 2 * MLA_HEADS * MLA_NOPE).astype(BF16)
    return w_t, w_q, w_kv


def _mixer(h, lay, positions, rope_tabs, w_t, w_q, w_kv, pool_w, pool_scale, mla_q_norm, mla_kv_norm,
           ml_conv_w, ml_conv_b, ml_gate_b, ml_head_norm, w_branch, tm=1024):
    b, s, d = h.shape
    m = b * s
    h2 = h.reshape(m, d)
    cos, sina, sinb = rope_tabs
    tab = lambda a: (a, (tm, LANES), lambda i, j: (i, 0))
    tile = lambda i, j: (i, j)
    layvec = lambda a: (a.reshape(DEPTH, 1, -1), (1, 1, a.shape[-1]), lambda i, j: (lay, 0, 0))

    o = _OFF
    wide = 1024
    zp = _wst_mm(h2, w_t, lay, [], [((m, N_PLAIN), BF16, (tm, wide), tile)], _plain_epilogue,
                 tm=tm, tn=wide, nj=N_PLAIN // wide, segs=((0, o[0]), (MIX_W // wide, o[4])))[0]
    zs = _wst_mm(h2, w_t, lay, [], [((m, N_SIG), BF16, (tm, wide), tile)], _sigmoid_epilogue,
                 tm=tm, tn=wide, nj=N_SIG // wide, segs=((0, o[7]), (MIX_W // wide, o[10])))[0]
    cqn = _wst_mm(h2, w_t, lay, [layvec(mla_q_norm)], [((m, MLA_Q_RANK), BF16, (tm, MLA_Q_RANK), tile)],
                  _rmsw_epilogue, tm=tm, tn=MLA_Q_RANK, nj=1, segs=((0, o[1]),))[0]
    ckvn, kr, gates = _wst_mm(
        h2, w_t, lay, [layvec(mla_kv_norm), tab(cos), tab(sina), tab(sinb)],
        [((m, MLA_KV_RANK), BF16, (tm, MLA_KV_RANK), tile),
         ((m, LANES), BF16, (tm, LANES), tile),
         ((m, 2 * LANES), F32, (tm, 2 * LANES), tile)],
        _kvlatent_epilogue, tm=tm, tn=N_LAT, nj=1,
        parts=((o[2], MLA_KV_RANK, 0), (o[3], MLA_ROPE, MLA_KV_RANK),
               (o[8], ML_HEADS, MLA_KV_RANK + LANES), (o[9], ML_HEADS, MLA_KV_RANK + 2 * LANES)))

    zp3 = zp.reshape(b, s, N_PLAIN)
    ya = _pool(zp3, pool_w, pool_scale, lay, b, s)

    nq = MLA_HEADS * LANES
    q = _mm(cqn, [(w_q, lay, 0)], [tab(cos), tab(sina), tab(sinb)], [((m, 2 * nq), BF16, (tm, nq), tile)],
            functools.partial(_q_epilogue, heads_per_tile=MLA_HEADS, n_nope_tiles=1),
            tm=tm, tn=nq, nj=2)[0]
    kv = _mm(ckvn, [(w_kv, lay, 0)], [], [((m, 2 * nq), BF16, (tm, nq), tile)],
             _plain_epilogue, tm=tm, tn=nq, nj=2)[0]
    yb = _attention(q.reshape(b, s, 2 * nq), kv.reshape(b, s, 2 * nq), kr.reshape(b, s, LANES), positions)

    qk = _conv_silu(zp3, ml_conv_w, ml_conv_b, lay, b, s)
    yc = _mlstm(qk, zp3, gates.reshape(b, s, 2 * LANES), zs.reshape(b, s, N_SIG), ml_gate_b[lay],
                ml_head_norm, lay)

    return _merge(ya.reshape(m, MIX_W), yb.reshape(m, MIX_W), yc.reshape(m, MIX_W), w_branch, lay, zs)


def _q_epilogue(accs, e_refs, o_refs, *, heads_per_tile, n_nope_tiles):
    j = pl.program_id(1)

    @pl.when(j < n_nope_tiles)
    def _():
        _scaled_epilogue(accs, (), o_refs, scale=Q_SCALE_LOG2)

    @pl.when(j >= n_nope_tiles)
    def _():
        _qrope_epilogue(accs, e_refs, o_refs, heads_per_tile=heads_per_tile)


def kernel(x, c, positions, w_ada, b_ada, ada_table, ffn_a_w_in, ffn_a_w_out, w_mix_in, pool_w, pool_scale, mla_q_norm, mla_w_uq, mla_kv_norm, mla_w_ukv, ml_conv_w, ml_conv_b, ml_gate_b, ml_head_norm, w_branch, w_out, ffn_b_w_in, ffn_b_w_out, final_norm):
    b, s, d = x.shape
    m = b * s
    mod = _ada(c, w_ada, b_ada, ada_table)
    rope_tabs = _rope_tables(positions)
    w_t, w_q, w_kv = _prep_mixer_weights(w_mix_in, mla_w_uq, mla_w_ukv)
    pool_wb = pool_w.astype(BF16)
    for l in range(DEPTH):
        md = mod[l]
        h = _normmod(x, md[:, 0], md[:, 1])
        x = _ffn(x, h, ffn_a_w_in, ffn_a_w_out, l, md[:, 2])
        h = _normmod(x, md[:, 3], md[:, 4])
        merged = _mixer(h, l, positions, rope_tabs, w_t, w_q, w_kv, pool_wb, pool_scale, mla_q_norm,
                        mla_kv_norm, ml_conv_w, ml_conv_b, ml_gate_b, ml_head_norm, w_branch)
        x = _resid_mm(merged, w_out, l, x.reshape(m, d), md[:, 5], 1.0, s, tm=512, tn=1024).reshape(b, s, d)
        h = _normmod(x, md[:, 6], md[:, 7])
        x = _ffn(x, h, ffn_b_w_in, ffn_b_w_out, l, md[:, 8])
    return _finalnorm(x, final_norm)
```

```python
import functools

import numpy as np
import jax
import jax.numpy as jnp
from jax import lax
from jax.experimental import pallas as pl
from jax.experimental.pallas import tpu as pltpu

F32 = jnp.float32
BF16 = jnp.bfloat16

D_MODEL = 4096
DEPTH = 2
CHUNK = 64
EPS = 1e-6
D_FF = 2 * D_MODEL
MIX_W = D_MODEL // 2
N_BRANCH = 3
N_MOD = 9
POOL_WINDOWS = (2, 4, 8, 16)
POOL_GW = MIX_W // len(POOL_WINDOWS)
MLA_NOPE = 128
MLA_ROPE = 64
MLA_V = 128
MLA_HEADS = MIX_W // MLA_V
MLA_Q_RANK = D_MODEL // 4
MLA_KV_RANK = 512
MLA_SCALE = (MLA_NOPE + MLA_ROPE) ** -0.5
Q_SCALE_LOG2 = MLA_SCALE * float(np.log2(np.e))
ROPE_THETA = 10000.0
ML_HEADS = 8
ML_DK = 128
ML_DV = MIX_W // ML_HEADS
CONV_W = 4

_SPLITS = (MIX_W, MLA_Q_RANK, MLA_KV_RANK, MLA_ROPE, ML_HEADS * ML_DK, ML_HEADS * ML_DK,
           ML_HEADS * ML_DV, ML_HEADS * ML_DV, ML_HEADS, ML_HEADS, N_BRANCH * D_MODEL)
_OFF = tuple(int(v) for v in np.cumsum((0,) + _SPLITS))

LANES = 128
SUBLANES = 8
HALO = 16
VMEM_LIMIT = 60 * 1024 * 1024

N_PLAIN = 3 * MIX_W
N_SIG = MIX_W + N_BRANCH * D_MODEL
N_LAT = MLA_KV_RANK + 3 * LANES


def _params(sem):
    return pltpu.CompilerParams(dimension_semantics=sem, vmem_limit_bytes=VMEM_LIMIT)


def _mm_body(*refs, nw, ne, no, epilogue):
    x_ref = refs[0]
    w_refs = refs[1:1 + nw]
    e_refs = refs[1 + nw:1 + nw + ne]
    o_refs = refs[1 + nw + ne:1 + nw + ne + no]
    accs = [jnp.dot(x_ref[...], w[0], preferred_element_type=F32) for w in w_refs]
    epilogue(accs, e_refs, o_refs)


def _mm(x, ws, extras, outs, epilogue, *, tm, tn, nj):
    m, kdim = x.shape
    in_specs = [pl.BlockSpec((tm, kdim), lambda i, j: (i, 0))]
    for _, lay, col in ws:
        assert col % tn == 0
        in_specs.append(pl.BlockSpec((1, kdim, tn), lambda i, j, lay=lay, off=col // tn: (lay, 0, off + j)))
    for _, blk, f in extras:
        in_specs.append(pl.BlockSpec(blk, f))
    out_specs = [pl.BlockSpec(blk, f) for _, _, blk, f in outs]
    out_shape = [jax.ShapeDtypeStruct(s, d) for s, d, _, _ in outs]
    body = functools.partial(_mm_body, nw=len(ws), ne=len(extras), no=len(outs), epilogue=epilogue)
    return pl.pallas_call(
        body, grid=(m // tm, nj), in_specs=in_specs, out_specs=out_specs, out_shape=out_shape,
        compiler_params=_params(("parallel", "parallel")),
    )(x, *[w for w, _, _ in ws], *[a for a, _, _ in extras])


def _sigmoid(x):
    return 0.5 * jnp.tanh(0.5 * x) + 0.5


def _rope128(v, cos, sina, sinb):
    return (v * cos + pltpu.roll(v, LANES - MLA_ROPE // 2, 1) * sina
            + pltpu.roll(v, MLA_ROPE // 2, 1) * sinb)


def _ada_kernel(c_ref, w_ref, b_ref, t_ref, o_ref):
    c = c_ref[...]
    s = c * jax.nn.sigmoid(c)
    acc = jnp.dot(s.astype(BF16), w_ref[...].astype(BF16), preferred_element_type=F32) + b_ref[...]
    for l in range(DEPTH):
        o_ref[l] = acc + t_ref[l]


def _ada(c, w_ada, b_ada, ada_table):
    b = c.shape[0]
    rows = SUBLANES
    assert b <= rows
    c8 = jnp.zeros((rows, D_MODEL), F32).at[:b].set(c)
    n = N_MOD * D_MODEL
    tn = 1024
    out = pl.pallas_call(
        _ada_kernel, grid=(n // tn,),
        in_specs=[pl.BlockSpec((rows, D_MODEL), lambda j: (0, 0)),
                  pl.BlockSpec((D_MODEL, tn), lambda j: (0, j)),
                  pl.BlockSpec((1, tn), lambda j: (0, j)),
                  pl.BlockSpec((DEPTH, 1, tn), lambda j: (0, 0, j))],
        out_specs=pl.BlockSpec((DEPTH, rows, tn), lambda j: (0, 0, j)),
        out_shape=jax.ShapeDtypeStruct((DEPTH, rows, n), F32),
        compiler_params=_params(("parallel",)),
    )(c8, w_ada, b_ada.reshape(1, n), ada_table.reshape(DEPTH, 1, n))
    return out[:, :b].reshape(DEPTH, b, N_MOD, D_MODEL)


def _normmod_kernel(x_ref, shift_ref, scale_ref, o_ref):
    x = x_ref[0]
    y = x * lax.rsqrt(jnp.mean(x * x, axis=-1, keepdims=True) + EPS)
    o_ref[0] = (y * (1.0 + scale_ref[0]) + shift_ref[0]).astype(o_ref.dtype)


def _finalnorm_kernel(x_ref, w_ref, o_ref):
    x = x_ref[0]
    y = x * lax.rsqrt(jnp.mean(x * x, axis=-1, keepdims=True) + EPS)
    o_ref[0] = y * w_ref[...]


def _normmod(x, shift, scale, ts=512):
    b, s, d = x.shape
    vec = pl.BlockSpec((1, 1, d), lambda bi, i: (bi, 0, 0))
    return pl.pallas_call(
        _normmod_kernel, grid=(b, s // ts),
        in_specs=[pl.BlockSpec((1, ts, d), lambda bi, i: (bi, i, 0)), vec, vec],
        out_specs=pl.BlockSpec((1, ts, d), lambda bi, i: (bi, i, 0)),
        out_shape=jax.ShapeDtypeStruct((b, s, d), BF16),
        compiler_params=_params(("parallel", "parallel")),
    )(x, shift.reshape(b, 1, d), scale.reshape(b, 1, d))


def _finalnorm(x, w, ts=512):
    b, s, d = x.shape
    return pl.pallas_call(
        _finalnorm_kernel, grid=(b, s // ts),
        in_specs=[pl.BlockSpec((1, ts, d), lambda bi, i: (bi, i, 0)),
                  pl.BlockSpec((1, d), lambda bi, i: (0, 0))],
        out_specs=pl.BlockSpec((1, ts, d), lambda bi, i: (bi, i, 0)),
        out_shape=jax.ShapeDtypeStruct((b, s, d), F32),
        compiler_params=_params(("parallel", "parallel")),
    )(x, w.reshape(1, d))


def _resid_epilogue(accs, e_refs, o_refs, *, coef):
    x_ref, g_ref = e_refs
    o_refs[0][...] = x_ref[...] + (coef * g_ref[0]) * accs[0]


def _ws_body(*refs, nx, nwt, ne, no, pairs, tn, nj, epilogue):
    x_refs = refs[:nx]
    w_hbm = refs[nx:nx + nwt]
    e_refs = refs[nx + nwt:nx + nwt + ne]
    o_refs = refs[nx + nwt + ne:nx + nwt + ne + no]
    stage, wbf, sems = refs[nx + nwt + ne + no:]
    j = pl.program_id(0)
    i = pl.program_id(1)

    def tile_copy(jj, p):
        _, wi, lead, col = pairs[p]
        src = w_hbm[wi].at[(*lead, slice(None), pl.ds(pl.multiple_of(col + jj * tn, tn), tn))]
        return pltpu.make_async_copy(src, stage.at[p], sems.at[p])

    @pl.when(i == 0)
    def _():
        @pl.when(j == 0)
        def _():
            for p in range(len(pairs)):
                tile_copy(j, p).start()

        for p in range(len(pairs)):
            tile_copy(j, p).wait()
            wbf[p] = stage[p].astype(BF16)

        @pl.when(j + 1 < nj)
        def _():
            for p in range(len(pairs)):
                tile_copy(j + 1, p).start()

    accs = [jnp.dot(x_refs[xi][...], wbf[p], preferred_element_type=F32)
            for p, (xi, _, _, _) in enumerate(pairs)]
    epilogue(accs, e_refs, o_refs)


def _ws_mm(xs, wts, pairs, extras, outs, epilogue, *, tm, tn, nj):
    m, kdim = xs[0].shape
    assert all(c % tn == 0 for _, _, _, c in pairs)
    in_specs = [pl.BlockSpec((tm, kdim), lambda j, i: (i, 0)) for _ in xs]
    in_specs += [pl.BlockSpec(memory_space=pl.ANY) for _ in wts]
    in_specs += [pl.BlockSpec(blk, lambda j, i, f=f: f(i, j)) for _, blk, f in extras]
    out_specs = [pl.BlockSpec(blk, lambda j, i, f=f: f(i, j)) for _, _, blk, f in outs]
    out_shape = [jax.ShapeDtypeStruct(sh, dt) for sh, dt, _, _ in outs]
    body = functools.partial(_ws_body, nx=len(xs), nwt=len(wts), ne=len(extras), no=len(outs),
                             pairs=pairs, tn=tn, nj=nj, epilogue=epilogue)
    return pl.pallas_call(
        body, grid=(nj, m // tm), in_specs=in_specs, out_specs=out_specs, out_shape=out_shape,
        scratch_shapes=[pltpu.VMEM((len(pairs), kdim, tn), F32), pltpu.VMEM((len(pairs), kdim, tn), BF16),
                        pltpu.SemaphoreType.DMA((len(pairs),))],
        compiler_params=_params(("arbitrary", "arbitrary")),
    )(*xs, *wts, *[a for a, _, _ in extras])


def _wst_body(*refs, ne, no, lay, tn, nj, segs, parts, epilogue):
    x_ref, wt_hbm = refs[0], refs[1]
    e_refs = refs[2:2 + ne]
    o_refs = refs[2 + ne:2 + ne + no]
    stage, wbf, sems = refs[2 + ne + no:]
    j = pl.program_id(0)
    i = pl.program_id(1)

    def copies(jj):
        if parts is not None:
            return [pltpu.make_async_copy(wt_hbm.at[lay, pl.ds(src, n), :], stage.at[pl.ds(dst, n), :],
                                          sems.at[c]) for c, (src, n, dst) in enumerate(parts)]
        row = segs[0][1] + jj * tn
        for first_tile, row0 in segs[1:]:
            row = jnp.where(jj >= first_tile, row0 + (jj - first_tile) * tn, row)
        src = wt_hbm.at[lay, pl.ds(pl.multiple_of(row, SUBLANES), tn), :]
        return [pltpu.make_async_copy(src, stage, sems.at[0])]

    @pl.when(i == 0)
    def _():
        @pl.when(j == 0)
        def _():
            if parts is not None:
                covered = sorted((dst, dst + n) for _, n, dst in parts)
                for lo, hi in zip([0] + [b for _, b in covered], [a for a, _ in covered] + [tn]):
                    if hi > lo:
                        stage[lo:hi, :] = jnp.zeros((hi - lo, stage.shape[1]), F32)
            for cp in copies(j):
                cp.start()

        for cp in copies(j):
            cp.wait()
        wbf[...] = stage[...].astype(BF16)

        @pl.when(j + 1 < nj)
        def _():
            for cp in copies(j + 1):
                cp.start()

    acc = lax.dot_general(x_ref[...], wbf[...], (((1,), (1,)), ((), ())), preferred_element_type=F32)
    epilogue([acc], e_refs, o_refs)


def _wst_mm(x, wt, lay, extras, outs, epilogue, *, tm, tn, nj, segs=None, parts=None):
    m, kdim = x.shape
    assert (segs is None) != (parts is None) and (parts is None or nj == 1)
    in_specs = [pl.BlockSpec((tm, kdim), lambda j, i: (i, 0)), pl.BlockSpec(memory_space=pl.ANY)]
    in_specs += [pl.BlockSpec(blk, lambda j, i, f=f: f(i, j)) for _, blk, f in extras]
    out_specs = [pl.BlockSpec(blk, lambda j, i, f=f: f(i, j)) for _, _, blk, f in outs]
    out_shape = [jax.ShapeDtypeStruct(sh, dt) for sh, dt, _, _ in outs]
    body = functools.partial(_wst_body, ne=len(extras), no=len(outs), lay=lay, tn=tn, nj=nj,
                             segs=segs, parts=parts, epilogue=epilogue)
    return pl.pallas_call(
        body, grid=(nj, m // tm), in_specs=in_specs, out_specs=out_specs, out_shape=out_shape,
        scratch_shapes=[pltpu.VMEM((tn, kdim), F32), pltpu.VMEM((tn, kdim), BF16),
                        pltpu.SemaphoreType.DMA((len(parts) if parts else 1,))],
        compiler_params=_params(("arbitrary", "arbitrary")),
    )(x, wt, *[a for a, _, _ in extras])


def _swiglu_epilogue(accs, e_refs, o_refs):
    g, u = accs
    o_refs[0][...] = (g * _sigmoid(g) * u).astype(BF16)


def _resid_mm(a, w, lay, x2d, gate, coef, seq, *, tm, tn):
    m, n = x2d.shape
    per_b = seq // tm
    return _ws_mm([a], [w], [(0, 0, (lay,), 0)],
                  [(x2d, (tm, tn), lambda i, j: (i, j)),
                   (gate.reshape(-1, 1, n), (1, 1, tn), lambda i, j: (i // per_b, 0, j))],
                  [((m, n), F32, (tm, tn), lambda i, j: (i, j))],
                  functools.partial(_resid_epilogue, coef=coef), tm=tm, tn=tn, nj=n // tn)[0]


def _swiglu_mm(h2, w_in, lay, *, tm=1024, tn=512):
    m, _ = h2.shape
    return _ws_mm([h2], [w_in], [(0, 0, (lay,), 0), (0, 0, (lay,), D_FF)], [],
                  [((m, D_FF), BF16, (tm, tn), lambda i, j: (i, j))],
                  _swiglu_epilogue, tm=tm, tn=tn, nj=D_FF // tn)[0]


def _ffn(x, h, w_in, w_out, lay, gate):
    b, s, d = x.shape
    m = b * s
    a = _swiglu_mm(h.reshape(m, d), w_in, lay)
    y = _resid_mm(a, w_out, lay, x.reshape(m, d), gate, 0.5, s, tm=512, tn=512)
    return y.reshape(b, s, d)


def _rope_tab_kernel(p_ref, inv_ref, cos_ref, sina_ref, sinb_ref):
    ang = p_ref[...].astype(F32) * inv_ref[...]
    lane = lax.broadcasted_iota(jnp.int32, ang.shape, 1)
    half = MLA_ROPE // 2
    c = jnp.cos(ang)
    s = jnp.sin(ang)
    cos_ref[...] = jnp.where(lane < MLA_ROPE, c, 0.0)
    sina_ref[...] = jnp.where(lane < half, -s, 0.0)
    sinb_ref[...] = jnp.where(lane >= half, jnp.where(lane < MLA_ROPE, s, 0.0), 0.0)


def _rope_tables(positions, ts=512):
    m = positions.size
    half = MLA_ROPE // 2
    inv = ROPE_THETA ** (-jnp.arange(0, MLA_ROPE, 2, dtype=F32) / MLA_ROPE)
    inv128 = jnp.concatenate([inv, inv, jnp.zeros((LANES - 2 * half,), F32)]).reshape(1, LANES)
    spec = pl.BlockSpec((ts, LANES), lambda i: (i, 0))
    shp = jax.ShapeDtypeStruct((m, LANES), F32)
    return pl.pallas_call(
        _rope_tab_kernel, grid=(m // ts,),
        in_specs=[pl.BlockSpec((ts, 1), lambda i: (i, 0)), pl.BlockSpec((1, LANES), lambda i: (0, 0))],
        out_specs=[spec, spec, spec], out_shape=[shp, shp, shp],
        compiler_params=_params(("parallel",)),
    )(positions.reshape(m, 1), inv128)


def _plain_epilogue(accs, e_refs, o_refs):
    o_refs[0][...] = accs[0].astype(o_refs[0].dtype)


def _sigmoid_epilogue(accs, e_refs, o_refs):
    o_refs[0][...] = _sigmoid(accs[0]).astype(o_refs[0].dtype)


def _rmsw_epilogue(accs, e_refs, o_refs):
    a = accs[0]
    y = a * lax.rsqrt(jnp.mean(a * a, axis=-1, keepdims=True) + EPS) * e_refs[0][0]
    o_refs[0][...] = y.astype(o_refs[0].dtype)


def _kvlatent_epilogue(accs, e_refs, o_refs):
    a = accs[0]
    w_ref, cos_ref, sina_ref, sinb_ref = e_refs
    ckv = a[:, :MLA_KV_RANK]
    y = ckv * lax.rsqrt(jnp.mean(ckv * ckv, axis=-1, keepdims=True) + EPS) * w_ref[0]
    o_refs[0][...] = y.astype(BF16)
    kr = a[:, MLA_KV_RANK:MLA_KV_RANK + LANES]
    o_refs[1][...] = _rope128(kr, cos_ref[...], sina_ref[...], sinb_ref[...]).astype(BF16)
    o_refs[2][...] = a[:, MLA_KV_RANK + LANES:]


def _qrope_epilogue(accs, e_refs, o_refs, *, heads_per_tile):
    a = accs[0] * Q_SCALE_LOG2
    cos_ref, sina_ref, sinb_ref = e_refs
    cos, sina, sinb = cos_ref[...], sina_ref[...], sinb_ref[...]
    for c in range(heads_per_tile):
        sl = slice(c * LANES, (c + 1) * LANES)
        o_refs[0][:, sl] = _rope128(a[:, sl], cos, sina, sinb).astype(BF16)


def _scaled_epilogue(accs, e_refs, o_refs, *, scale):
    o_refs[0][...] = (accs[0] * scale).astype(o_refs[0].dtype)


def _band(ts, lo, hi, first_tile):
    t = lax.broadcasted_iota(jnp.int32, (ts, HALO + ts), 0)
    s = lax.broadcasted_iota(jnp.int32, (ts, HALO + ts), 1)
    d = t + HALO - s
    ok = jnp.where(d >= lo, jnp.where(d < hi, 1.0, 0.0), 0.0)
    ok = jnp.where(s < HALO, jnp.where(first_tile, 0.0, ok), ok)
    return ok.astype(BF16)


def _pool_kernel(u_ref, halo_ref, pw_ref, ps_ref, o_ref, *, ts):
    i = pl.program_id(1)
    u = u_ref[0]
    ucat = jnp.concatenate([halo_ref[0], u], axis=0)
    tg = i * ts + lax.broadcasted_iota(jnp.int32, (ts, 1), 0)
    for g, w in enumerate(POOL_WINDOWS):
        sl = slice(g * POOL_GW, (g + 1) * POOL_GW)
        win = jnp.dot(_band(ts, 0, w, i == 0), ucat[:, sl], preferred_element_type=F32)
        cnt = jnp.minimum(tg + 1, w).astype(F32)
        p = win / cnt - u[:, sl].astype(F32)
        y = jnp.dot(p.astype(BF16), pw_ref[0, g], preferred_element_type=F32)
        o_ref[0, :, sl] = (y * ps_ref[0, :, sl]).astype(BF16)


def _pool(zp, pool_w, pool_scale, lay, b, s, ts=256):
    hb = ts // HALO
    ng = len(POOL_WINDOWS)
    return pl.pallas_call(
        functools.partial(_pool_kernel, ts=ts), grid=(b, s // ts),
        in_specs=[pl.BlockSpec((1, ts, MIX_W), lambda bi, i: (bi, i, 0)),
                  pl.BlockSpec((1, HALO, MIX_W), lambda bi, i: (bi, jnp.maximum(i * hb - 1, 0), 0)),
                  pl.BlockSpec((1, ng, POOL_GW, POOL_GW), lambda bi, i: (lay, 0, 0, 0)),
                  pl.BlockSpec((1, 1, MIX_W), lambda bi, i: (lay, 0, 0))],
        out_specs=pl.BlockSpec((1, ts, MIX_W), lambda bi, i: (bi, i, 0)),
        out_shape=jax.ShapeDtypeStruct((b, s, MIX_W), BF16),
        compiler_params=_params(("parallel", "parallel")),
    )(zp, zp, pool_w, pool_scale.reshape(DEPTH, 1, MIX_W))


def _conv_kernel(x_ref, halo_ref, w_ref, b_ref, sc_ref, o_ref, *, ts):
    i = pl.program_id(1)
    x = x_ref[0]
    xcat = jnp.concatenate([halo_ref[0], x], axis=0)
    w = w_ref[0]
    acc = x.astype(F32) * w[CONV_W - 1:CONV_W, :] + b_ref[0]
    for d in range(1, CONV_W):
        xs = jnp.dot(_band(ts, d, d + 1, i == 0), xcat, preferred_element_type=F32)
        acc = acc + xs * w[CONV_W - 1 - d:CONV_W - d, :]
    o_ref[0] = (acc * jax.nn.sigmoid(acc) * sc_ref[...]).astype(BF16)


def _conv_silu(zp, conv_w, conv_b, lay, b, s, ts=128):
    c = 2 * ML_HEADS * ML_DK
    hb = ts // HALO
    post = jnp.concatenate([jnp.full((c // 2,), ML_DK ** -0.5, F32), jnp.ones((c // 2,), F32)])
    return pl.pallas_call(
        functools.partial(_conv_kernel, ts=ts), grid=(b, s // ts),
        in_specs=[pl.BlockSpec((1, ts, c), lambda bi, i: (bi, i, 1)),
                  pl.BlockSpec((1, HALO, c), lambda bi, i: (bi, jnp.maximum(i * hb - 1, 0), 1)),
                  pl.BlockSpec((1, CONV_W, c), lambda bi, i: (lay, 0, 0)),
                  pl.BlockSpec((1, 1, c), lambda bi, i: (lay, 0, 0)),
                  pl.BlockSpec((1, c), lambda bi, i: (0, 0))],
        out_specs=pl.BlockSpec((1, ts, c), lambda bi, i: (bi, i, 0)),
        out_shape=jax.ShapeDtypeStruct((b, s, c), BF16),
        compiler_params=_params(("parallel", "parallel")),
    )(zp, zp, conv_w, conv_b.reshape(DEPTH, 1, c), post.reshape(1, c))


ATTN_HEADS_PER_STEP = 8


def _attn_kernel(qn_ref, qr_ref, kn_ref, kr_ref, v_ref, pq_ref, pk_ref, o_ref, m_sc, l_sc, acc_sc, qt_sc, vt_sc,
                 *, tq):
    i = pl.program_id(2)
    hs = [slice(g * LANES, (g + 1) * LANES) for g in range(ATTN_HEADS_PER_STEP)]

    for g, sl in enumerate(hs):
        qt_sc[g, :LANES, :] = qn_ref[0, :, sl].astype(F32).T.astype(BF16)
        qt_sc[g, LANES:, :] = qr_ref[0, :, sl].astype(F32).T.astype(BF16)

    @pl.when(i == 0)
    def _():
        for g, sl in enumerate(hs):
            vt_sc[g] = v_ref[0, :, sl].astype(F32).T.astype(BF16)

    def block(start, mask, first):
        kr = kr_ref[0, pl.ds(start, tq), :]

        def scores(g):
            k = jnp.concatenate([kn_ref[0, pl.ds(start, tq), hs[g]], kr], axis=-1)
            st = jnp.dot(k, qt_sc[g], preferred_element_type=F32)
            if mask is not None:
                st = jnp.where(mask, st, -jnp.inf)
            return st

        st_next = scores(0)
        for g, sl in enumerate(hs):
            st = st_next
            if g + 1 < len(hs):
                st_next = scores(g + 1)
            vt = vt_sc[g, :, pl.ds(start, tq)]
            smax = jnp.max(st, axis=0, keepdims=True)
            if first:
                m_new = smax
                p = jnp.exp2(st - m_new)
                l_sc[g] = jnp.sum(p, axis=0, keepdims=True)
                acc_sc[g] = jnp.dot(vt, p.astype(BF16), preferred_element_type=F32)
            else:
                m_old = m_sc[g]
                m_new = jnp.maximum(m_old, smax)
                alpha = jnp.exp2(m_old - m_new)
                p = jnp.exp2(st - m_new)
                l_sc[g] = alpha * l_sc[g] + jnp.sum(p, axis=0, keepdims=True)
                acc_sc[g] = alpha * acc_sc[g] + jnp.dot(vt, p.astype(BF16), preferred_element_type=F32)
            m_sc[g] = m_new

    sh = CHUNK.bit_length() - 1
    mask = lax.shift_right_arithmetic(pk_ref[0], sh) <= lax.shift_right_arithmetic(pq_ref[0], sh)
    block(pl.multiple_of(i * tq, tq), mask, True)

    def body(j, carry):
        block(pl.multiple_of(j * tq, tq), None, False)
        return carry

    lax.fori_loop(0, i, body, 0)
    for g, sl in enumerate(hs):
        o_ref[0, :, sl] = (acc_sc[g] / l_sc[g]).T.astype(o_ref.dtype)


def _attention(q, kv, kr, positions, tq=512):
    b, s, _ = q.shape
    G = ATTN_HEADS_PER_STEP
    w = G * LANES
    ng = MLA_HEADS // G
    return pl.pallas_call(
        functools.partial(_attn_kernel, tq=tq), grid=(b, ng, s // tq),
        in_specs=[pl.BlockSpec((1, tq, w), lambda bi, hi, i: (bi, i, hi)),
                  pl.BlockSpec((1, tq, w), lambda bi, hi, i: (bi, i, ng + hi)),
                  pl.BlockSpec((1, s, w), lambda bi, hi, i: (bi, 0, hi)),
                  pl.BlockSpec((1, s, LANES), lambda bi, hi, i: (bi, 0, 0)),
                  pl.BlockSpec((1, s, w), lambda bi, hi, i: (bi, 0, ng + hi)),
                  pl.BlockSpec((1, 1, tq), lambda bi, hi, i: (bi, 0, i)),
                  pl.BlockSpec((1, tq, 1), lambda bi, hi, i: (bi, i, 0))],
        out_specs=pl.BlockSpec((1, tq, w), lambda bi, hi, i: (bi, i, hi)),
        out_shape=jax.ShapeDtypeStruct((b, s, MLA_HEADS * MLA_V), BF16),
        scratch_shapes=[pltpu.VMEM((G, 1, tq), F32), pltpu.VMEM((G, 1, tq), F32),
                        pltpu.VMEM((G, MLA_V, tq), F32),
                        pltpu.VMEM((G, 2 * LANES, tq), BF16), pltpu.VMEM((G, MLA_V, s), BF16)],
        compiler_params=_params(("parallel", "parallel", "arbitrary")),
    )(q, q, kv, kr, kv, positions.reshape(b, 1, s), positions.reshape(b, s, 1))


def _split3_dot(a_bf16, x):
    hi = x.astype(BF16)
    r1 = x - hi.astype(F32)
    mid = r1.astype(BF16)
    lo = (r1 - mid.astype(F32)).astype(BF16)
    return (jnp.dot(a_bf16, hi, preferred_element_type=F32)
            + jnp.dot(a_bf16, mid, preferred_element_type=F32)
            + jnp.dot(a_bf16, lo, preferred_element_type=F32))


def _mlstm_kernel(qk_ref, v_ref, g_ref, og_ref, gb_ref, hn_ref, o_ref, ct_ref, m_ref):
    cidx = pl.program_id(1)

    @pl.when(cidx == 0)
    def _():
        ct_ref[...] = jnp.zeros_like(ct_ref)
        m_ref[...] = jnp.zeros_like(m_ref)

    L = CHUNK
    g = g_ref[0]
    gb = gb_ref[...]
    logi = g[:, :LANES] + gb[:, :LANES]
    xf = g[:, LANES:] + gb[:, LANES:]
    logf = jnp.minimum(xf, 0.0) - jnp.log1p(jnp.exp(-jnp.abs(xf)))
    row = lax.broadcasted_iota(jnp.int32, (L, L), 0)
    col = lax.broadcasted_iota(jnp.int32, (L, L), 1)
    tril = row >= col
    bcum = _split3_dot(jnp.where(tril, 1.0, 0.0).astype(BF16), logf)
    x = logi - bcum
    xt = x.T
    m_row = m_ref[...]
    b_last = bcum[L - 1:L, :]
    m_new = jnp.maximum(b_last + m_row, b_last + jnp.max(x, axis=0, keepdims=True))
    decay = jnp.exp(b_last + m_row - m_new)
    ws_all = jnp.exp(b_last + x - m_new)
    inter_all = bcum + m_row
    ones_col = jnp.ones((L, LANES), BF16)
    nt = (((1,), (1,)), ((), ()))
    tn = (((0,), (0,)), ((), ()))

    def early(h):
        q = qk_ref[0, :, h * ML_DK:(h + 1) * ML_DK]
        k = qk_ref[0, :, (ML_HEADS + h) * ML_DK:(ML_HEADS + h + 1) * ML_DK]
        ct = ct_ref[h]
        qk = lax.dot_general(q, k, nt, preferred_element_type=F32)
        qc = jnp.dot(q, ct.astype(BF16), preferred_element_type=F32)
        dlog = jnp.where(tril, bcum[:, h:h + 1] + xt[h:h + 1, :], -jnp.inf)
        inter = inter_all[:, h:h + 1]
        mt = jnp.maximum(inter, jnp.max(dlog, axis=-1, keepdims=True))
        return k, ct, qk, qc, mt, jnp.exp(inter - mt), jnp.exp(dlog - mt)

    ahead = early(0)
    hh_all = []
    for h in range(ML_HEADS):
        k, ct, qk, qc, mt, w_inter, e = ahead
        if h + 1 < ML_HEADS:
            ahead = early(h + 1)
        vaug = jnp.concatenate([v_ref[0, :, h * ML_DV:(h + 1) * ML_DV], ones_col], axis=-1)
        num = jnp.dot((qk * e).astype(BF16), vaug, preferred_element_type=F32) + w_inter * qc
        den = jnp.maximum(jnp.abs(num[:, ML_DV:]), jnp.exp(-mt))
        hh_all.append(num[:, :ML_DV] / jnp.concatenate([den, den], axis=-1))
        wv = (ws_all[:, h:h + 1] * vaug.astype(F32)).astype(BF16)
        ct_ref[h] = decay[:, h:h + 1] * ct + lax.dot_general(k, wv, tn, preferred_element_type=F32)
    for h, hh in enumerate(hh_all):
        hs = slice(h * ML_DV, (h + 1) * ML_DV)
        hn = hh * lax.rsqrt(jnp.mean(hh * hh, axis=-1, keepdims=True) + EPS) * hn_ref[0, :, hs]
        o_ref[0, :, hs] = (og_ref[0, :, hs].astype(F32) * hn).astype(BF16)
    m_ref[...] = m_new


def _mlstm(qk, zp, gates, zs, gate_b, head_norm, lay):
    b, s, _ = qk.shape
    gb = jnp.zeros((1, 2 * LANES), F32)
    gb = gb.at[0, :ML_HEADS].set(gate_b[:ML_HEADS]).at[0, LANES:LANES + ML_HEADS].set(gate_b[ML_HEADS:])
    blk = lambda col: pl.BlockSpec((1, CHUNK, MIX_W), lambda bi, c, col=col: (bi, c, col))
    return pl.pallas_call(
        _mlstm_kernel, grid=(b, s // CHUNK),
        in_specs=[blk(0), blk(2),
                  pl.BlockSpec((1, CHUNK, 2 * LANES), lambda bi, c: (bi, c, 0)),
                  blk(0),
                  pl.BlockSpec((1, 2 * LANES), lambda bi, c: (0, 0)),
                  pl.BlockSpec((1, 1, MIX_W), lambda bi, c: (lay, 0, 0))],
        out_specs=blk(0),
        out_shape=jax.ShapeDtypeStruct((b, s, MIX_W), BF16),
        scratch_shapes=[pltpu.VMEM((ML_HEADS, ML_DK, ML_DV + LANES), F32), pltpu.VMEM((1, LANES), F32)],
        compiler_params=_params(("parallel", "arbitrary")),
    )(qk, zp, gates, zs, gb, head_norm.reshape(DEPTH, 1, MIX_W))


def _merge_epilogue(accs, e_refs, o_refs):
    acc = e_refs[0][...].astype(F32) * accs[0]
    acc += e_refs[1][...].astype(F32) * accs[1]
    acc += e_refs[2][...].astype(F32) * accs[2]
    o_refs[0][...] = acc.astype(BF16)


def _merge(ya, yb, yc, w_branch, lay, zs, tm=1024, tn=512):
    m = ya.shape[0]
    goff = MIX_W // tn
    gate = lambda k: (zs, (tm, tn), lambda i, j, k=k: (i, goff + k * (D_MODEL // tn) + j))
    return _ws_mm([ya, yb, yc], [w_branch], [(k, 0, (lay, k), 0) for k in range(N_BRANCH)],
                  [gate(k) for k in range(N_BRANCH)],
                  [((m, D_MODEL), BF16, (tm, tn), lambda i, j: (i, j))],
                  _merge_epilogue, tm=tm, tn=tn, nj=D_MODEL // tn)[0]


def _prep_mixer_weights(w_mix_in, mla_w_uq, mla_w_ukv):
    w_t = jnp.swapaxes(w_mix_in, 1, 2)
    uq = mla_w_uq.reshape(DEPTH, MLA_Q_RANK, MLA_HEADS, MLA_NOPE + MLA_ROPE)
    w_qn = uq[..., :MLA_NOPE].reshape(DEPTH, MLA_Q_RANK, MLA_HEADS * MLA_NOPE)
    w_qr = jnp.pad(uq[..., MLA_NOPE:], ((0, 0), (0, 0), (0, 0), (0, LANES - MLA_ROPE)))
    w_q = jnp.concatenate([w_qn, w_qr.reshape(DEPTH, MLA_Q_RANK, MLA_HEADS * LANES)], axis=-1).astype(BF16)
    ukv = mla_w_ukv.reshape(DEPTH, MLA_KV_RANK, MLA_HEADS, 2, MLA_NOPE)
    w_kv = ukv.transpose(0, 1, 3, 2, 4).reshape(DEPTH, MLA_KV_RANK, 2 * MLA_HEADS * MLA_NOPE).astype(BF16)
    return w_t, w_q, w_kv


def _mixer(h, lay, positions, rope_tabs, w_t, w_q, w_kv, pool_w, pool_scale, mla_q_norm, mla_kv_norm,
           ml_conv_w, ml_conv_b, ml_gate_b, ml_head_norm, w_branch, tm=1024):
    b, s, d = h.shape
    m = b * s
    h2 = h.reshape(m, d)
    cos, sina, sinb = rope_tabs
    tab = lambda a: (a, (tm, LANES), lambda i, j: (i, 0))
    tile = lambda i, j: (i, j)
    layvec = lambda a: (a.reshape(DEPTH, 1, -1), (1, 1, a.shape[-1]), lambda i, j: (lay, 0, 0))

    o = _OFF
    wide = 1024
    zp = _wst_mm(h2, w_t, lay, [], [((m, N_PLAIN), BF16, (tm, wide), tile)], _plain_epilogue,
                 tm=tm, tn=wide, nj=N_PLAIN // wide, segs=((0, o[0]), (MIX_W // wide, o[4])))[0]
    zs = _wst_mm(h2, w_t, lay, [], [((m, N_SIG), BF16, (tm, wide), tile)], _sigmoid_epilogue,
                 tm=tm, tn=wide, nj=N_SIG // wide, segs=((0, o[7]), (MIX_W // wide, o[10])))[0]
    cqn = _wst_mm(h2, w_t, lay, [layvec(mla_q_norm)], [((m, MLA_Q_RANK), BF16, (tm, MLA_Q_RANK), tile)],
                  _rmsw_epilogue, tm=tm, tn=MLA_Q_RANK, nj=1, segs=((0, o[1]),))[0]
    ckvn, kr, gates = _wst_mm(
        h2, w_t, lay, [layvec(mla_kv_norm), tab(cos), tab(sina), tab(sinb)],
        [((m, MLA_KV_RANK), BF16, (tm, MLA_KV_RANK), tile),
         ((m, LANES), BF16, (tm, LANES), tile),
         ((m, 2 * LANES), F32, (tm, 2 * LANES), tile)],
        _kvlatent_epilogue, tm=tm, tn=N_LAT, nj=1,
        parts=((o[2], MLA_KV_RANK, 0), (o[3], MLA_ROPE, MLA_KV_RANK),
               (o[8], ML_HEADS, MLA_KV_RANK + LANES), (o[9], ML_HEADS, MLA_KV_RANK + 2 * LANES)))

    zp3 = zp.reshape(b, s, N_PLAIN)
    ya = _pool(zp3, pool_w, pool_scale, lay, b, s)

    nq = MLA_HEADS * LANES
    q = _mm(cqn, [(w_q, lay, 0)], [tab(cos), tab(sina), tab(sinb)], [((m, 2 * nq), BF16, (tm, nq), tile)],
            functools.partial(_q_epilogue, heads_per_tile=MLA_HEADS, n_nope_tiles=1),
            tm=tm, tn=nq, nj=2)[0]
    kv = _mm(ckvn, [(w_kv, lay, 0)], [], [((m, 2 * nq), BF16, (tm, nq), tile)],
             _plain_epilogue, tm=tm, tn=nq, nj=2)[0]
    yb = _attention(q.reshape(b, s, 2 * nq), kv.reshape(b, s, 2 * nq), kr.reshape(b, s, LANES), positions)

    qk = _conv_silu(zp3, ml_conv_w, ml_conv_b, lay, b, s)
    yc = _mlstm(qk, zp3, gates.reshape(b, s, 2 * LANES), zs.reshape(b, s, N_SIG), ml_gate_b[lay],
                ml_head_norm, lay)

    return _merge(ya.reshape(m, MIX_W), yb.reshape(m, MIX_W), yc.reshape(m, MIX_W), w_branch, lay, zs)


def _q_epilogue(accs, e_refs, o_refs, *, heads_per_tile, n_nope_tiles):
    j = pl.program_id(1)

    @pl.when(j < n_nope_tiles)
    def _():
        _scaled_epilogue(accs, (), o_refs, scale=Q_SCALE_LOG2)

    @pl.when(j >= n_nope_tiles)
    def _():
        _qrope_epilogue(accs, e_refs, o_refs, heads_per_tile=heads_per_tile)


def kernel(x, c, positions, w_ada, b_ada, ada_table, ffn_a_w_in, ffn_a_w_out, w_mix_in, pool_w, pool_scale, mla_q_norm, mla_w_uq, mla_kv_norm, mla_w_ukv, ml_conv_w, ml_conv_b, ml_gate_b, ml_head_norm, w_branch, w_out, ffn_b_w_in, ffn_b_w_out, final_norm):
    b, s, d = x.shape
    m = b * s
    mod = _ada(c, w_ada, b_ada, ada_table)
    rope_tabs = _rope_tables(positions)
    w_t, w_q, w_kv = _prep_mixer_weights(w_mix_in, mla_w_uq, mla_w_ukv)
    pool_wb = pool_w.astype(BF16)
    for l in range(DEPTH):
        md = mod[l]
        h = _normmod(x, md[:, 0], md[:, 1])
        x = _ffn(x, h, ffn_a_w_in, ffn_a_w_out, l, md[:, 2])
        h = _normmod(x, md[:, 3], md[:, 4])
        merged = _mixer(h, l, positions, rope_tabs, w_t, w_q, w_kv, pool_wb, pool_scale, mla_q_norm,
                        mla_kv_norm, ml_conv_w, ml_conv_b, ml_gate_b, ml_head_norm, w_branch)
        x = _resid_mm(merged, w_out, l, x.reshape(m, d), md[:, 5], 1.0, s, tm=512, tn=1024).reshape(b, s, d)
        h = _normmod(x, md[:, 6], md[:, 7])
        x = _ffn(x, h, ffn_b_w_in, ffn_b_w_out, l, md[:, 8])
    return _finalnorm(x, final_norm)
```

```python
import functools

import numpy as np
import jax
import jax.numpy as jnp
from jax import lax
from jax.experimental import pallas as pl
from jax.experimental.pallas import tpu as pltpu

F32 = jnp.float32
BF16 = jnp.bfloat16

D_MODEL = 4096
DEPTH = 2
CHUNK = 64
EPS = 1e-6
D_FF = 2 * D_MODEL
MIX_W = D_MODEL // 2
N_BRANCH = 3
N_MOD = 9
POOL_WINDOWS = (2, 4, 8, 16)
POOL_GW = MIX_W // len(POOL_WINDOWS)
MLA_NOPE = 128
MLA_ROPE = 64
MLA_V = 128
MLA_HEADS = MIX_W // MLA_V
MLA_Q_RANK = D_MODEL // 4
MLA_KV_RANK = 512
MLA_SCALE = (MLA_NOPE + MLA_ROPE) ** -0.5
Q_SCALE_LOG2 = MLA_SCALE * float(np.log2(np.e))
ROPE_THETA = 10000.0
ML_HEADS = 8
ML_DK = 128
ML_DV = MIX_W // ML_HEADS
CONV_W = 4

_SPLITS = (MIX_W, MLA_Q_RANK, MLA_KV_RANK, MLA_ROPE, ML_HEADS * ML_DK, ML_HEADS * ML_DK,
           ML_HEADS * ML_DV, ML_HEADS * ML_DV, ML_HEADS, ML_HEADS, N_BRANCH * D_MODEL)
_OFF = tuple(int(v) for v in np.cumsum((0,) + _SPLITS))

LANES = 128
SUBLANES = 8
HALO = 16
VMEM_LIMIT = 60 * 1024 * 1024

N_PLAIN = 3 * MIX_W
N_SIG = MIX_W + N_BRANCH * D_MODEL
N_LAT = MLA_KV_RANK + 3 * LANES


def _params(sem):
    return pltpu.CompilerParams(dimension_semantics=sem, vmem_limit_bytes=VMEM_LIMIT)


def _mm_body(*refs, nw, ne, no, epilogue):
    x_ref = refs[0]
    w_refs = refs[1:1 + nw]
    e_refs = refs[1 + nw:1 + nw + ne]
    o_refs = refs[1 + nw + ne:1 + nw + ne + no]
    accs = [jnp.dot(x_ref[...], w[0], preferred_element_type=F32) for w in w_refs]
    epilogue(accs, e_refs, o_refs)


def _mm(x, ws, extras, outs, epilogue, *, tm, tn, nj):
    m, kdim = x.shape
    in_specs = [pl.BlockSpec((tm, kdim), lambda i, j: (i, 0))]
    for _, lay, col in ws:
        assert col % tn == 0
        in_specs.append(pl.BlockSpec((1, kdim, tn), lambda i, j, lay=lay, off=col // tn: (lay, 0, off + j)))
    for _, blk, f in extras:
        in_specs.append(pl.BlockSpec(blk, f))
    out_specs = [pl.BlockSpec(blk, f) for _, _, blk, f in outs]
    out_shape = [jax.ShapeDtypeStruct(s, d) for s, d, _, _ in outs]
    body = functools.partial(_mm_body, nw=len(ws), ne=len(extras), no=len(outs), epilogue=epilogue)
    return pl.pallas_call(
        body, grid=(m // tm, nj), in_specs=in_specs, out_specs=out_specs, out_shape=out_shape,
        compiler_params=_params(("parallel", "parallel")),
    )(x, *[w for w, _, _ in ws], *[a for a, _, _ in extras])


def _sigmoid(x):
    return 0.5 * jnp.tanh(0.5 * x) + 0.5


def _rope128(v, cos, sina, sinb):
    return (v * cos + pltpu.roll(v, LANES - MLA_ROPE // 2, 1) * sina
            + pltpu.roll(v, MLA_ROPE // 2, 1) * sinb)


def _ada_kernel(c_ref, w_ref, b_ref, t_ref, o_ref):
    c = c_ref[...]
    s = c * jax.nn.sigmoid(c)
    acc = jnp.dot(s.astype(BF16), w_ref[...].astype(BF16), preferred_element_type=F32) + b_ref[...]
    for l in range(DEPTH):
        o_ref[l] = acc + t_ref[l]


def _ada(c, w_ada, b_ada, ada_table):
    b = c.shape[0]
    rows = SUBLANES
    assert b <= rows
    c8 = jnp.zeros((rows, D_MODEL), F32).at[:b].set(c)
    n = N_MOD * D_MODEL
    tn = 1024
    out = pl.pallas_call(
        _ada_kernel, grid=(n // tn,),
        in_specs=[pl.BlockSpec((rows, D_MODEL), lambda j: (0, 0)),
                  pl.BlockSpec((D_MODEL, tn), lambda j: (0, j)),
                  pl.BlockSpec((1, tn), lambda j: (0, j)),
                  pl.BlockSpec((DEPTH, 1, tn), lambda j: (0, 0, j))],
        out_specs=pl.BlockSpec((DEPTH, rows, tn), lambda j: (0, 0, j)),
        out_shape=jax.ShapeDtypeStruct((DEPTH, rows, n), F32),
        compiler_params=_params(("parallel",)),
    )(c8, w_ada, b_ada.reshape(1, n), ada_table.reshape(DEPTH, 1, n))
    return out[:, :b].reshape(DEPTH, b, N_MOD, D_MODEL)


def _normmod_kernel(x_ref, shift_ref, scale_ref, o_ref):
    x = x_ref[0]
    y = x * lax.rsqrt(jnp.mean(x * x, axis=-1, keepdims=True) + EPS)
    o_ref[0] = (y * (1.0 + scale_ref[0]) + shift_ref[0]).astype(o_ref.dtype)


def _finalnorm_kernel(x_ref, w_ref, o_ref):
    x = x_ref[0]
    y = x * lax.rsqrt(jnp.mean(x * x, axis=-1, keepdims=True) + EPS)
    o_ref[0] = y * w_ref[...]


def _normmod(x, shift, scale, ts=512):
    b, s, d = x.shape
    vec = pl.BlockSpec((1, 1, d), lambda bi, i: (bi, 0, 0))
    return pl.pallas_call(
        _normmod_kernel, grid=(b, s // ts),
        in_specs=[pl.BlockSpec((1, ts, d), lambda bi, i: (bi, i, 0)), vec, vec],
        out_specs=pl.BlockSpec((1, ts, d), lambda bi, i: (bi, i, 0)),
        out_shape=jax.ShapeDtypeStruct((b, s, d), BF16),
        compiler_params=_params(("parallel", "parallel")),
    )(x, shift.reshape(b, 1, d), scale.reshape(b, 1, d))


def _finalnorm(x, w, ts=512):
    b, s, d = x.shape
    return pl.pallas_call(
        _finalnorm_kernel, grid=(b, s // ts),
        in_specs=[pl.BlockSpec((1, ts, d), lambda bi, i: (bi, i, 0)),
                  pl.BlockSpec((1, d), lambda bi, i: (0, 0))],
        out_specs=pl.BlockSpec((1, ts, d), lambda bi, i: (bi, i, 0)),
        out_shape=jax.ShapeDtypeStruct((b, s, d), F32),
        compiler_params=_params(("parallel", "parallel")),
    )(x, w.reshape(1, d))


def _resid_epilogue(accs, e_refs, o_refs, *, coef):
    x_ref, g_ref = e_refs
    o_refs[0][...] = x_ref[...] + (coef * g_ref[0]) * accs[0]


def _ws_body(*refs, nx, nwt, ne, no, pairs, tn, nj, epilogue):
    x_refs = refs[:nx]
    w_hbm = refs[nx:nx + nwt]
    e_refs = refs[nx + nwt:nx + nwt + ne]
    o_refs = refs[nx + nwt + ne:nx + nwt + ne + no]
    stage, wbf, sems = refs[nx + nwt + ne + no:]
    j = pl.program_id(0)
    i = pl.program_id(1)

    def tile_copy(jj, p):
        _, wi, lead, col = pairs[p]
        src = w_hbm[wi].at[(*lead, slice(None), pl.ds(pl.multiple_of(col + jj * tn, tn), tn))]
        return pltpu.make_async_copy(src, stage.at[p], sems.at[p])

    @pl.when(i == 0)
    def _():
        @pl.when(j == 0)
        def _():
            for p in range(len(pairs)):
                tile_copy(j, p).start()

        for p in range(len(pairs)):
            tile_copy(j, p).wait()
            wbf[p] = stage[p].astype(BF16)

        @pl.when(j + 1 < nj)
        def _():
            for p in range(len(pairs)):
                tile_copy(j + 1, p).start()

    accs = [jnp.dot(x_refs[xi][...], wbf[p], preferred_element_type=F32)
            for p, (xi, _, _, _) in enumerate(pairs)]
    epilogue(accs, e_refs, o_refs)


def _ws_mm(xs, wts, pairs, extras, outs, epilogue, *, tm, tn, nj):
    m, kdim = xs[0].shape
    assert all(c % tn == 0 for _, _, _, c in pairs)
    in_specs = [pl.BlockSpec((tm, kdim), lambda j, i: (i, 0)) for _ in xs]
    in_specs += [pl.BlockSpec(memory_space=pl.ANY) for _ in wts]
    in_specs += [pl.BlockSpec(blk, lambda j, i, f=f: f(i, j)) for _, blk, f in extras]
    out_specs = [pl.BlockSpec(blk, lambda j, i, f=f: f(i, j)) for _, _, blk, f in outs]
    out_shape = [jax.ShapeDtypeStruct(sh, dt) for sh, dt, _, _ in outs]
    body = functools.partial(_ws_body, nx=len(xs), nwt=len(wts), ne=len(extras), no=len(outs),
                             pairs=pairs, tn=tn, nj=nj, epilogue=epilogue)
    return pl.pallas_call(
        body, grid=(nj, m // tm), in_specs=in_specs, out_specs=out_specs, out_shape=out_shape,
        scratch_shapes=[pltpu.VMEM((len(pairs), kdim, tn), F32), pltpu.VMEM((len(pairs), kdim, tn), BF16),
                        pltpu.SemaphoreType.DMA((len(pairs),))],
        compiler_params=_params(("arbitrary", "arbitrary")),
    )(*xs, *wts, *[a for a, _, _ in extras])


def _wst_body(*refs, ne, no, lay, tn, nj, segs, parts, epilogue):
    x_ref, wt_hbm = refs[0], refs[1]
    e_refs = refs[2:2 + ne]
    o_refs = refs[2 + ne:2 + ne + no]
    stage, wbf, sems = refs[2 + ne + no:]
    j = pl.program_id(0)
    i = pl.program_id(1)

    def copies(jj):
        if parts is not None:
            return [pltpu.make_async_copy(wt_hbm.at[lay, pl.ds(src, n), :], stage.at[pl.ds(dst, n), :],
                                          sems.at[c]) for c, (src, n, dst) in enumerate(parts)]
        row = segs[0][1] + jj * tn
        for first_tile, row0 in segs[1:]:
            row = jnp.where(jj >= first_tile, row0 + (jj - first_tile) * tn, row)
        src = wt_hbm.at[lay, pl.ds(pl.multiple_of(row, SUBLANES), tn), :]
        return [pltpu.make_async_copy(src, stage, sems.at[0])]

    @pl.when(i == 0)
    def _():
        @pl.when(j == 0)
        def _():
            if parts is not None:
                covered = sorted((dst, dst + n) for _, n, dst in parts)
                for lo, hi in zip([0] + [b for _, b in covered], [a for a, _ in covered] + [tn]):
                    if hi > lo:
                        stage[lo:hi, :] = jnp.zeros((hi - lo, stage.shape[1]), F32)
            for cp in copies(j):
                cp.start()

        for cp in copies(j):
            cp.wait()
        wbf[...] = stage[...].astype(BF16)

        @pl.when(j + 1 < nj)
        def _():
            for cp in copies(j + 1):
                cp.start()

    acc = lax.dot_general(x_ref[...], wbf[...], (((1,), (1,)), ((), ())), preferred_element_type=F32)
    epilogue([acc], e_refs, o_refs)


def _wst_mm(x, wt, lay, extras, outs, epilogue, *, tm, tn, nj, segs=None, parts=None):
    m, kdim = x.shape
    assert (segs is None) != (parts is None) and (parts is None or nj == 1)
    in_specs = [pl.BlockSpec((tm, kdim), lambda j, i: (i, 0)), pl.BlockSpec(memory_space=pl.ANY)]
    in_specs += [pl.BlockSpec(blk, lambda j, i, f=f: f(i, j)) for _, blk, f in extras]
    out_specs = [pl.BlockSpec(blk, lambda j, i, f=f: f(i, j)) for _, _, blk, f in outs]
    out_shape = [jax.ShapeDtypeStruct(sh, dt) for sh, dt, _, _ in outs]
    body = functools.partial(_wst_body, ne=len(extras), no=len(outs), lay=lay, tn=tn, nj=nj,
                             segs=segs, parts=parts, epilogue=epilogue)
    return pl.pallas_call(
        body, grid=(nj, m // tm), in_specs=in_specs, out_specs=out_specs, out_shape=out_shape,
        scratch_shapes=[pltpu.VMEM((tn, kdim), F32), pltpu.VMEM((tn, kdim), BF16),
                        pltpu.SemaphoreType.DMA((len(parts) if parts else 1,))],
        compiler_params=_params(("arbitrary", "arbitrary")),
    )(x, wt, *[a for a, _, _ in extras])


def _swiglu_epilogue(accs, e_refs, o_refs):
    g, u = accs
    o_refs[0][...] = (g * _sigmoid(g) * u).astype(BF16)


def _resid_mm(a, w, lay, x2d, gate, coef, seq, *, tm, tn):
    m, n = x2d.shape
    per_b = seq // tm
    return _ws_mm([a], [w], [(0, 0, (lay,), 0)],
                  [(x2d, (tm, tn), lambda i, j: (i, j)),
                   (gate.reshape(-1, 1, n), (1, 1, tn), lambda i, j: (i // per_b, 0, j))],
                  [((m, n), F32, (tm, tn), lambda i, j: (i, j))],
                  functools.partial(_resid_epilogue, coef=coef), tm=tm, tn=tn, nj=n // tn)[0]


def _swiglu_mm(h2, w_in, lay, *, tm=1024, tn=512):
    m, _ = h2.shape
    return _ws_mm([h2], [w_in], [(0, 0, (lay,), 0), (0, 0, (lay,), D_FF)], [],
                  [((m, D_FF), BF16, (tm, tn), lambda i, j: (i, j))],
                  _swiglu_epilogue, tm=tm, tn=tn, nj=D_FF // tn)[0]


def _ffn(x, h, w_in, w_out, lay, gate):
    b, s, d = x.shape
    m = b * s
    a = _swiglu_mm(h.reshape(m, d), w_in, lay)
    y = _resid_mm(a, w_out, lay, x.reshape(m, d), gate, 0.5, s, tm=512, tn=512)
    return y.reshape(b, s, d)


def _rope_tab_kernel(p_ref, inv_ref, cos_ref, sina_ref, sinb_ref):
    ang = p_ref[...].astype(F32) * inv_ref[...]
    lane = lax.broadcasted_iota(jnp.int32, ang.shape, 1)
    half = MLA_ROPE // 2
    c = jnp.cos(ang)
    s = jnp.sin(ang)
    cos_ref[...] = jnp.where(lane < MLA_ROPE, c, 0.0)
    sina_ref[...] = jnp.where(lane < half, -s, 0.0)
    sinb_ref[...] = jnp.where(lane >= half, jnp.where(lane < MLA_ROPE, s, 0.0), 0.0)


def _rope_tables(positions, ts=512):
    m = positions.size
    half = MLA_ROPE // 2
    inv = ROPE_THETA ** (-jnp.arange(0, MLA_ROPE, 2, dtype=F32) / MLA_ROPE)
    inv128 = jnp.concatenate([inv, inv, jnp.zeros((LANES - 2 * half,), F32)]).reshape(1, LANES)
    spec = pl.BlockSpec((ts, LANES), lambda i: (i, 0))
    shp = jax.ShapeDtypeStruct((m, LANES), F32)
    return pl.pallas_call(
        _rope_tab_kernel, grid=(m // ts,),
        in_specs=[pl.BlockSpec((ts, 1), lambda i: (i, 0)), pl.BlockSpec((1, LANES), lambda i: (0, 0))],
        out_specs=[spec, spec, spec], out_shape=[shp, shp, shp],
        compiler_params=_params(("parallel",)),
    )(positions.reshape(m, 1), inv128)


def _plain_epilogue(accs, e_refs, o_refs):
    o_refs[0][...] = accs[0].astype(o_refs[0].dtype)


def _sigmoid_epilogue(accs, e_refs, o_refs):
    o_refs[0][...] = _sigmoid(accs[0]).astype(o_refs[0].dtype)


def _rmsw_epilogue(accs, e_refs, o_refs):
    a = accs[0]
    y = a * lax.rsqrt(jnp.mean(a * a, axis=-1, keepdims=True) + EPS) * e_refs[0][0]
    o_refs[0][...] = y.astype(o_refs[0].dtype)


def _kvlatent_epilogue(accs, e_refs, o_refs):
    a = accs[0]
    w_ref, cos_ref, sina_ref, sinb_ref = e_refs
    ckv = a[:, :MLA_KV_RANK]
    y = ckv * lax.rsqrt(jnp.mean(ckv * ckv, axis=-1, keepdims=True) + EPS) * w_ref[0]
    o_refs[0][...] = y.astype(BF16)
    kr = a[:, MLA_KV_RANK:MLA_KV_RANK + LANES]
    o_refs[1][...] = _rope128(kr, cos_ref[...], sina_ref[...], sinb_ref[...]).astype(BF16)
    o_refs[2][...] = a[:, MLA_KV_RANK + LANES:]


def _band(ts, lo, hi, first_tile):
    t = lax.broadcasted_iota(jnp.int32, (ts, HALO + ts), 0)
    s = lax.broadcasted_iota(jnp.int32, (ts, HALO + ts), 1)
    d = t + HALO - s
    ok = jnp.where(d >= lo, jnp.where(d < hi, 1.0, 0.0), 0.0)
    ok = jnp.where(s < HALO, jnp.where(first_tile, 0.0, ok), ok)
    return ok.astype(BF16)


def _pool_kernel(u_ref, halo_ref, pw_ref, ps_ref, o_ref, *, ts):
    i = pl.program_id(1)
    u = u_ref[0]
    ucat = jnp.concatenate([halo_ref[0], u], axis=0)
    tg = i * ts + lax.broadcasted_iota(jnp.int32, (ts, 1), 0)
    for g, w in enumerate(POOL_WINDOWS):
        sl = slice(g * POOL_GW, (g + 1) * POOL_GW)
        win = jnp.dot(_band(ts, 0, w, i == 0), ucat[:, sl], preferred_element_type=F32)
        cnt = jnp.minimum(tg + 1, w).astype(F32)
        p = win / cnt - u[:, sl].astype(F32)
        y = jnp.dot(p.astype(BF16), pw_ref[0, g], preferred_element_type=F32)
        o_ref[0, :, sl] = (y * ps_ref[0, :, sl]).astype(BF16)


def _pool(zp, pool_w, pool_scale, lay, b, s, ts=256):
    hb = ts // HALO
    ng = len(POOL_WINDOWS)
    return pl.pallas_call(
        functools.partial(_pool_kernel, ts=ts), grid=(b, s // ts),
        in_specs=[pl.BlockSpec((1, ts, MIX_W), lambda bi, i: (bi, i, 0)),
                  pl.BlockSpec((1, HALO, MIX_W), lambda bi, i: (bi, jnp.maximum(i * hb - 1, 0), 0)),
                  pl.BlockSpec((1, ng, POOL_GW, POOL_GW), lambda bi, i: (lay, 0, 0, 0)),
                  pl.BlockSpec((1, 1, MIX_W), lambda bi, i: (lay, 0, 0))],
        out_specs=pl.BlockSpec((1, ts, MIX_W), lambda bi, i: (bi, i, 0)),
        out_shape=jax.ShapeDtypeStruct((b, s, MIX_W), BF16),
        compiler_params=_params(("parallel", "parallel")),
    )(zp, zp, pool_w, pool_scale.reshape(DEPTH, 1, MIX_W))


def _conv_kernel(x_ref, halo_ref, w_ref, b_ref, sc_ref, o_ref, *, ts):
    i = pl.program_id(1)
    x = x_ref[0]
    xcat = jnp.concatenate([halo_ref[0], x], axis=0)
    w = w_ref[0]
    acc = x.astype(F32) * w[CONV_W - 1:CONV_W, :] + b_ref[0]
    for d in range(1, CONV_W):
        xs = jnp.dot(_band(ts, d, d + 1, i == 0), xcat, preferred_element_type=F32)
        acc = acc + xs * w[CONV_W - 1 - d:CONV_W - d, :]
    o_ref[0] = (acc * jax.nn.sigmoid(acc) * sc_ref[...]).astype(BF16)


def _conv_silu(zp, conv_w, conv_b, lay, b, s, ts=128):
    c = 2 * ML_HEADS * ML_DK
    hb = ts // HALO
    post = jnp.concatenate([jnp.full((c // 2,), ML_DK ** -0.5, F32), jnp.ones((c // 2,), F32)])
    return pl.pallas_call(
        functools.partial(_conv_kernel, ts=ts), grid=(b, s // ts),
        in_specs=[pl.BlockSpec((1, ts, c), lambda bi, i: (bi, i, 1)),
                  pl.BlockSpec((1, HALO, c), lambda bi, i: (bi, jnp.maximum(i * hb - 1, 0), 1)),
                  pl.BlockSpec((1, CONV_W, c), lambda bi, i: (lay, 0, 0)),
                  pl.BlockSpec((1, 1, c), lambda bi, i: (lay, 0, 0)),
                  pl.BlockSpec((1, c), lambda bi, i: (0, 0))],
        out_specs=pl.BlockSpec((1, ts, c), lambda bi, i: (bi, i, 0)),
        out_shape=jax.ShapeDtypeStruct((b, s, c), BF16),
        compiler_params=_params(("parallel", "parallel")),
    )(zp, zp, conv_w, conv_b.reshape(DEPTH, 1, c), post.reshape(1, c))


ATTN_HEADS_PER_STEP = 8


def _attn_kernel(qn_ref, qr_ref, kn_ref, kr_ref, v_ref, pq_ref, pk_ref, o_ref, m_sc, l_sc, acc_sc, qt_sc, vt_sc,
                 *, tq):
    i = pl.program_id(2)
    hs = [slice(g * LANES, (g + 1) * LANES) for g in range(ATTN_HEADS_PER_STEP)]

    for g, sl in enumerate(hs):
        qt_sc[g, :LANES, :] = qn_ref[0, :, sl].astype(F32).T.astype(BF16)
        qt_sc[g, LANES:, :] = qr_ref[0, :, sl].astype(F32).T.astype(BF16)

    @pl.when(i == 0)
    def _():
        for g, sl in enumerate(hs):
            vt_sc[g] = v_ref[0, :, sl].astype(F32).T.astype(BF16)

    def block(start, mask, first):
        kr = kr_ref[0, pl.ds(start, tq), :]

        def scores(g):
            k = jnp.concatenate([kn_ref[0, pl.ds(start, tq), hs[g]], kr], axis=-1)
            st = jnp.dot(k, qt_sc[g], preferred_element_type=F32)
            if mask is not None:
                st = jnp.where(mask, st, -jnp.inf)
            return st

        st_next = scores(0)
        for g, sl in enumerate(hs):
            st = st_next
            if g + 1 < len(hs):
                st_next = scores(g + 1)
            vt = vt_sc[g, :, pl.ds(start, tq)]
            smax = jnp.max(st, axis=0, keepdims=True)
            if first:
                m_new = smax
                p = jnp.exp2(st - m_new)
                l_sc[g] = jnp.sum(p, axis=0, keepdims=True)
                acc_sc[g] = jnp.dot(vt, p.astype(BF16), preferred_element_type=F32)
            else:
                m_old = m_sc[g]
                m_new = jnp.maximum(m_old, smax)
                alpha = jnp.exp2(m_old - m_new)
                p = jnp.exp2(st - m_new)
                l_sc[g] = alpha * l_sc[g] + jnp.sum(p, axis=0, keepdims=True)
                acc_sc[g] = alpha * acc_sc[g] + jnp.dot(vt, p.astype(BF16), preferred_element_type=F32)
            m_sc[g] = m_new

    sh = CHUNK.bit_length() - 1
    mask = lax.shift_right_arithmetic(pk_ref[0], sh) <= lax.shift_right_arithmetic(pq_ref[0], sh)
    block(pl.multiple_of(i * tq, tq), mask, True)

    def body(j, carry):
        block(pl.multiple_of(j * tq, tq), None, False)
        return carry

    lax.fori_loop(0, i, body, 0)
    for g, sl in enumerate(hs):
        o_ref[0, :, sl] = (acc_sc[g] / l_sc[g]).T.astype(o_ref.dtype)


def _attention(q, kv, kr, positions, tq=512):
    b, s, _ = q.shape
    G = ATTN_HEADS_PER_STEP
    w = G * LANES
    ng = MLA_HEADS // G
    return pl.pallas_call(
        functools.partial(_attn_kernel, tq=tq), grid=(b, ng, s // tq),
        in_specs=[pl.BlockSpec((1, tq, w), lambda bi, hi, i: (bi, i, hi)),
                  pl.BlockSpec((1, tq, w), lambda bi, hi, i: (bi, i, ng + hi)),
                  pl.BlockSpec((1, s, w), lambda bi, hi, i: (bi, 0, hi)),
                  pl.BlockSpec((1, s, LANES), lambda bi, hi, i: (bi, 0, 0)),
                  pl.BlockSpec((1, s, w), lambda bi, hi, i: (bi, 0, ng + hi)),
                  pl.BlockSpec((1, 1, tq), lambda bi, hi, i: (bi, 0, i)),
                  pl.BlockSpec((1, tq, 1), lambda bi, hi, i: (bi, i, 0))],
        out_specs=pl.BlockSpec((1, tq, w), lambda bi, hi, i: (bi, i, hi)),
        out_shape=jax.ShapeDtypeStruct((b, s, MLA_HEADS * MLA_V), BF16),
        scratch_shapes=[pltpu.VMEM((G, 1, tq), F32), pltpu.VMEM((G, 1, tq), F32),
                        pltpu.VMEM((G, MLA_V, tq), F32),
                        pltpu.VMEM((G, 2 * LANES, tq), BF16), pltpu.VMEM((G, MLA_V, s), BF16)],
        compiler_params=_params(("parallel", "parallel", "arbitrary")),
    )(q, q, kv, kr, kv, positions.reshape(b, 1, s), positions.reshape(b, s, 1))


def _split3_dot(a_bf16, x):
    hi = x.astype(BF16)
    r1 = x - hi.astype(F32)
    mid = r1.astype(BF16)
    lo = (r1 - mid.astype(F32)).astype(BF16)
    return (jnp.dot(a_bf16, hi, preferred_element_type=F32)
            + jnp.dot(a_bf16, mid, preferred_element_type=F32)
            + jnp.dot(a_bf16, lo, preferred_element_type=F32))


def _mlstm_kernel(qk_ref, v_ref, g_ref, og_ref, gb_ref, hn_ref, o_ref, ct_ref, m_ref):
    cidx = pl.program_id(1)

    @pl.when(cidx == 0)
    def _():
        ct_ref[...] = jnp.zeros_like(ct_ref)
        m_ref[...] = jnp.zeros_like(m_ref)

    L = CHUNK
    g = g_ref[0]
    gb = gb_ref[...]
    logi = g[:, :LANES] + gb[:, :LANES]
    xf = g[:, LANES:] + gb[:, LANES:]
    logf = jnp.minimum(xf, 0.0) - jnp.log1p(jnp.exp(-jnp.abs(xf)))
    row = lax.broadcasted_iota(jnp.int32, (L, L), 0)
    col = lax.broadcasted_iota(jnp.int32, (L, L), 1)
    tril = row >= col
    bcum = _split3_dot(jnp.where(tril, 1.0, 0.0).astype(BF16), logf)
    x = logi - bcum
    xt = x.T
    m_row = m_ref[...]
    b_last = bcum[L - 1:L, :]
    m_new = jnp.maximum(b_last + m_row, b_last + jnp.max(x, axis=0, keepdims=True))
    decay = jnp.exp(b_last + m_row - m_new)
    ws_all = jnp.exp(b_last + x - m_new)
    inter_all = bcum + m_row
    ones_col = jnp.ones((L, LANES), BF16)
    nt = (((1,), (1,)), ((), ()))
    tn = (((0,), (0,)), ((), ()))

    def early(h):
        q = qk_ref[0, :, h * ML_DK:(h + 1) * ML_DK]
        k = qk_ref[0, :, (ML_HEADS + h) * ML_DK:(ML_HEADS + h + 1) * ML_DK]
        ct = ct_ref[h]
        qk = lax.dot_general(q, k, nt, preferred_element_type=F32)
        qc = jnp.dot(q, ct.astype(BF16), preferred_element_type=F32)
        dlog = jnp.where(tril, bcum[:, h:h + 1] + xt[h:h + 1, :], -jnp.inf)
        inter = inter_all[:, h:h + 1]
        mt = jnp.maximum(inter, jnp.max(dlog, axis=-1, keepdims=True))
        return k, ct, qk, qc, mt, jnp.exp(inter - mt), jnp.exp(dlog - mt)

    ahead = early(0)
    hh_all = []
    for h in range(ML_HEADS):
        k, ct, qk, qc, mt, w_inter, e = ahead
        if h + 1 < ML_HEADS:
            ahead = early(h + 1)
        vaug = jnp.concatenate([v_ref[0, :, h * ML_DV:(h + 1) * ML_DV], ones_col], axis=-1)
        num = jnp.dot((qk * e).astype(BF16), vaug, preferred_element_type=F32) + w_inter * qc
        den = jnp.maximum(jnp.abs(num[:, ML_DV:]), jnp.exp(-mt))
        hh_all.append(num[:, :ML_DV] / jnp.concatenate([den, den], axis=-1))
        wv = (ws_all[:, h:h + 1] * vaug.astype(F32)).astype(BF16)
        ct_ref[h] = decay[:, h:h + 1] * ct + lax.dot_general(k, wv, tn, preferred_element_type=F32)
    for h, hh in enumerate(hh_all):
        hs = slice(h * ML_DV, (h + 1) * ML_DV)
        hn = hh * lax.rsqrt(jnp.mean(hh * hh, axis=-1, keepdims=True) + EPS) * hn_ref[0, :, hs]
        o_ref[0, :, hs] = (og_ref[0, :, hs].astype(F32) * hn).astype(BF16)
    m_ref[...] = m_new


def _mlstm(qk, zp, gates, zs, gate_b, head_norm, lay):
    b, s, _ = qk.shape
    gb = jnp.zeros((1, 2 * LANES), F32)
    gb = gb.at[0, :ML_HEADS].set(gate_b[:ML_HEADS]).at[0, LANES:LANES + ML_HEADS].set(gate_b[ML_HEADS:])
    blk = lambda col: pl.BlockSpec((1, CHUNK, MIX_W), lambda bi, c, col=col: (bi, c, col))
    return pl.pallas_call(
        _mlstm_kernel, grid=(b, s // CHUNK),
        in_specs=[blk(0), blk(2),
                  pl.BlockSpec((1, CHUNK, 2 * LANES), lambda bi, c: (bi, c, 0)),
                  blk(0),
                  pl.BlockSpec((1, 2 * LANES), lambda bi, c: (0, 0)),
                  pl.BlockSpec((1, 1, MIX_W), lambda bi, c: (lay, 0, 0))],
        out_specs=blk(0),
        out_shape=jax.ShapeDtypeStruct((b, s, MIX_W), BF16),
        scratch_shapes=[pltpu.VMEM((ML_HEADS, ML_DK, ML_DV + LANES), F32), pltpu.VMEM((1, LANES), F32)],
        compiler_params=_params(("parallel", "arbitrary")),
    )(qk, zp, gates, zs, gb, head_norm.reshape(DEPTH, 1, MIX_W))


def _merge_epilogue(accs, e_refs, o_refs):
    acc = e_refs[0][...].astype(F32) * accs[0]
    acc += e_refs[1][...].astype(F32) * accs[1]
    acc += e_refs[2][...].astype(F32) * accs[2]
    o_refs[0][...] = acc.astype(BF16)


def _merge(ya, yb, yc, w_branch, lay, zs, tm=1024, tn=512):
    m = ya.shape[0]
    goff = MIX_W // tn
    gate = lambda k: (zs, (tm, tn), lambda i, j, k=k: (i, goff + k * (D_MODEL // tn) + j))
    return _ws_mm([ya, yb, yc], [w_branch], [(k, 0, (lay, k), 0) for k in range(N_BRANCH)],
                  [gate(k) for k in range(N_BRANCH)],
                  [((m, D_MODEL), BF16, (tm, tn), lambda i, j: (i, j))],
                  _merge_epilogue, tm=tm, tn=tn, nj=D_MODEL // tn)[0]


def _prep_mixer_weights(w_mix_in, mla_w_uq, mla_w_ukv):
    w_t = jnp.swapaxes(w_mix_in, 1, 2)
    uq = mla_w_uq.reshape(DEPTH, MLA_Q_RANK, MLA_HEADS, MLA_NOPE + MLA_ROPE)
    w_qn = uq[..., :MLA_NOPE].reshape(DEPTH, MLA_Q_RANK, MLA_HEADS * MLA_NOPE)
    w_qr = jnp.pad(uq[..., MLA_NOPE:], ((0, 0), (0, 0), (0, 0), (0, LANES - MLA_ROPE)))
    w_q = jnp.concatenate([w_qn, w_qr.reshape(DEPTH, MLA_Q_RANK, MLA_HEADS * LANES)], axis=-1).astype(BF16)
    ukv = mla_w_ukv.reshape(DEPTH, MLA_KV_RANK, MLA_HEADS, 2, MLA_NOPE)
    w_kv = ukv.transpose(0, 1, 3, 2, 4).reshape(DEPTH, MLA_KV_RANK, 2 * MLA_HEADS * MLA_NOPE).astype(BF16)
    return w_t, w_q, w_kv


def _mixer(h, lay, positions, rope_tabs, w_t, w_q, w_kv, pool_w, pool_scale, mla_q_norm, mla_kv_norm,
           ml_conv_w, ml_conv_b, ml_gate_b, ml_head_norm, w_branch, tm=1024):
    b, s, d = h.shape
    m = b * s
    h2 = h.reshape(m, d)
    cos, sina, sinb = rope_tabs
    tab = lambda a: (a, (tm, LANES), lambda i, j: (i, 0))
    tile = lambda i, j: (i, j)
    layvec = lambda a: (a.reshape(DEPTH, 1, -1), (1, 1, a.shape[-1]), lambda i, j: (lay, 0, 0))

    o = _OFF
    wide = 1024
    zp = _wst_mm(h2, w_t, lay, [], [((m, N_PLAIN), BF16, (tm, wide), tile)], _plain_epilogue,
                 tm=tm, tn=wide, nj=N_PLAIN // wide, segs=((0, o[0]), (MIX_W // wide, o[4])))[0]
    zs = _wst_mm(h2, w_t, lay, [], [((m, N_SIG), BF16, (tm, wide), tile)], _sigmoid_epilogue,
                 tm=tm, tn=wide, nj=N_SIG // wide, segs=((0, o[7]), (MIX_W // wide, o[10])))[0]
    cqn = _wst_mm(h2, w_t, lay, [layvec(mla_q_norm)], [((m, MLA_Q_RANK), BF16, (tm, MLA_Q_RANK), tile)],
                  _rmsw_epilogue, tm=tm, tn=MLA_Q_RANK, nj=1, segs=((0, o[1]),))[0]
    ckvn, kr, gates = _wst_mm(
        h2, w_t, lay, [layvec(mla_kv_norm), tab(cos), tab(sina), tab(sinb)],
        [((m, MLA_KV_RANK), BF16, (tm, MLA_KV_RANK), tile),
         ((m, LANES), BF16, (tm, LANES), tile),
         ((m, 2 * LANES), F32, (tm, 2 * LANES), tile)],
        _kvlatent_epilogue, tm=tm, tn=N_LAT, nj=1,
        parts=((o[2], MLA_KV_RANK, 0), (o[3], MLA_ROPE, MLA_KV_RANK),
               (o[8], ML_HEADS, MLA_KV_RANK + LANES), (o[9], ML_HEADS, MLA_KV_RANK + 2 * LANES)))

    zp3 = zp.reshape(b, s, N_PLAIN)
    ya = _pool(zp3, pool_w, pool_scale, lay, b, s)

    nq = MLA_HEADS * LANES
    tmq = tm // 2
    tabq = lambda a: (a, (tmq, LANES), lambda i, j: (i, 0))
    q = _mm(cqn, [(w_q, lay, 0)], [tabq(cos), tabq(sina), tabq(sinb)],
            [((m, 2 * nq), BF16, (tmq, 2 * nq), tile)], _q_epilogue, tm=tmq, tn=2 * nq, nj=1)[0]
    kv = _mm(ckvn, [(w_kv, lay, 0)], [], [((m, 2 * nq), BF16, (tm, 2 * nq), tile)],
             _plain_epilogue, tm=tm, tn=2 * nq, nj=1)[0]
    yb = _attention(q.reshape(b, s, 2 * nq), kv.reshape(b, s, 2 * nq), kr.reshape(b, s, LANES), positions)

    qk = _conv_silu(zp3, ml_conv_w, ml_conv_b, lay, b, s)
    yc = _mlstm(qk, zp3, gates.reshape(b, s, 2 * LANES), zs.reshape(b, s, N_SIG), ml_gate_b[lay],
                ml_head_norm, lay)

    return _merge(ya.reshape(m, MIX_W), yb.reshape(m, MIX_W), yc.reshape(m, MIX_W), w_branch, lay, zs)


def _q_epilogue(accs, e_refs, o_refs):
    nq = MLA_HEADS * LANES
    a = accs[0]
    o_refs[0][:, :nq] = (a[:, :nq] * Q_SCALE_LOG2).astype(BF16)
    cos_ref, sina_ref, sinb_ref = e_refs
    cos, sina, sinb = cos_ref[...], sina_ref[...], sinb_ref[...]
    for c in range(MLA_HEADS):
        sl = slice(nq + c * LANES, nq + (c + 1) * LANES)
        o_refs[0][:, sl] = _rope128(a[:, sl] * Q_SCALE_LOG2, cos, sina, sinb).astype(BF16)


def kernel(x, c, positions, w_ada, b_ada, ada_table, ffn_a_w_in, ffn_a_w_out, w_mix_in, pool_w, pool_scale, mla_q_norm, mla_w_uq, mla_kv_norm, mla_w_ukv, ml_conv_w, ml_conv_b, ml_gate_b, ml_head_norm, w_branch, w_out, ffn_b_w_in, ffn_b_w_out, final_norm):
    b, s, d = x.shape
    m = b * s
    mod = _ada(c, w_ada, b_ada, ada_table)
    rope_tabs = _rope_tables(positions)
    w_t, w_q, w_kv = _prep_mixer_weights(w_mix_in, mla_w_uq, mla_w_ukv)
    pool_wb = pool_w.astype(BF16)
    for l in range(DEPTH):
        md = mod[l]
        h = _normmod(x, md[:, 0], md[:, 1])
        x = _ffn(x, h, ffn_a_w_in, ffn_a_w_out, l, md[:, 2])
        h = _normmod(x, md[:, 3], md[:, 4])
        merged = _mixer(h, l, positions, rope_tabs, w_t, w_q, w_kv, pool_wb, pool_scale, mla_q_norm,
                        mla_kv_norm, ml_conv_w, ml_conv_b, ml_gate_b, ml_head_norm, w_branch)
        x = _resid_mm(merged, w_out, l, x.reshape(m, d), md[:, 5], 1.0, s, tm=512, tn=1024).reshape(b, s, d)
        h = _normmod(x, md[:, 6], md[:, 7])
        x = _ffn(x, h, ffn_b_w_in, ffn_b_w_out, l, md[:, 8])
    return _finalnorm(x, final_norm)
```

```python
import functools

import numpy as np
import jax
import jax.numpy as jnp
from jax import lax
from jax.experimental import pallas as pl
from jax.experimental.pallas import tpu as pltpu

F32 = jnp.float32
BF16 = jnp.bfloat16

D_MODEL = 4096
DEPTH = 2
CHUNK = 64
EPS = 1e-6
D_FF = 2 * D_MODEL
MIX_W = D_MODEL // 2
N_BRANCH = 3
N_MOD = 9
POOL_WINDOWS = (2, 4, 8, 16)
POOL_GW = MIX_W // len(POOL_WINDOWS)
MLA_NOPE = 128
MLA_ROPE = 64
MLA_V = 128
MLA_HEADS = MIX_W // MLA_V
MLA_Q_RANK = D_MODEL // 4
MLA_KV_RANK = 512
MLA_SCALE = (MLA_NOPE + MLA_ROPE) ** -0.5
Q_SCALE_LOG2 = MLA_SCALE * float(np.log2(np.e))
ROPE_THETA = 10000.0
ML_HEADS = 8
ML_DK = 128
ML_DV = MIX_W // ML_HEADS
CONV_W = 4

_SPLITS = (MIX_W, MLA_Q_RANK, MLA_KV_RANK, MLA_ROPE, ML_HEADS * ML_DK, ML_HEADS * ML_DK,
           ML_HEADS * ML_DV, ML_HEADS * ML_DV, ML_HEADS, ML_HEADS, N_BRANCH * D_MODEL)
_OFF = tuple(int(v) for v in np.cumsum((0,) + _SPLITS))

LANES = 128
SUBLANES = 8
HALO = 16
VMEM_LIMIT = 60 * 1024 * 1024

N_PLAIN = 3 * MIX_W
N_SIG = MIX_W + N_BRANCH * D_MODEL
N_LAT = MLA_KV_RANK + 3 * LANES


def _params(sem):
    return pltpu.CompilerParams(dimension_semantics=sem, vmem_limit_bytes=VMEM_LIMIT)


def _mm_body(*refs, nw, ne, no, epilogue):
    x_ref = refs[0]
    w_refs = refs[1:1 + nw]
    e_refs = refs[1 + nw:1 + nw + ne]
    o_refs = refs[1 + nw + ne:1 + nw + ne + no]
    accs = [jnp.dot(x_ref[...], w[0], preferred_element_type=F32) for w in w_refs]
    epilogue(accs, e_refs, o_refs)


def _mm(x, ws, extras, outs, epilogue, *, tm, tn, nj):
    m, kdim = x.shape
    in_specs = [pl.BlockSpec((tm, kdim), lambda i, j: (i, 0))]
    for _, lay, col in ws:
        assert col % tn == 0
        in_specs.append(pl.BlockSpec((1, kdim, tn), lambda i, j, lay=lay, off=col // tn: (lay, 0, off + j)))
    for _, blk, f in extras:
        in_specs.append(pl.BlockSpec(blk, f))
    out_specs = [pl.BlockSpec(blk, f) for _, _, blk, f in outs]
    out_shape = [jax.ShapeDtypeStruct(s, d) for s, d, _, _ in outs]
    body = functools.partial(_mm_body, nw=len(ws), ne=len(extras), no=len(outs), epilogue=epilogue)
    return pl.pallas_call(
        body, grid=(m // tm, nj), in_specs=in_specs, out_specs=out_specs, out_shape=out_shape,
        compiler_params=_params(("parallel", "parallel")),
    )(x, *[w for w, _, _ in ws], *[a for a, _, _ in extras])


def _sigmoid(x):
    return 0.5 * jnp.tanh(0.5 * x) + 0.5


def _rope128(v, cos, sina, sinb):
    return (v * cos + pltpu.roll(v, LANES - MLA_ROPE // 2, 1) * sina
            + pltpu.roll(v, MLA_ROPE // 2, 1) * sinb)


def _ada_kernel(c_ref, w_ref, b_ref, t_ref, o_ref):
    c = c_ref[...]
    s = c * jax.nn.sigmoid(c)
    acc = jnp.dot(s.astype(BF16), w_ref[...].astype(BF16), preferred_element_type=F32) + b_ref[...]
    for l in range(DEPTH):
        o_ref[l] = acc + t_ref[l]


def _ada(c, w_ada, b_ada, ada_table):
    b = c.shape[0]
    rows = SUBLANES
    assert b <= rows
    c8 = jnp.zeros((rows, D_MODEL), F32).at[:b].set(c)
    n = N_MOD * D_MODEL
    tn = 1024
    out = pl.pallas_call(
        _ada_kernel, grid=(n // tn,),
        in_specs=[pl.BlockSpec((rows, D_MODEL), lambda j: (0, 0)),
                  pl.BlockSpec((D_MODEL, tn), lambda j: (0, j)),
                  pl.BlockSpec((1, tn), lambda j: (0, j)),
                  pl.BlockSpec((DEPTH, 1, tn), lambda j: (0, 0, j))],
        out_specs=pl.BlockSpec((DEPTH, rows, tn), lambda j: (0, 0, j)),
        out_shape=jax.ShapeDtypeStruct((DEPTH, rows, n), F32),
        compiler_params=_params(("parallel",)),
    )(c8, w_ada, b_ada.reshape(1, n), ada_table.reshape(DEPTH, 1, n))
    return out[:, :b].reshape(DEPTH, b, N_MOD, D_MODEL)


def _normmod_kernel(x_ref, shift_ref, scale_ref, o_ref):
    x = x_ref[0]
    y = x * lax.rsqrt(jnp.mean(x * x, axis=-1, keepdims=True) + EPS)
    o_ref[0] = (y * (1.0 + scale_ref[0]) + shift_ref[0]).astype(o_ref.dtype)


def _finalnorm_kernel(x_ref, w_ref, o_ref):
    x = x_ref[0]
    y = x * lax.rsqrt(jnp.mean(x * x, axis=-1, keepdims=True) + EPS)
    o_ref[0] = y * w_ref[...]


def _normmod(x, shift, scale, ts=512):
    b, s, d = x.shape
    vec = pl.BlockSpec((1, 1, d), lambda bi, i: (bi, 0, 0))
    return pl.pallas_call(
        _normmod_kernel, grid=(b, s // ts),
        in_specs=[pl.BlockSpec((1, ts, d), lambda bi, i: (bi, i, 0)), vec, vec],
        out_specs=pl.BlockSpec((1, ts, d), lambda bi, i: (bi, i, 0)),
        out_shape=jax.ShapeDtypeStruct((b, s, d), BF16),
        compiler_params=_params(("parallel", "parallel")),
    )(x, shift.reshape(b, 1, d), scale.reshape(b, 1, d))


def _finalnorm(x, w, ts=512):
    b, s, d = x.shape
    return pl.pallas_call(
        _finalnorm_kernel, grid=(b, s // ts),
        in_specs=[pl.BlockSpec((1, ts, d), lambda bi, i: (bi, i, 0)),
                  pl.BlockSpec((1, d), lambda bi, i: (0, 0))],
        out_specs=pl.BlockSpec((1, ts, d), lambda bi, i: (bi, i, 0)),
        out_shape=jax.ShapeDtypeStruct((b, s, d), F32),
        compiler_params=_params(("parallel", "parallel")),
    )(x, w.reshape(1, d))


def _resid_epilogue(accs, e_refs, o_refs, *, coef):
    x_ref, g_ref = e_refs
    o_refs[0][...] = x_ref[...] + (coef * g_ref[0]) * accs[0]


def _ws_body(*refs, nx, nwt, ne, no, pairs, tn, nj, x_ring, epilogue):
    x_refs = refs[:nx]
    w_hbm = refs[nx:nx + nwt]
    e_refs = refs[nx + nwt:nx + nwt + ne]
    o_refs = refs[nx + nwt + ne:nx + nwt + ne + no]
    stage, wbf, sems = refs[nx + nwt + ne + no:nx + nwt + ne + no + 3]
    j = pl.program_id(0)
    i = pl.program_id(1)

    if x_ring:
        xbuf, xsems = refs[nx + nwt + ne + no + 3:]
        ni = pl.num_programs(1)
        step = j * ni + i
        tm = xbuf.shape[1]

        def x_copy(t):
            row = pl.multiple_of((t % ni) * tm, tm)
            return pltpu.make_async_copy(x_refs[0].at[pl.ds(row, tm), :], xbuf.at[t % x_ring],
                                         xsems.at[t % x_ring])

        @pl.when(step == 0)
        def _():
            for t in range(x_ring - 1):
                x_copy(t).start()

        x_copy(step).wait()

        @pl.when(step + x_ring - 1 < nj * ni)
        def _():
            x_copy(step + x_ring - 1).start()

        x_tiles = [xbuf.at[step % x_ring]]
    else:
        x_tiles = x_refs

    def tile_copy(jj, p):
        _, wi, lead, col = pairs[p]
        src = w_hbm[wi].at[(*lead, slice(None), pl.ds(pl.multiple_of(col + jj * tn, tn), tn))]
        return pltpu.make_async_copy(src, stage.at[p], sems.at[p])

    @pl.when(i == 0)
    def _():
        @pl.when(j == 0)
        def _():
            for p in range(len(pairs)):
                tile_copy(j, p).start()

        for p in range(len(pairs)):
            tile_copy(j, p).wait()
            wbf[p] = stage[p].astype(BF16)

        @pl.when(j + 1 < nj)
        def _():
            for p in range(len(pairs)):
                tile_copy(j + 1, p).start()

    accs = [jnp.dot(x_tiles[xi][...], wbf[p], preferred_element_type=F32)
            for p, (xi, _, _, _) in enumerate(pairs)]
    epilogue(accs, e_refs, o_refs)


def _ws_mm(xs, wts, pairs, extras, outs, epilogue, *, tm, tn, nj, x_ring=0):
    m, kdim = xs[0].shape
    assert all(c % tn == 0 for _, _, _, c in pairs) and (not x_ring or len(xs) == 1)
    if x_ring:
        in_specs = [pl.BlockSpec(memory_space=pl.ANY)]
    else:
        in_specs = [pl.BlockSpec((tm, kdim), lambda j, i: (i, 0)) for _ in xs]
    in_specs += [pl.BlockSpec(memory_space=pl.ANY) for _ in wts]
    in_specs += [pl.BlockSpec(blk, lambda j, i, f=f: f(i, j)) for _, blk, f in extras]
    out_specs = [pl.BlockSpec(blk, lambda j, i, f=f: f(i, j)) for _, _, blk, f in outs]
    out_shape = [jax.ShapeDtypeStruct(sh, dt) for sh, dt, _, _ in outs]
    body = functools.partial(_ws_body, nx=len(xs), nwt=len(wts), ne=len(extras), no=len(outs),
                             pairs=pairs, tn=tn, nj=nj, x_ring=x_ring, epilogue=epilogue)
    ring = [pltpu.VMEM((x_ring, tm, kdim), BF16), pltpu.SemaphoreType.DMA((x_ring,))] if x_ring else []
    return pl.pallas_call(
        body, grid=(nj, m // tm), in_specs=in_specs, out_specs=out_specs, out_shape=out_shape,
        scratch_shapes=[pltpu.VMEM((len(pairs), kdim, tn), F32), pltpu.VMEM((len(pairs), kdim, tn), BF16),
                        pltpu.SemaphoreType.DMA((len(pairs),))] + ring,
        compiler_params=_params(("arbitrary", "arbitrary")),
    )(*xs, *wts, *[a for a, _, _ in extras])


def _wst_body(*refs, ne, no, lay, tn, nj, segs, parts, epilogue):
    x_ref, wt_hbm = refs[0], refs[1]
    e_refs = refs[2:2 + ne]
    o_refs = refs[2 + ne:2 + ne + no]
    stage, wbf, sems = refs[2 + ne + no:]
    j = pl.program_id(0)
    i = pl.program_id(1)

    def copies(jj):
        if parts is not None:
            return [pltpu.make_async_copy(wt_hbm.at[lay, pl.ds(src, n), :], stage.at[pl.ds(dst, n), :],
                                          sems.at[c]) for c, (src, n, dst) in enumerate(parts)]
        row = segs[0][1] + jj * tn
        for first_tile, row0 in segs[1:]:
            row = jnp.where(jj >= first_tile, row0 + (jj - first_tile) * tn, row)
        src = wt_hbm.at[lay, pl.ds(pl.multiple_of(row, SUBLANES), tn), :]
        return [pltpu.make_async_copy(src, stage, sems.at[0])]

    @pl.when(i == 0)
    def _():
        @pl.when(j == 0)
        def _():
            if parts is not None:
                covered = sorted((dst, dst + n) for _, n, dst in parts)
                for lo, hi in zip([0] + [b for _, b in covered], [a for a, _ in covered] + [tn]):
                    if hi > lo:
                        stage[lo:hi, :] = jnp.zeros((hi - lo, stage.shape[1]), F32)
            for cp in copies(j):
                cp.start()

        for cp in copies(j):
            cp.wait()
        wbf[...] = stage[...].astype(BF16)

        @pl.when(j + 1 < nj)
        def _():
            for cp in copies(j + 1):
                cp.start()

    acc = lax.dot_general(x_ref[...], wbf[...], (((1,), (1,)), ((), ())), preferred_element_type=F32)
    epilogue([acc], e_refs, o_refs)


def _wst_mm(x, wt, lay, extras, outs, epilogue, *, tm, tn, nj, segs=None, parts=None):
    m, kdim = x.shape
    assert (segs is None) != (parts is None) and (parts is None or nj == 1)
    in_specs = [pl.BlockSpec((tm, kdim), lambda j, i: (i, 0)), pl.BlockSpec(memory_space=pl.ANY)]
    in_specs += [pl.BlockSpec(blk, lambda j, i, f=f: f(i, j)) for _, blk, f in extras]
    out_specs = [pl.BlockSpec(blk, lambda j, i, f=f: f(i, j)) for _, _, blk, f in outs]
    out_shape = [jax.ShapeDtypeStruct(sh, dt) for sh, dt, _, _ in outs]
    body = functools.partial(_wst_body, ne=len(extras), no=len(outs), lay=lay, tn=tn, nj=nj,
                             segs=segs, parts=parts, epilogue=epilogue)
    return pl.pallas_call(
        body, grid=(nj, m // tm), in_specs=in_specs, out_specs=out_specs, out_shape=out_shape,
        scratch_shapes=[pltpu.VMEM((tn, kdim), F32), pltpu.VMEM((tn, kdim), BF16),
                        pltpu.SemaphoreType.DMA((len(parts) if parts else 1,))],
        compiler_params=_params(("arbitrary", "arbitrary")),
    )(x, wt, *[a for a, _, _ in extras])


def _swiglu_epilogue(accs, e_refs, o_refs):
    g, u = accs
    o_refs[0][...] = (g * _sigmoid(g) * u).astype(BF16)


def _resid_mm(a, w, lay, x2d, gate, coef, seq, *, tm, tn, x_ring=0):
    m, n = x2d.shape
    per_b = seq // tm
    return _ws_mm([a], [w], [(0, 0, (lay,), 0)],
                  [(x2d, (tm, tn), lambda i, j: (i, j)),
                   (gate.reshape(-1, 1, n), (1, 1, tn), lambda i, j: (i // per_b, 0, j))],
                  [((m, n), F32, (tm, tn), lambda i, j: (i, j))],
                  functools.partial(_resid_epilogue, coef=coef), tm=tm, tn=tn, nj=n // tn, x_ring=x_ring)[0]


def _swiglu_mm(h2, w_in, lay, *, tm=1024, tn=512):
    m, _ = h2.shape
    return _ws_mm([h2], [w_in], [(0, 0, (lay,), 0), (0, 0, (lay,), D_FF)], [],
                  [((m, D_FF), BF16, (tm, tn), lambda i, j: (i, j))],
                  _swiglu_epilogue, tm=tm, tn=tn, nj=D_FF // tn)[0]


def _ffn(x, h, w_in, w_out, lay, gate):
    b, s, d = x.shape
    m = b * s
    a = _swiglu_mm(h.reshape(m, d), w_in, lay)
    y = _resid_mm(a, w_out, lay, x.reshape(m, d), gate, 0.5, s, tm=512, tn=512, x_ring=3)
    return y.reshape(b, s, d)


def _rope_tab_kernel(p_ref, inv_ref, cos_ref, sina_ref, sinb_ref):
    ang = p_ref[...].astype(F32) * inv_ref[...]
    lane = lax.broadcasted_iota(jnp.int32, ang.shape, 1)
    half = MLA_ROPE // 2
    c = jnp.cos(ang)
    s = jnp.sin(ang)
    cos_ref[...] = jnp.where(lane < MLA_ROPE, c, 0.0)
    sina_ref[...] = jnp.where(lane < half, -s, 0.0)
    sinb_ref[...] = jnp.where(lane >= half, jnp.where(lane < MLA_ROPE, s, 0.0), 0.0)


def _rope_tables(positions, ts=512):
    m = positions.size
    half = MLA_ROPE // 2
    inv = ROPE_THETA ** (-jnp.arange(0, MLA_ROPE, 2, dtype=F32) / MLA_ROPE)
    inv128 = jnp.concatenate([inv, inv, jnp.zeros((LANES - 2 * half,), F32)]).reshape(1, LANES)
    spec = pl.BlockSpec((ts, LANES), lambda i: (i, 0))
    shp = jax.ShapeDtypeStruct((m, LANES), F32)
    return pl.pallas_call(
        _rope_tab_kernel, grid=(m // ts,),
        in_specs=[pl.BlockSpec((ts, 1), lambda i: (i, 0)), pl.BlockSpec((1, LANES), lambda i: (0, 0))],
        out_specs=[spec, spec, spec], out_shape=[shp, shp, shp],
        compiler_params=_params(("parallel",)),
    )(positions.reshape(m, 1), inv128)


def _plain_epilogue(accs, e_refs, o_refs):
    o_refs[0][...] = accs[0].astype(o_refs[0].dtype)


def _sigmoid_epilogue(accs, e_refs, o_refs):
    o_refs[0][...] = _sigmoid(accs[0]).astype(o_refs[0].dtype)


def _rmsw_epilogue(accs, e_refs, o_refs):
    a = accs[0]
    y = a * lax.rsqrt(jnp.mean(a * a, axis=-1, keepdims=True) + EPS) * e_refs[0][0]
    o_refs[0][...] = y.astype(o_refs[0].dtype)


def _kvlatent_epilogue(accs, e_refs, o_refs):
    a = accs[0]
    w_ref, cos_ref, sina_ref, sinb_ref = e_refs
    ckv = a[:, :MLA_KV_RANK]
    y = ckv * lax.rsqrt(jnp.mean(ckv * ckv, axis=-1, keepdims=True) + EPS) * w_ref[0]
    o_refs[0][...] = y.astype(BF16)
    kr = a[:, MLA_KV_RANK:MLA_KV_RANK + LANES]
    o_refs[1][...] = _rope128(kr, cos_ref[...], sina_ref[...], sinb_ref[...]).astype(BF16)
    o_refs[2][...] = a[:, MLA_KV_RANK + LANES:]


def _band(ts, lo, hi, first_tile):
    t = lax.broadcasted_iota(jnp.int32, (ts, HALO + ts), 0)
    s = lax.broadcasted_iota(jnp.int32, (ts, HALO + ts), 1)
    d = t + HALO - s
    ok = jnp.where(d >= lo, jnp.where(d < hi, 1.0, 0.0), 0.0)
    ok = jnp.where(s < HALO, jnp.where(first_tile, 0.0, ok), ok)
    return ok.astype(BF16)


def _pool_kernel(u_ref, halo_ref, pw_ref, ps_ref, o_ref, *, ts):
    i = pl.program_id(1)
    u = u_ref[0]
    ucat = jnp.concatenate([halo_ref[0], u], axis=0)
    tg = i * ts + lax.broadcasted_iota(jnp.int32, (ts, 1), 0)
    for g, w in enumerate(POOL_WINDOWS):
        sl = slice(g * POOL_GW, (g + 1) * POOL_GW)
        win = jnp.dot(_band(ts, 0, w, i == 0), ucat[:, sl], preferred_element_type=F32)
        cnt = jnp.minimum(tg + 1, w).astype(F32)
        p = win / cnt - u[:, sl].astype(F32)
        y = jnp.dot(p.astype(BF16), pw_ref[0, g], preferred_element_type=F32)
        o_ref[0, :, sl] = (y * ps_ref[0, :, sl]).astype(BF16)


def _pool(zp, pool_w, pool_scale, lay, b, s, ts=256):
    hb = ts // HALO
    ng = len(POOL_WINDOWS)
    return pl.pallas_call(
        functools.partial(_pool_kernel, ts=ts), grid=(b, s // ts),
        in_specs=[pl.BlockSpec((1, ts, MIX_W), lambda bi, i: (bi, i, 0)),
                  pl.BlockSpec((1, HALO, MIX_W), lambda bi, i: (bi, jnp.maximum(i * hb - 1, 0), 0)),
                  pl.BlockSpec((1, ng, POOL_GW, POOL_GW), lambda bi, i: (lay, 0, 0, 0)),
                  pl.BlockSpec((1, 1, MIX_W), lambda bi, i: (lay, 0, 0))],
        out_specs=pl.BlockSpec((1, ts, MIX_W), lambda bi, i: (bi, i, 0)),
        out_shape=jax.ShapeDtypeStruct((b, s, MIX_W), BF16),
        compiler_params=_params(("parallel", "parallel")),
    )(zp, zp, pool_w, pool_scale.reshape(DEPTH, 1, MIX_W))


def _conv_kernel(x_ref, halo_ref, w_ref, b_ref, sc_ref, o_ref, *, ts):
    i = pl.program_id(1)
    x = x_ref[0]
    xcat = jnp.concatenate([halo_ref[0], x], axis=0)
    w = w_ref[0]
    acc = x.astype(F32) * w[CONV_W - 1:CONV_W, :] + b_ref[0]
    for d in range(1, CONV_W):
        xs = jnp.dot(_band(ts, d, d + 1, i == 0), xcat, preferred_element_type=F32)
        acc = acc + xs * w[CONV_W - 1 - d:CONV_W - d, :]
    o_ref[0] = (acc * jax.nn.sigmoid(acc) * sc_ref[...]).astype(BF16)


def _conv_silu(zp, conv_w, conv_b, lay, b, s, ts=128):
    c = 2 * ML_HEADS * ML_DK
    hb = ts // HALO
    post = jnp.concatenate([jnp.full((c // 2,), ML_DK ** -0.5, F32), jnp.ones((c // 2,), F32)])
    return pl.pallas_call(
        functools.partial(_conv_kernel, ts=ts), grid=(b, s // ts),
        in_specs=[pl.BlockSpec((1, ts, c), lambda bi, i: (bi, i, 1)),
                  pl.BlockSpec((1, HALO, c), lambda bi, i: (bi, jnp.maximum(i * hb - 1, 0), 1)),
                  pl.BlockSpec((1, CONV_W, c), lambda bi, i: (lay, 0, 0)),
                  pl.BlockSpec((1, 1, c), lambda bi, i: (lay, 0, 0)),
                  pl.BlockSpec((1, c), lambda bi, i: (0, 0))],
        out_specs=pl.BlockSpec((1, ts, c), lambda bi, i: (bi, i, 0)),
        out_shape=jax.ShapeDtypeStruct((b, s, c), BF16),
        compiler_params=_params(("parallel", "parallel")),
    )(zp, zp, conv_w, conv_b.reshape(DEPTH, 1, c), post.reshape(1, c))


ATTN_HEADS_PER_STEP = 8


def _attn_kernel(qn_ref, qr_ref, kn_ref, kr_ref, v_ref, pq_ref, pk_ref, o_ref, m_sc, l_sc, acc_sc, qt_sc, vt_sc,
                 *, tq):
    i = pl.program_id(2)
    hs = [slice(g * LANES, (g + 1) * LANES) for g in range(ATTN_HEADS_PER_STEP)]

    for g, sl in enumerate(hs):
        qt_sc[g, :LANES, :] = qn_ref[0, :, sl].astype(F32).T.astype(BF16)
        qt_sc[g, LANES:, :] = qr_ref[0, :, sl].astype(F32).T.astype(BF16)

    @pl.when(i == 0)
    def _():
        for g, sl in enumerate(hs):
            vt_sc[g] = v_ref[0, :, sl].astype(F32).T.astype(BF16)

    def block(start, mask, first):
        kr = kr_ref[0, pl.ds(start, tq), :]

        def scores(g):
            k = jnp.concatenate([kn_ref[0, pl.ds(start, tq), hs[g]], kr], axis=-1)
            st = jnp.dot(k, qt_sc[g], preferred_element_type=F32)
            if mask is not None:
                st = jnp.where(mask, st, -jnp.inf)
            return st

        st_next = scores(0)
        for g, sl in enumerate(hs):
            st = st_next
            if g + 1 < len(hs):
                st_next = scores(g + 1)
            vt = vt_sc[g, :, pl.ds(start, tq)]
            smax = jnp.max(st, axis=0, keepdims=True)
            if first:
                m_new = smax
                p = jnp.exp2(st - m_new)
                l_sc[g] = jnp.sum(p, axis=0, keepdims=True)
                acc_sc[g] = jnp.dot(vt, p.astype(BF16), preferred_element_type=F32)
            else:
                m_old = m_sc[g]
                m_new = jnp.maximum(m_old, smax)
                alpha = jnp.exp2(m_old - m_new)
                p = jnp.exp2(st - m_new)
                l_sc[g] = alpha * l_sc[g] + jnp.sum(p, axis=0, keepdims=True)
                acc_sc[g] = alpha * acc_sc[g] + jnp.dot(vt, p.astype(BF16), preferred_element_type=F32)
            m_sc[g] = m_new

    sh = CHUNK.bit_length() - 1
    mask = lax.shift_right_arithmetic(pk_ref[0], sh) <= lax.shift_right_arithmetic(pq_ref[0], sh)
    block(pl.multiple_of(i * tq, tq), mask, True)

    def body(j, carry):
        block(pl.multiple_of(j * tq, tq), None, False)
        return carry

    lax.fori_loop(0, i, body, 0)
    for g, sl in enumerate(hs):
        o_ref[0, :, sl] = (acc_sc[g] / l_sc[g]).T.astype(o_ref.dtype)


def _attention(q, kv, kr, positions, tq=512):
    b, s, _ = q.shape
    G = ATTN_HEADS_PER_STEP
    w = G * LANES
    ng = MLA_HEADS // G
    return pl.pallas_call(
        functools.partial(_attn_kernel, tq=tq), grid=(b, ng, s // tq),
        in_specs=[pl.BlockSpec((1, tq, w), lambda bi, hi, i: (bi, i, hi)),
                  pl.BlockSpec((1, tq, w), lambda bi, hi, i: (bi, i, ng + hi)),
                  pl.BlockSpec((1, s, w), lambda bi, hi, i: (bi, 0, hi)),
                  pl.BlockSpec((1, s, LANES), lambda bi, hi, i: (bi, 0, 0)),
                  pl.BlockSpec((1, s, w), lambda bi, hi, i: (bi, 0, ng + hi)),
                  pl.BlockSpec((1, 1, tq), lambda bi, hi, i: (bi, 0, i)),
                  pl.BlockSpec((1, tq, 1), lambda bi, hi, i: (bi, i, 0))],
        out_specs=pl.BlockSpec((1, tq, w), lambda bi, hi, i: (bi, i, hi)),
        out_shape=jax.ShapeDtypeStruct((b, s, MLA_HEADS * MLA_V), BF16),
        scratch_shapes=[pltpu.VMEM((G, 1, tq), F32), pltpu.VMEM((G, 1, tq), F32),
                        pltpu.VMEM((G, MLA_V, tq), F32),
                        pltpu.VMEM((G, 2 * LANES, tq), BF16), pltpu.VMEM((G, MLA_V, s), BF16)],
        compiler_params=_params(("parallel", "parallel", "arbitrary")),
    )(q, q, kv, kr, kv, positions.reshape(b, 1, s), positions.reshape(b, s, 1))


def _split3_dot(a_bf16, x):
    hi = x.astype(BF16)
    r1 = x - hi.astype(F32)
    mid = r1.astype(BF16)
    lo = (r1 - mid.astype(F32)).astype(BF16)
    return (jnp.dot(a_bf16, hi, preferred_element_type=F32)
            + jnp.dot(a_bf16, mid, preferred_element_type=F32)
            + jnp.dot(a_bf16, lo, preferred_element_type=F32))


def _mlstm_kernel(qk_ref, v_ref, g_ref, og_ref, gb_ref, hn_ref, o_ref, ct_ref, m_ref):
    cidx = pl.program_id(1)

    @pl.when(cidx == 0)
    def _():
        ct_ref[...] = jnp.zeros_like(ct_ref)
        m_ref[...] = jnp.zeros_like(m_ref)

    L = CHUNK
    g = g_ref[0]
    gb = gb_ref[...]
    logi = g[:, :LANES] + gb[:, :LANES]
    xf = g[:, LANES:] + gb[:, LANES:]
    logf = jnp.minimum(xf, 0.0) - jnp.log1p(jnp.exp(-jnp.abs(xf)))
    row = lax.broadcasted_iota(jnp.int32, (L, L), 0)
    col = lax.broadcasted_iota(jnp.int32, (L, L), 1)
    tril = row >= col
    bcum = _split3_dot(jnp.where(tril, 1.0, 0.0).astype(BF16), logf)
    x = logi - bcum
    xt = x.T
    m_row = m_ref[...]
    b_last = bcum[L - 1:L, :]
    m_new = jnp.maximum(b_last + m_row, b_last + jnp.max(x, axis=0, keepdims=True))
    decay = jnp.exp(b_last + m_row - m_new)
    ws_all = jnp.exp(b_last + x - m_new)
    inter_all = bcum + m_row
    ones_col = jnp.ones((L, LANES), BF16)
    nt = (((1,), (1,)), ((), ()))
    tn = (((0,), (0,)), ((), ()))

    def early(h):
        q = qk_ref[0, :, h * ML_DK:(h + 1) * ML_DK]
        k = qk_ref[0, :, (ML_HEADS + h) * ML_DK:(ML_HEADS + h + 1) * ML_DK]
        ct = ct_ref[h]
        qk = lax.dot_general(q, k, nt, preferred_element_type=F32)
        qc = jnp.dot(q, ct.astype(BF16), preferred_element_type=F32)
        dlog = jnp.where(tril, bcum[:, h:h + 1] + xt[h:h + 1, :], -jnp.inf)
        inter = inter_all[:, h:h + 1]
        mt = jnp.maximum(inter, jnp.max(dlog, axis=-1, keepdims=True))
        return k, ct, qk, qc, mt, jnp.exp(inter - mt), jnp.exp(dlog - mt)

    ahead = early(0)
    hh_all = []
    for h in range(ML_HEADS):
        k, ct, qk, qc, mt, w_inter, e = ahead
        if h + 1 < ML_HEADS:
            ahead = early(h + 1)
        vaug = jnp.concatenate([v_ref[0, :, h * ML_DV:(h + 1) * ML_DV], ones_col], axis=-1)
        num = jnp.dot((qk * e).astype(BF16), vaug, preferred_element_type=F32) + w_inter * qc
        den = jnp.maximum(jnp.abs(num[:, ML_DV:]), jnp.exp(-mt))
        hh_all.append(num[:, :ML_DV] / jnp.concatenate([den, den], axis=-1))
        wv = (ws_all[:, h:h + 1] * vaug.astype(F32)).astype(BF16)
        ct_ref[h] = decay[:, h:h + 1] * ct + lax.dot_general(k, wv, tn, preferred_element_type=F32)
    for h, hh in enumerate(hh_all):
        hs = slice(h * ML_DV, (h + 1) * ML_DV)
        hn = hh * lax.rsqrt(jnp.mean(hh * hh, axis=-1, keepdims=True) + EPS) * hn_ref[0, :, hs]
        o_ref[0, :, hs] = (og_ref[0, :, hs].astype(F32) * hn).astype(BF16)
    m_ref[...] = m_new


def _mlstm(qk, zp, gates, zs, gate_b, head_norm, lay):
    b, s, _ = qk.shape
    gb = jnp.zeros((1, 2 * LANES), F32)
    gb = gb.at[0, :ML_HEADS].set(gate_b[:ML_HEADS]).at[0, LANES:LANES + ML_HEADS].set(gate_b[ML_HEADS:])
    blk = lambda col: pl.BlockSpec((1, CHUNK, MIX_W), lambda bi, c, col=col: (bi, c, col))
    return pl.pallas_call(
        _mlstm_kernel, grid=(b, s // CHUNK),
        in_specs=[blk(0), blk(2),
                  pl.BlockSpec((1, CHUNK, 2 * LANES), lambda bi, c: (bi, c, 0)),
                  blk(0),
                  pl.BlockSpec((1, 2 * LANES), lambda bi, c: (0, 0)),
                  pl.BlockSpec((1, 1, MIX_W), lambda bi, c: (lay, 0, 0))],
        out_specs=blk(0),
        out_shape=jax.ShapeDtypeStruct((b, s, MIX_W), BF16),
        scratch_shapes=[pltpu.VMEM((ML_HEADS, ML_DK, ML_DV + LANES), F32), pltpu.VMEM((1, LANES), F32)],
        compiler_params=_params(("parallel", "arbitrary")),
    )(qk, zp, gates, zs, gb, head_norm.reshape(DEPTH, 1, MIX_W))


def _merge_epilogue(accs, e_refs, o_refs):
    acc = e_refs[0][...].astype(F32) * accs[0]
    acc += e_refs[1][...].astype(F32) * accs[1]
    acc += e_refs[2][...].astype(F32) * accs[2]
    o_refs[0][...] = acc.astype(BF16)


def _merge(ya, yb, yc, w_branch, lay, zs, tm=1024, tn=512):
    m = ya.shape[0]
    goff = MIX_W // tn
    gate = lambda k: (zs, (tm, tn), lambda i, j, k=k: (i, goff + k * (D_MODEL // tn) + j))
    return _ws_mm([ya, yb, yc], [w_branch], [(k, 0, (lay, k), 0) for k in range(N_BRANCH)],
                  [gate(k) for k in range(N_BRANCH)],
                  [((m, D_MODEL), BF16, (tm, tn), lambda i, j: (i, j))],
                  _merge_epilogue, tm=tm, tn=tn, nj=D_MODEL // tn)[0]


def _prep_mixer_weights(w_mix_in, mla_w_uq, mla_w_ukv):
    w_t = jnp.swapaxes(w_mix_in, 1, 2)
    uq = mla_w_uq.reshape(DEPTH, MLA_Q_RANK, MLA_HEADS, MLA_NOPE + MLA_ROPE)
    w_qn = uq[..., :MLA_NOPE].reshape(DEPTH, MLA_Q_RANK, MLA_HEADS * MLA_NOPE)
    w_qr = jnp.pad(uq[..., MLA_NOPE:], ((0, 0), (0, 0), (0, 0), (0, LANES - MLA_ROPE)))
    w_q = jnp.concatenate([w_qn, w_qr.reshape(DEPTH, MLA_Q_RANK, MLA_HEADS * LANES)], axis=-1).astype(BF16)
    ukv = mla_w_ukv.reshape(DEPTH, MLA_KV_RANK, MLA_HEADS, 2, MLA_NOPE)
    w_kv = ukv.transpose(0, 1, 3, 2, 4).reshape(DEPTH, MLA_KV_RANK, 2 * MLA_HEADS * MLA_NOPE).astype(BF16)
    return w_t, w_q, w_kv


def _mixer(h, lay, positions, rope_tabs, w_t, w_q, w_kv, pool_w, pool_scale, mla_q_norm, mla_kv_norm,
           ml_conv_w, ml_conv_b, ml_gate_b, ml_head_norm, w_branch, tm=1024):
    b, s, d = h.shape
    m = b * s
    h2 = h.reshape(m, d)
    cos, sina, sinb = rope_tabs
    tab = lambda a: (a, (tm, LANES), lambda i, j: (i, 0))
    tile = lambda i, j: (i, j)
    layvec = lambda a: (a.reshape(DEPTH, 1, -1), (1, 1, a.shape[-1]), lambda i, j: (lay, 0, 0))

    o = _OFF
    wide = 1024
    zp = _wst_mm(h2, w_t, lay, [], [((m, N_PLAIN), BF16, (tm, wide), tile)], _plain_epilogue,
                 tm=tm, tn=wide, nj=N_PLAIN // wide, segs=((0, o[0]), (MIX_W // wide, o[4])))[0]
    zs = _wst_mm(h2, w_t, lay, [], [((m, N_SIG), BF16, (tm, wide), tile)], _sigmoid_epilogue,
                 tm=tm, tn=wide, nj=N_SIG // wide, segs=((0, o[7]), (MIX_W // wide, o[10])))[0]
    cqn = _wst_mm(h2, w_t, lay, [layvec(mla_q_norm)], [((m, MLA_Q_RANK), BF16, (tm, MLA_Q_RANK), tile)],
                  _rmsw_epilogue, tm=tm, tn=MLA_Q_RANK, nj=1, segs=((0, o[1]),))[0]
    ckvn, kr, gates = _wst_mm(
        h2, w_t, lay, [layvec(mla_kv_norm), tab(cos), tab(sina), tab(sinb)],
        [((m, MLA_KV_RANK), BF16, (tm, MLA_KV_RANK), tile),
         ((m, LANES), BF16, (tm, LANES), tile),
         ((m, 2 * LANES), F32, (tm, 2 * LANES), tile)],
        _kvlatent_epilogue, tm=tm, tn=N_LAT, nj=1,
        parts=((o[2], MLA_KV_RANK, 0), (o[3], MLA_ROPE, MLA_KV_RANK),
               (o[8], ML_HEADS, MLA_KV_RANK + LANES), (o[9], ML_HEADS, MLA_KV_RANK + 2 * LANES)))

    zp3 = zp.reshape(b, s, N_PLAIN)
    ya = _pool(zp3, pool_w, pool_scale, lay, b, s)

    nq = MLA_HEADS * LANES
    tmq = tm // 2
    tabq = lambda a: (a, (tmq, LANES), lambda i, j: (i, 0))
    q = _mm(cqn, [(w_q, lay, 0)], [tabq(cos), tabq(sina), tabq(sinb)],
            [((m, 2 * nq), BF16, (tmq, 2 * nq), tile)], _q_epilogue, tm=tmq, tn=2 * nq, nj=1)[0]
    kv = _mm(ckvn, [(w_kv, lay, 0)], [], [((m, 2 * nq), BF16, (tm, 2 * nq), tile)],
             _plain_epilogue, tm=tm, tn=2 * nq, nj=1)[0]
    yb = _attention(q.reshape(b, s, 2 * nq), kv.reshape(b, s, 2 * nq), kr.reshape(b, s, LANES), positions)

    qk = _conv_silu(zp3, ml_conv_w, ml_conv_b, lay, b, s)
    yc = _mlstm(qk, zp3, gates.reshape(b, s, 2 * LANES), zs.reshape(b, s, N_SIG), ml_gate_b[lay],
                ml_head_norm, lay)

    return _merge(ya.reshape(m, MIX_W), yb.reshape(m, MIX_W), yc.reshape(m, MIX_W), w_branch, lay, zs)


def _q_epilogue(accs, e_refs, o_refs):
    nq = MLA_HEADS * LANES
    a = accs[0]
    o_refs[0][:, :nq] = (a[:, :nq] * Q_SCALE_LOG2).astype(BF16)
    cos_ref, sina_ref, sinb_ref = e_refs
    cos, sina, sinb = cos_ref[...], sina_ref[...], sinb_ref[...]
    for c in range(MLA_HEADS):
        sl = slice(nq + c * LANES, nq + (c + 1) * LANES)
        o_refs[0][:, sl] = _rope128(a[:, sl] * Q_SCALE_LOG2, cos, sina, sinb).astype(BF16)


def kernel(x, c, positions, w_ada, b_ada, ada_table, ffn_a_w_in, ffn_a_w_out, w_mix_in, pool_w, pool_scale, mla_q_norm, mla_w_uq, mla_kv_norm, mla_w_ukv, ml_conv_w, ml_conv_b, ml_gate_b, ml_head_norm, w_branch, w_out, ffn_b_w_in, ffn_b_w_out, final_norm):
    b, s, d = x.shape
    m = b * s
    mod = _ada(c, w_ada, b_ada, ada_table)
    rope_tabs = _rope_tables(positions)
    w_t, w_q, w_kv = _prep_mixer_weights(w_mix_in, mla_w_uq, mla_w_ukv)
    pool_wb = pool_w.astype(BF16)
    for l in range(DEPTH):
        md = mod[l]
        h = _normmod(x, md[:, 0], md[:, 1])
        x = _ffn(x, h, ffn_a_w_in, ffn_a_w_out, l, md[:, 2])
        h = _normmod(x, md[:, 3], md[:, 4])
        merged = _mixer(h, l, positions, rope_tabs, w_t, w_q, w_kv, pool_wb, pool_scale, mla_q_norm,
                        mla_kv_norm, ml_conv_w, ml_conv_b, ml_gate_b, ml_head_norm, w_branch)
        x = _resid_mm(merged, w_out, l, x.reshape(m, d), md[:, 5], 1.0, s, tm=512, tn=1024).reshape(b, s, d)
        h = _normmod(x, md[:, 6], md[:, 7])
        x = _ffn(x, h, ffn_b_w_in, ffn_b_w_out, l, md[:, 8])
    return _finalnorm(x, final_norm)
```
